```python
import math, functools
import jax, jax.numpy as jnp
from jax import lax
import numpy as np


D_MODEL = 2048
BATCH = 2
SEQ = 8192
DEPTH = 4

GRID_W = 64
CTX_LEN = 256
N_MIXERS = 2
N_MOD = 6
MLA_HEADS = 16
MLA_Q_RANK = 512
MLA_KV_RANK = 512
MLA_NOPE = 128
MLA_ROPE = 64
MLA_V = 128
GQA_HEADS = 16
GQA_KV_HEADS = 4
GQA_HEAD_DIM = 128
D_FF = 5632
CONV_W = 3

ROPE_BASE = 10000.0
EPS = 1e-6
Q_BLOCK = 128
N_MLA_LAYERS = (DEPTH + N_MIXERS - 1) // N_MIXERS
N_GQA_LAYERS = DEPTH // N_MIXERS
MLA_SCALE = 1.0 / math.sqrt(MLA_NOPE + MLA_ROPE)
GQA_SCALE = 1.0 / math.sqrt(GQA_HEAD_DIM)

kernel_name = "hybrid_mla_gqa_convffn_dit"


def rms_norm(x, g):
    xf = x.astype(jnp.float32)
    y = xf * lax.rsqrt(jnp.mean(xf * xf, axis=-1, keepdims=True) + EPS)
    return (y * g.astype(jnp.float32)).astype(x.dtype)


def modulate(x, g, shift, scale):
    return rms_norm(x, g) * (1.0 + scale) + shift


def axial_rope_tables(rows, cols, rot_dim):
    axis_dim = rot_dim // 2
    inv = jnp.power(ROPE_BASE, -jnp.arange(0, axis_dim, 2, dtype=jnp.float32) / axis_dim)
    ang_r = rows.astype(jnp.float32)[:, None] * inv
    ang_c = cols.astype(jnp.float32)[:, None] * inv
    ang = jnp.concatenate([ang_r, ang_r, ang_c, ang_c], axis=-1)
    return jnp.cos(ang), jnp.sin(ang)


def rotate_half(x):
    x1, x2 = jnp.split(x, 2, axis=-1)
    return jnp.concatenate([-x2, x1], axis=-1)


def apply_axial_rope(x, cos, sin):
    half = x.shape[-1] // 2
    rot = jnp.concatenate([rotate_half(x[..., :half]), rotate_half(x[..., half:])], axis=-1)
    return (x * cos[:, None, :] + rot * sin[:, None, :]).astype(x.dtype)


def attend(q, k, v, scale):
    B, Q, H, Dq = q.shape
    Hk = k.shape[2]
    qg = q.reshape(B, Q, Hk, H // Hk, Dq)
    s = jnp.einsum("bqkgd,btkd->bkgqt", qg, k, preferred_element_type=jnp.float32) * scale
    p = jax.nn.softmax(s, axis=-1)
    o = jnp.einsum("bkgqt,btkd->bqkgd", p.astype(v.dtype), v, preferred_element_type=jnp.float32)
    return o.reshape(B, Q, H, v.shape[-1]).astype(q.dtype)


def blocked_attention(q, k, v, scale):
    B, S, H, Dq = q.shape
    nb = S // Q_BLOCK
    qs = q.reshape(B, nb, Q_BLOCK, H, Dq).swapaxes(0, 1)
    o = lax.map(lambda qb: attend(qb, k, v, scale), qs)
    return o.swapaxes(0, 1).reshape(B, S, H, v.shape[-1])


def mla_queries(h, rope, w_dq, g_dq, w_uq, g_q_nope, g_q_pe):
    B, S, _ = h.shape
    cq = rms_norm(h @ w_dq, g_dq)
    q = (cq @ w_uq).reshape(B, S, MLA_HEADS, MLA_NOPE + MLA_ROPE)
    q_nope = rms_norm(q[..., :MLA_NOPE], g_q_nope)
    q_pe = rms_norm(q[..., MLA_NOPE:], g_q_pe)
    if rope is not None:
        q_pe = apply_axial_rope(q_pe, *rope)
    return jnp.concatenate([q_nope, q_pe], axis=-1)


def mla_keys_values(h, rope, w_dkv, g_dkv, g_k_pe, w_ukv, g_k_nope):
    B, S, _ = h.shape
    kv_a = h @ w_dkv
    c_kv = rms_norm(kv_a[..., :MLA_KV_RANK], g_dkv)
    k_pe = rms_norm(kv_a[..., MLA_KV_RANK:], g_k_pe)[:, :, None, :]
    if rope is not None:
        k_pe = apply_axial_rope(k_pe, *rope)
    kv = (c_kv @ w_ukv).reshape(B, S, MLA_HEADS, MLA_NOPE + MLA_V)
    k_nope = rms_norm(kv[..., :MLA_NOPE], g_k_nope)
    v = kv[..., MLA_NOPE:]
    k = jnp.concatenate([k_nope, jnp.broadcast_to(k_pe, (B, S, MLA_HEADS, MLA_ROPE))], axis=-1)
    return k, v


def gqa_queries(h, rope, w_q, g_q):
    B, S, _ = h.shape
    q = rms_norm((h @ w_q).reshape(B, S, GQA_HEADS, GQA_HEAD_DIM), g_q)
    if rope is not None:
        q = apply_axial_rope(q, *rope)
    return q


def gqa_keys_values(h, rope, w_kv, g_k):
    B, S, _ = h.shape
    kv = (h @ w_kv).reshape(B, S, 2, GQA_KV_HEADS, GQA_HEAD_DIM)
    k = rms_norm(kv[:, :, 0], g_k)
    v = kv[:, :, 1]
    if rope is not None:
        k = apply_axial_rope(k, *rope)
    return k, v


def depthwise_conv_centred(u, w, b):
    S = u.shape[1]
    pad = CONV_W // 2
    up = jnp.pad(u, ((0, 0), (pad, pad), (0, 0)))
    return sum(up[:, k:k + S] * w[k] for k in range(CONV_W)) + b


def conv_ffn(h, w_up, conv_w, conv_b, w_down):
    u = h @ w_up
    gate, val = u[..., :D_FF], u[..., D_FF:]
    gate = depthwise_conv_centred(gate, conv_w, conv_b)
    return (jax.nn.silu(gate) * val) @ w_down


def setup_inputs(seed: int = 0) -> dict:
    key = jax.random.key(seed)
    ks = iter(jax.random.split(key, 40))
    D, L, LA, LB = D_MODEL, DEPTH, N_MLA_LAYERS, N_GQA_LAYERS

    def nrm(shape, scale):
        return jax.random.normal(next(ks), shape, jnp.float32) * scale

    def gain(shape):
        return 1.0 + nrm(shape, 0.02)

    return {
        "x": nrm((BATCH, SEQ, D), 1.0),
        "c": nrm((BATCH, D), 1.0),
        "ctx": nrm((BATCH, CTX_LEN, D), 1.0),
        "c_ctx": nrm((D,), 1.0),
        "w_mod": nrm((L, D, N_MOD * D), 0.5 * D ** -0.5),
        "b_mod": nrm((L, N_MOD * D), 0.01),
        "norm_mix": gain((L, D)),
        "norm_ffn": gain((L, D)),
        "mla_w_dq": nrm((LA, D, MLA_Q_RANK), D ** -0.5),
        "mla_g_dq": gain((LA, MLA_Q_RANK)),
        "mla_w_uq": nrm((LA, MLA_Q_RANK, MLA_HEADS * (MLA_NOPE + MLA_ROPE)), MLA_Q_RANK ** -0.5),
        "mla_g_q_nope": gain((LA, MLA_NOPE)),
        "mla_g_q_pe": gain((LA, MLA_ROPE)),
        "mla_w_dkv": nrm((LA, D, MLA_KV_RANK + MLA_ROPE), D ** -0.5),
        "mla_g_dkv": gain((LA, MLA_KV_RANK)),
        "mla_g_k_pe": gain((LA, MLA_ROPE)),
        "mla_w_ukv": nrm((LA, MLA_KV_RANK, MLA_HEADS * (MLA_NOPE + MLA_V)), MLA_KV_RANK ** -0.5),
        "mla_g_k_nope": gain((LA, MLA_NOPE)),
        "mla_w_o": nrm((LA, MLA_HEADS * MLA_V, D), (MLA_HEADS * MLA_V) ** -0.5),
        "gqa_w_q": nrm((LB, D, GQA_HEADS * GQA_HEAD_DIM), D ** -0.5),
        "gqa_g_q": gain((LB, GQA_HEAD_DIM)),
        "gqa_w_kv": nrm((LB, D, 2 * GQA_KV_HEADS * GQA_HEAD_DIM), D ** -0.5),
        "gqa_g_k": gain((LB, GQA_HEAD_DIM)),
        "gqa_w_o": nrm((LB, GQA_HEADS * GQA_HEAD_DIM, D), (GQA_HEADS * GQA_HEAD_DIM) ** -0.5),
        "ffn_w_up": nrm((L, D, 2 * D_FF), D ** -0.5),
        "ffn_conv_w": nrm((L, CONV_W, D_FF), CONV_W ** -0.5),
        "ffn_conv_b": nrm((L, D_FF), 0.01),
        "ffn_w_down": nrm((L, D_FF, D), D_FF ** -0.5),
    }


def reference(x, c, ctx, c_ctx, w_mod, b_mod, norm_mix, norm_ffn,
              mla_w_dq, mla_g_dq, mla_w_uq, mla_g_q_nope, mla_g_q_pe,
              mla_w_dkv, mla_g_dkv, mla_g_k_pe, mla_w_ukv, mla_g_k_nope, mla_w_o,
              gqa_w_q, gqa_g_q, gqa_w_kv, gqa_g_k, gqa_w_o,
              ffn_w_up, ffn_conv_w, ffn_conv_b, ffn_w_down):
    B, S, _ = x.shape
    C = ctx.shape[1]
    ROWS = S // GRID_W
    rows = jnp.repeat(jnp.arange(ROWS, dtype=jnp.int32), GRID_W)
    cols = jnp.tile(jnp.arange(GRID_W, dtype=jnp.int32), ROWS)
    rope_mla = axial_rope_tables(rows, cols, MLA_ROPE)
    rope_gqa = axial_rope_tables(rows, cols, GQA_HEAD_DIM)
    silu_c = jax.nn.silu(c)
    silu_cc = jax.nn.silu(c_ctx)

    for i in range(DEPTH):
        last = i == DEPTH - 1
        j = i // N_MIXERS
        mod = (silu_c @ w_mod[i] + b_mod[i])[:, None, :]
        mod_c = silu_cc @ w_mod[i] + b_mod[i]
        sh1, sc1, g1, sh2, sc2, g2 = jnp.split(mod, N_MOD, axis=-1)
        csh1, csc1, cg1, csh2, csc2, cg2 = jnp.split(mod_c, N_MOD, axis=-1)

        h = modulate(x, norm_mix[i], sh1, sc1)
        hc = modulate(ctx, norm_mix[i], csh1, csc1)
        if i % N_MIXERS == 0:
            q_fn = functools.partial(mla_queries, w_dq=mla_w_dq[j], g_dq=mla_g_dq[j], w_uq=mla_w_uq[j],
                                     g_q_nope=mla_g_q_nope[j], g_q_pe=mla_g_q_pe[j])
            kv_fn = functools.partial(mla_keys_values, w_dkv=mla_w_dkv[j], g_dkv=mla_g_dkv[j],
                                      g_k_pe=mla_g_k_pe[j], w_ukv=mla_w_ukv[j], g_k_nope=mla_g_k_nope[j])
            w_o, rope, scale = mla_w_o[j], rope_mla, MLA_SCALE
        else:
            q_fn = functools.partial(gqa_queries, w_q=gqa_w_q[j], g_q=gqa_g_q[j])
            kv_fn = functools.partial(gqa_keys_values, w_kv=gqa_w_kv[j], g_k=gqa_g_k[j])
            w_o, rope, scale = gqa_w_o[j], rope_gqa, GQA_SCALE

        k_lat, v_lat = kv_fn(h, rope)
        k_ctx, v_ctx = kv_fn(hc, None)
        o = blocked_attention(q_fn(h, rope),
                              jnp.concatenate([k_lat, k_ctx], axis=1),
                              jnp.concatenate([v_lat, v_ctx], axis=1), scale)
        x = x + g1 * (o.reshape(B, S, -1) @ w_o)
        if not last:
            oc = attend(q_fn(hc, None), k_ctx, v_ctx, scale)
            ctx = ctx + cg1 * (oc.reshape(B, C, -1) @ w_o)

        x = x + g2 * conv_ffn(modulate(x, norm_ffn[i], sh2, sc2),
                              ffn_w_up[i], ffn_conv_w[i], ffn_conv_b[i], ffn_w_down[i])
        if not last:
            ctx = ctx + cg2 * conv_ffn(modulate(ctx, norm_ffn[i], csh2, csc2),
                                       ffn_w_up[i], ffn_conv_w[i], ffn_conv_b[i], ffn_w_down[i])
    return x
```

```python
import functools
import math

import jax
import jax.numpy as jnp
from jax import lax
from jax.experimental import pallas as pl
from jax.experimental.pallas import tpu as pltpu

F32 = jnp.float32
BF16 = jnp.bfloat16

GRID_W = 64
N_MOD = 6
MLA_HEADS = 16
MLA_Q_RANK = 512
MLA_KV_RANK = 512
MLA_NOPE = 128
MLA_ROPE = 64
MLA_V = 128
MLA_QK_PAD = 256
GQA_HEADS = 16
GQA_KV_HEADS = 4
GQA_HEAD_DIM = 128
CONV_W = 3
ROPE_BASE = 10000.0
EPS = 1e-6
LOG2E = 1.4426950408889634
MLA_SCALE = 1.0 / math.sqrt(MLA_NOPE + MLA_ROPE)
GQA_SCALE = 1.0 / math.sqrt(GQA_HEAD_DIM)

LANES = 128
BF16_SUBLANES = 16
VMEM_LIMIT_BYTES = 56 * 1024 * 1024

ROW_TILE = 512
FFN_UP_ROW_TILE = 1024
FFN_UP_COL_TILE = 512
FFN_DOWN_ROW_TILE = 512
FFN_DOWN_K_TILE = 512
ATTN_Q_TILE = 256
ATTN_K_CHUNK = 512
MOD_COL_TILE = 1024
HALO = BF16_SUBLANES


def _cparams(sem):
    return pltpu.CompilerParams(dimension_semantics=sem, vmem_limit_bytes=VMEM_LIMIT_BYTES)


def _dot(a, b):
    return jnp.dot(a, b, preferred_element_type=F32)


def _rms(y, n):
    return y * lax.rsqrt(jnp.sum(y * y, axis=-1, keepdims=True) * (1.0 / n) + EPS)


def _modulate(x, g, shift, scale):
    return (_rms(x, x.shape[-1]) * g) * (1.0 + scale) + shift


def _rope(x, cos, sin_a, sin_b, quarter):
    return (x * cos + pltpu.roll(x, LANES - quarter, 1) * sin_a
            + pltpu.roll(x, quarter, 1) * sin_b)


def _mod_kernel(c_ref, w_ref, b_ref, o_ref):
    c = c_ref[...]
    sc = (c * jax.nn.sigmoid(c)).astype(BF16)
    o_ref[0] = _dot(sc, w_ref[0].astype(BF16)) + b_ref[0]


def _mod_all(cvec, w_mod, b_mod):
    L, D, N = w_mod.shape
    R = cvec.shape[0]
    return pl.pallas_call(
        _mod_kernel,
        grid=(L, N // MOD_COL_TILE),
        in_specs=[
            pl.BlockSpec((R, D), lambda l, j: (0, 0)),
            pl.BlockSpec((1, D, MOD_COL_TILE), lambda l, j: (l, 0, j)),
            pl.BlockSpec((1, 1, MOD_COL_TILE), lambda l, j: (l, 0, j)),
        ],
        out_specs=pl.BlockSpec((1, R, MOD_COL_TILE), lambda l, j: (l, 0, j)),
        out_shape=jax.ShapeDtypeStruct((L, R, N), F32),
        compiler_params=_cparams(("parallel", "parallel")),
        name="mod_all",
    )(cvec, w_mod, b_mod.reshape(L, 1, N))


def _mla_qkv_kernel(x_ref, g_ref, sh_ref, sc_ref, w1_ref, g1_ref, wuq_ref, gq_ref,
                    wukv_ref, gk_ref, cos_ref, sa_ref, sb_ref,
                    q_ref, k_ref, vt_ref):
    h = _modulate(x_ref[0], g_ref[...], sh_ref[0], sc_ref[0]).astype(BF16)
    y1 = _dot(h, w1_ref[...])
    g1 = g1_ref[...]
    cq = (_rms(y1[:, :MLA_Q_RANK], MLA_Q_RANK) * g1[:, :MLA_Q_RANK]).astype(BF16)
    lo, hi = MLA_Q_RANK, MLA_Q_RANK + MLA_KV_RANK
    ckv = (_rms(y1[:, lo:hi], MLA_KV_RANK) * g1[:, lo:hi]).astype(BF16)
    cos, sa, sb = cos_ref[...], sa_ref[...], sb_ref[...]
    quarter = MLA_ROPE // 4
    kpe = _rms(y1[:, hi:], MLA_ROPE) * g1[:, hi:]
    kpe = _rope(kpe, cos, sa, sb, quarter).astype(BF16)
    gq = gq_ref[...]
    gk = gk_ref[...]
    qscale = MLA_SCALE * LOG2E
    for hh in range(MLA_HEADS):
        c0 = hh * MLA_QK_PAD
        qh = _dot(cq, wuq_ref[:, c0:c0 + MLA_QK_PAD])
        qn = _rms(qh[:, :MLA_NOPE], MLA_NOPE) * gq[:, :MLA_NOPE]
        qp = _rms(qh[:, MLA_NOPE:], MLA_ROPE) * gq[:, MLA_NOPE:]
        qp = _rope(qp, cos, sa, sb, quarter)
        q_ref[0, :, c0:c0 + MLA_NOPE] = (qn * qscale).astype(BF16)
        q_ref[0, :, c0 + MLA_NOPE:c0 + MLA_QK_PAD] = (qp * qscale).astype(BF16)
        kvh = _dot(ckv, wukv_ref[:, c0:c0 + MLA_NOPE + MLA_V])
        kn = _rms(kvh[:, :MLA_NOPE], MLA_NOPE) * gk
        k_ref[0, :, c0:c0 + MLA_NOPE] = kn.astype(BF16)
        k_ref[0, :, c0 + MLA_NOPE:c0 + MLA_QK_PAD] = kpe
        vt_ref[0, hh] = kvh[:, MLA_NOPE:].T.astype(BF16)


def _mla_qkv(x, g, shift, scale, w, rope, tm, rope_blk0):
    B, R, D = x.shape
    H = MLA_HEADS
    nq = H * MLA_QK_PAD
    const2 = lambda b, i: (0, 0)
    mod_map = (lambda b, i: (b, 0, 0)) if shift.shape[0] == B else (lambda b, i: (0, 0, 0))
    rope_spec = pl.BlockSpec((tm, LANES), lambda b, i: (rope_blk0 + i, 0))
    return pl.pallas_call(
        _mla_qkv_kernel,
        grid=(B, R // tm),
        in_specs=[
            pl.BlockSpec((1, tm, D), lambda b, i: (b, i, 0)),
            pl.BlockSpec((1, D), const2),
            pl.BlockSpec((1, 1, D), mod_map),
            pl.BlockSpec((1, 1, D), mod_map),
            pl.BlockSpec(w["w1"].shape, const2),
            pl.BlockSpec(w["g1"].shape, const2),
            pl.BlockSpec(w["wuq"].shape, const2),
            pl.BlockSpec(w["gq"].shape, const2),
            pl.BlockSpec(w["wukv"].shape, const2),
            pl.BlockSpec(w["gk"].shape, const2),
            rope_spec, rope_spec, rope_spec,
        ],
        out_specs=[
            pl.BlockSpec((1, tm, nq), lambda b, i: (b, i, 0)),
            pl.BlockSpec((1, tm, nq), lambda b, i: (b, i, 0)),
            pl.BlockSpec((1, H, MLA_V, tm), lambda b, i: (b, 0, 0, i)),
        ],
        out_shape=[
            jax.ShapeDtypeStruct((B, R, nq), BF16),
            jax.ShapeDtypeStruct((B, R, nq), BF16),
            jax.ShapeDtypeStruct((B, H, MLA_V, R), BF16),
        ],
        compiler_params=_cparams(("parallel", "parallel")),
        name="mla_qkv",
    )(x, g, shift, scale, w["w1"], w["g1"], w["wuq"], w["gq"], w["wukv"], w["gk"], *rope)


def _gqa_qkv_kernel(x_ref, g_ref, sh_ref, sc_ref, w_ref, gq_ref, gk_ref,
                    cos_ref, sa_ref, sb_ref, q_ref, k_ref, vt_ref):
    h = _modulate(x_ref[0], g_ref[...], sh_ref[0], sc_ref[0]).astype(BF16)
    cos, sa, sb = cos_ref[...], sa_ref[...], sb_ref[...]
    quarter = GQA_HEAD_DIM // 4
    hd = GQA_HEAD_DIM
    group = GQA_KV_HEADS * hd
    nq = GQA_HEADS * hd
    gq = gq_ref[...] * (GQA_SCALE * LOG2E)
    gk = gk_ref[...]
    for c0 in range(0, nq, group):
        y = _dot(h, w_ref[:, c0:c0 + group])
        for j in range(0, group, hd):
            qh = _rope(_rms(y[:, j:j + hd], hd) * gq, cos, sa, sb, quarter)
            q_ref[0, :, c0 + j:c0 + j + hd] = qh.astype(BF16)
    y = _dot(h, w_ref[:, nq:nq + group])
    for j in range(0, group, hd):
        kh = _rope(_rms(y[:, j:j + hd], hd) * gk, cos, sa, sb, quarter)
        k_ref[0, :, j:j + hd] = kh.astype(BF16)
    y = _dot(h, w_ref[:, nq + group:nq + 2 * group])
    for j in range(GQA_KV_HEADS):
        vt_ref[0, j] = y[:, j * hd:(j + 1) * hd].T.astype(BF16)


def _gqa_qkv(x, g, shift, scale, w, rope, tm, rope_blk0):
    B, R, D = x.shape
    nq = GQA_HEADS * GQA_HEAD_DIM
    nk = GQA_KV_HEADS * GQA_HEAD_DIM
    const2 = lambda b, i: (0, 0)
    mod_map = (lambda b, i: (b, 0, 0)) if shift.shape[0] == B else (lambda b, i: (0, 0, 0))
    rope_spec = pl.BlockSpec((tm, LANES), lambda b, i: (rope_blk0 + i, 0))
    return pl.pallas_call(
        _gqa_qkv_kernel,
        grid=(B, R // tm),
        in_specs=[
            pl.BlockSpec((1, tm, D), lambda b, i: (b, i, 0)),
            pl.BlockSpec((1, D), const2),
            pl.BlockSpec((1, 1, D), mod_map),
            pl.BlockSpec((1, 1, D), mod_map),
            pl.BlockSpec(w["w"].shape, const2),
            pl.BlockSpec(w["gq"].shape, const2),
            pl.BlockSpec(w["gk"].shape, const2),
            rope_spec, rope_spec, rope_spec,
        ],
        out_specs=[
            pl.BlockSpec((1, tm, nq), lambda b, i: (b, i, 0)),
            pl.BlockSpec((1, tm, nk), lambda b, i: (b, i, 0)),
            pl.BlockSpec((1, GQA_KV_HEADS, GQA_HEAD_DIM, tm), lambda b, i: (b, 0, 0, i)),
        ],
        out_shape=[
            jax.ShapeDtypeStruct((B, R, nq), BF16),
            jax.ShapeDtypeStruct((B, R, nk), BF16),
            jax.ShapeDtypeStruct((B, GQA_KV_HEADS, GQA_HEAD_DIM, R), BF16),
        ],
        compiler_params=_cparams(("parallel", "parallel")),
        name="gqa_qkv",
    )(x, g, shift, scale, w["w"], w["gq"], w["gk"], *rope)


def _attn_kernel(*refs, chunks):
    q_ref, o_ref = refs[0], refs[-1]
    kv_refs = refs[1:-1]
    q = q_ref[0]
    m = l = acc = None
    for src, start, size in chunks:
        k_c = kv_refs[2 * src][0, start:start + size, :]
        vt_c = kv_refs[2 * src + 1][0, 0, :, start:start + size]
        s = lax.dot_general(k_c, q, (((1,), (1,)), ((), ())), preferred_element_type=F32)
        m_c = jnp.max(s, axis=0, keepdims=True)
        if m is None:
            m = m_c
            p = jnp.exp2(s - m)
            l = jnp.sum(p, axis=0, keepdims=True)
            acc = _dot(vt_c, p.astype(BF16))
        else:
            m_new = jnp.maximum(m, m_c)
            alpha = jnp.exp2(m - m_new)
            p = jnp.exp2(s - m_new)
            l = alpha * l + jnp.sum(p, axis=0, keepdims=True)
            acc = alpha * acc + _dot(vt_c, p.astype(BF16))
            m = m_new
    o_ref[0] = (acc / l).T.astype(o_ref.dtype)


def _attention(q, kv_sources, n_heads, n_kv_heads, dq, dv, tq):
    B, Q, _ = q.shape
    G = n_heads // n_kv_heads
    chunks = []
    in_specs = [pl.BlockSpec((1, tq, dq), lambda b, hk, g, i: (b, i, hk * G + g))]
    args = [q]
    for src, (k, vt) in enumerate(kv_sources):
        T = k.shape[1]
        ck = min(ATTN_K_CHUNK, T)
        chunks += [(src, s, ck) for s in range(0, T, ck)]
        in_specs += [
            pl.BlockSpec((1, T, dq), lambda b, hk, g, i: (b, 0, hk)),
            pl.BlockSpec((1, 1, dv, T), lambda b, hk, g, i: (b, hk, 0, 0)),
        ]
        args += [k, vt]
    return pl.pallas_call(
        functools.partial(_attn_kernel, chunks=tuple(chunks)),
        grid=(B, n_kv_heads, G, Q // tq),
        in_specs=in_specs,
        out_specs=pl.BlockSpec((1, tq, dv), lambda b, hk, g, i: (b, i, hk * G + g)),
        out_shape=jax.ShapeDtypeStruct((B, Q, n_heads * dv), BF16),
        compiler_params=_cparams(("parallel", "parallel", "parallel", "parallel")),
        name="attention",
    )(*args)


def _oproj_kernel(o_ref, w_ref, x_ref, gate_ref, out_ref, *, col_tile):
    o = o_ref[0]
    gate = gate_ref[0]
    for c0 in range(0, out_ref.shape[-1], col_tile):
        y = _dot(o, w_ref[:, c0:c0 + col_tile])
        out_ref[0, :, c0:c0 + col_tile] = x_ref[0, :, c0:c0 + col_tile] + gate[:, c0:c0 + col_tile] * y


def _oproj(o, w_o, x, gate, tm):
    B, R, D = x.shape
    K = o.shape[-1]
    mod_map = (lambda b, i: (b, 0, 0)) if gate.shape[0] == B else (lambda b, i: (0, 0, 0))
    return pl.pallas_call(
        functools.partial(_oproj_kernel, col_tile=512),
        grid=(B, R // tm),
        in_specs=[
            pl.BlockSpec((1, tm, K), lambda b, i: (b, i, 0)),
            pl.BlockSpec((K, D), lambda b, i: (0, 0)),
            pl.BlockSpec((1, tm, D), lambda b, i: (b, i, 0)),
            pl.BlockSpec((1, 1, D), mod_map),
        ],
        out_specs=pl.BlockSpec((1, tm, D), lambda b, i: (b, i, 0)),
        out_shape=jax.ShapeDtypeStruct((B, R, D), F32),
        compiler_params=_cparams(("parallel", "parallel")),
        name="oproj",
    )(o, w_o, x, gate)


def _ffn_up_kernel(x_ref, g_ref, sh_ref, sc_ref, w_ref, u_ref, h_ref):
    @pl.when(pl.program_id(2) == 0)
    def _():
        h_ref[...] = _modulate(x_ref[0], g_ref[...], sh_ref[0], sc_ref[0]).astype(BF16)

    u_ref[0] = _dot(h_ref[...], w_ref[...]).astype(u_ref.dtype)


def _ffn_up(x, g, shift, scale, w_up, tm, tn):
    B, R, D = x.shape
    N = w_up.shape[1]
    mod_map = (lambda b, i, j: (b, 0, 0)) if shift.shape[0] == B else (lambda b, i, j: (0, 0, 0))
    return pl.pallas_call(
        _ffn_up_kernel,
        grid=(B, R // tm, N // tn),
        in_specs=[
            pl.BlockSpec((1, tm, D), lambda b, i, j: (b, i, 0)),
            pl.BlockSpec((1, D), lambda b, i, j: (0, 0)),
            pl.BlockSpec((1, 1, D), mod_map),
            pl.BlockSpec((1, 1, D), mod_map),
            pl.BlockSpec((D, tn), lambda b, i, j: (0, j)),
        ],
        out_specs=pl.BlockSpec((1, tm, tn), lambda b, i, j: (b, i, j)),
        out_shape=jax.ShapeDtypeStruct((B, R, N), BF16),
        scratch_shapes=[pltpu.VMEM((tm, D), BF16)],
        compiler_params=_cparams(("parallel", "parallel", "arbitrary")),
        name="ffn_up",
    )(x, g, shift, scale, w_up)


def _ffn_down_kernel(ug_ref, uv_ref, prev_ref, next_ref, cw_ref, cb_ref, w_ref, x_ref, gate_ref,
                     out_ref, win_ref):
    i = pl.program_id(1)
    kk = pl.program_id(2)
    tm = ug_ref.shape[1]
    first = i == 0
    last = i == pl.num_programs(1) - 1
    prev_row = prev_ref[0, HALO - 1:HALO, :].astype(F32)
    next_row = next_ref[0, 0:1, :].astype(F32)
    win_ref[HALO - 1:HALO, :] = jnp.where(first, 0.0, prev_row)
    win_ref[HALO + tm:HALO + tm + 1, :] = jnp.where(last, 0.0, next_row)
    win_ref[HALO:HALO + tm, :] = ug_ref[0].astype(F32)
    cw = cw_ref[...]
    conv = (win_ref[HALO - 1:HALO - 1 + tm, :] * cw[0:1, :]
            + win_ref[HALO:HALO + tm, :] * cw[1:2, :]
            + win_ref[HALO + 1:HALO + 1 + tm, :] * cw[2:3, :]
            + cb_ref[...])
    a = (conv * jax.nn.sigmoid(conv)) * uv_ref[0].astype(F32)
    y = _dot(a.astype(BF16), w_ref[...])

    @pl.when(kk == 0)
    def _():
        out_ref[0] = y

    @pl.when(kk > 0)
    def _():
        out_ref[0] += y

    @pl.when(kk == pl.num_programs(2) - 1)
    def _():
        out_ref[0] = x_ref[0] + gate_ref[0] * out_ref[0]


def _ffn_down(u, conv_w, conv_b, w_down, x, gate, tm, tk):
    B, R, D = x.shape
    F = w_down.shape[0]
    nk = F // tk
    rb = tm // HALO
    n_halo_blocks = R // HALO
    mod_map = (lambda b, i, k: (b, 0, 0)) if gate.shape[0] == B else (lambda b, i, k: (0, 0, 0))
    return pl.pallas_call(
        _ffn_down_kernel,
        grid=(B, R // tm, nk),
        in_specs=[
            pl.BlockSpec((1, tm, tk), lambda b, i, k: (b, i, k)),
            pl.BlockSpec((1, tm, tk), lambda b, i, k: (b, i, nk + k)),
            pl.BlockSpec((1, HALO, tk), lambda b, i, k: (b, jnp.maximum(i * rb - 1, 0), k)),
            pl.BlockSpec((1, HALO, tk),
                         lambda b, i, k: (b, jnp.minimum((i + 1) * rb, n_halo_blocks - 1), k)),
            pl.BlockSpec((CONV_W, tk), lambda b, i, k: (0, k)),
            pl.BlockSpec((1, tk), lambda b, i, k: (0, k)),
            pl.BlockSpec((tk, D), lambda b, i, k: (k, 0)),
            pl.BlockSpec((1, tm, D), lambda b, i, k: (b, i, 0)),
            pl.BlockSpec((1, 1, D), mod_map),
        ],
        out_specs=pl.BlockSpec((1, tm, D), lambda b, i, k: (b, i, 0)),
        out_shape=jax.ShapeDtypeStruct((B, R, D), F32),
        scratch_shapes=[pltpu.VMEM((tm + 2 * HALO, tk), F32)],
        compiler_params=_cparams(("parallel", "parallel", "arbitrary")),
        name="ffn_down",
    )(u, u, u, u, conv_w, conv_b.reshape(1, F), w_down, x, gate)


def _rope_tables(seq, ctx_len, rot_dim):
    t = jnp.arange(seq, dtype=jnp.int32)
    rows = (t // GRID_W).astype(F32)
    cols = (t % GRID_W).astype(F32)
    axis_dim = rot_dim // 2
    inv = jnp.power(ROPE_BASE, -jnp.arange(0, axis_dim, 2, dtype=F32) / axis_dim)
    ang_r = rows[:, None] * inv
    ang_c = cols[:, None] * inv
    ang = jnp.concatenate([ang_r, ang_r, ang_c, ang_c], axis=-1)
    cos, sin = jnp.cos(ang), jnp.sin(ang)
    lane = jnp.arange(rot_dim)
    first_quarter = (lane % axis_dim) < (axis_dim // 2)
    sin_a = jnp.where(first_quarter, -sin, 0.0)
    sin_b = jnp.where(first_quarter, 0.0, sin)

    def finish(tab, ctx_value):
        tab = jnp.concatenate([tab, jnp.full((ctx_len, rot_dim), ctx_value, F32)], axis=0)
        return jnp.pad(tab, ((0, 0), (0, LANES - rot_dim)))

    return finish(cos, 1.0), finish(sin_a, 0.0), finish(sin_b, 0.0)


def _mla_weights(j, w_dq, g_dq, w_uq, g_q_nope, g_q_pe, w_dkv, g_dkv, g_k_pe, w_ukv, g_k_nope):
    D = w_dq.shape[1]
    pad = LANES - MLA_ROPE
    w1 = jnp.concatenate([w_dq[j], w_dkv[j], jnp.zeros((D, pad), F32)], axis=1).astype(BF16)
    g1 = jnp.concatenate([g_dq[j], g_dkv[j], g_k_pe[j], jnp.zeros((pad,), F32)])[None, :]
    wuq = w_uq[j].reshape(MLA_Q_RANK, MLA_HEADS, MLA_NOPE + MLA_ROPE)
    wuq = jnp.pad(wuq, ((0, 0), (0, 0), (0, MLA_QK_PAD - MLA_NOPE - MLA_ROPE)))
    wuq = wuq.reshape(MLA_Q_RANK, MLA_HEADS * MLA_QK_PAD).astype(BF16)
    gq = jnp.concatenate([g_q_nope[j], g_q_pe[j], jnp.zeros((pad,), F32)])[None, :]
    return dict(w1=w1, g1=g1, wuq=wuq, gq=gq, wukv=w_ukv[j].astype(BF16), gk=g_k_nope[j][None, :])


def _gqa_weights(j, w_q, g_q, w_kv, g_k):
    w = jnp.concatenate([w_q[j], w_kv[j]], axis=1).astype(BF16)
    return dict(w=w, gq=g_q[j][None, :], gk=g_k[j][None, :])


def kernel(x, c, ctx, c_ctx, w_mod, b_mod, norm_mix, norm_ffn, mla_w_dq, mla_g_dq, mla_w_uq, mla_g_q_nope, mla_g_q_pe, mla_w_dkv, mla_g_dkv, mla_g_k_pe, mla_w_ukv, mla_g_k_nope, mla_w_o, gqa_w_q, gqa_g_q, gqa_w_kv, gqa_g_k, gqa_w_o, ffn_w_up, ffn_conv_w, ffn_conv_b, ffn_w_down):
    B, S, D = x.shape
    C = ctx.shape[1]
    depth = w_mod.shape[0]
    assert S % FFN_UP_ROW_TILE == 0 and S % ROW_TILE == 0 and C % HALO == 0 and C <= ROW_TILE

    cvec = jnp.concatenate([c, c_ctx[None, :], jnp.zeros((8 - B - 1, D), F32)], axis=0)
    mod = _mod_all(cvec, w_mod, b_mod).reshape(depth, 8, N_MOD, D)
    rope_mla = _rope_tables(S, C, MLA_ROPE)
    rope_gqa = _rope_tables(S, C, GQA_HEAD_DIM)

    for i in range(depth):
        last = i == depth - 1
        j = i // 2
        lat = [mod[i, :B, n][:, None, :] for n in range(N_MOD)]
        cmod = [mod[i, B:B + 1, n][:, None, :] for n in range(N_MOD)]
        g_mix = norm_mix[i][None, :]
        g_ffn = norm_ffn[i][None, :]

        if i % 2 == 0:
            w = _mla_weights(j, mla_w_dq, mla_g_dq, mla_w_uq, mla_g_q_nope, mla_g_q_pe,
                             mla_w_dkv, mla_g_dkv, mla_g_k_pe, mla_w_ukv, mla_g_k_nope)
            qkv, rope = _mla_qkv, rope_mla
            w_o = mla_w_o[j].astype(BF16)
            heads, kv_heads, dq, dv = MLA_HEADS, MLA_HEADS, MLA_QK_PAD, MLA_V
        else:
            w = _gqa_weights(j, gqa_w_q, gqa_g_q, gqa_w_kv, gqa_g_k)
            qkv, rope = _gqa_qkv, rope_gqa
            w_o = gqa_w_o[j].astype(BF16)
            heads, kv_heads, dq, dv = GQA_HEADS, GQA_KV_HEADS, GQA_HEAD_DIM, GQA_HEAD_DIM

        q, k, vt = qkv(x, g_mix, lat[0], lat[1], w, rope, ROW_TILE, 0)
        qc, kc, vtc = qkv(ctx, g_mix, cmod[0], cmod[1], w, rope, C, S // C)
        o = _attention(q, [(k, vt), (kc, vtc)], heads, kv_heads, dq, dv, ATTN_Q_TILE)
        x = _oproj(o, w_o, x, lat[2], ROW_TILE)
        if not last:
            oc = _attention(qc, [(kc, vtc)], heads, kv_heads, dq, dv, C)
            ctx = _oproj(oc, w_o, ctx, cmod[2], C)

        w_up = ffn_w_up[i].astype(BF16)
        w_down = ffn_w_down[i].astype(BF16)
        u = _ffn_up(x, g_ffn, lat[3], lat[4], w_up, FFN_UP_ROW_TILE, FFN_UP_COL_TILE)
        x = _ffn_down(u, ffn_conv_w[i], ffn_conv_b[i], w_down, x, lat[5],
                      FFN_DOWN_ROW_TILE, FFN_DOWN_K_TILE)
        if not last:
            uc = _ffn_up(ctx, g_ffn, cmod[3], cmod[4], w_up, C, FFN_UP_COL_TILE)
            ctx = _ffn_down(uc, ffn_conv_w[i], ffn_conv_b[i], w_down, ctx, cmod[5],
                            C, FFN_DOWN_K_TILE)
    return x
```

```python
import functools
import math

import jax
import jax.numpy as jnp
from jax import lax
from jax.experimental import pallas as pl
from jax.experimental.pallas import tpu as pltpu

F32 = jnp.float32
BF16 = jnp.bfloat16

GRID_W = 64
N_MOD = 6
MLA_HEADS = 16
MLA_Q_RANK = 512
MLA_KV_RANK = 512
MLA_NOPE = 128
MLA_ROPE = 64
MLA_V = 128
MLA_QK_PAD = 256
GQA_HEADS = 16
GQA_KV_HEADS = 4
GQA_HEAD_DIM = 128
CONV_W = 3
ROPE_BASE = 10000.0
EPS = 1e-6
LOG2E = 1.4426950408889634
MLA_SCALE = 1.0 / math.sqrt(MLA_NOPE + MLA_ROPE)
GQA_SCALE = 1.0 / math.sqrt(GQA_HEAD_DIM)

LANES = 128
BF16_SUBLANES = 16
VMEM_LIMIT_BYTES = 56 * 1024 * 1024

ROW_TILE = 512
FFN_UP_ROW_TILE = 1024
FFN_UP_COL_TILE = 512
FFN_DOWN_ROW_TILE = 512
FFN_DOWN_K_TILE = 512
ATTN_Q_TILE = 512
ATTN_K_CHUNK = 512
MOD_COL_TILE = 1024
HALO = BF16_SUBLANES


def _cparams(sem):
    return pltpu.CompilerParams(dimension_semantics=sem, vmem_limit_bytes=VMEM_LIMIT_BYTES)


def _dot(a, b):
    return jnp.dot(a, b, preferred_element_type=F32)


def _rms(y, n):
    return y * lax.rsqrt(jnp.sum(y * y, axis=-1, keepdims=True) * (1.0 / n) + EPS)


def _modulate(x, g, shift, scale):
    return (_rms(x, x.shape[-1]) * g) * (1.0 + scale) + shift


def _rope(x, cos, sin_a, sin_b, quarter):
    return (x * cos + pltpu.roll(x, LANES - quarter, 1) * sin_a
            + pltpu.roll(x, quarter, 1) * sin_b)


def _mod_kernel(c_ref, w_ref, b_ref, o_ref):
    c = c_ref[...]
    sc = (c * jax.nn.sigmoid(c)).astype(BF16)
    o_ref[0] = _dot(sc, w_ref[0].astype(BF16)) + b_ref[0]


def _mod_all(cvec, w_mod, b_mod):
    L, D, N = w_mod.shape
    R = cvec.shape[0]
    return pl.pallas_call(
        _mod_kernel,
        grid=(L, N // MOD_COL_TILE),
        in_specs=[
            pl.BlockSpec((R, D), lambda l, j: (0, 0)),
            pl.BlockSpec((1, D, MOD_COL_TILE), lambda l, j: (l, 0, j)),
            pl.BlockSpec((1, 1, MOD_COL_TILE), lambda l, j: (l, 0, j)),
        ],
        out_specs=pl.BlockSpec((1, R, MOD_COL_TILE), lambda l, j: (l, 0, j)),
        out_shape=jax.ShapeDtypeStruct((L, R, N), F32),
        compiler_params=_cparams(("parallel", "parallel")),
        name="mod_all",
    )(cvec, w_mod, b_mod.reshape(L, 1, N))


def _mla_qkv_kernel(x_ref, g_ref, sh_ref, sc_ref, w1_ref, g1_ref, wuq_ref, gq_ref,
                    wukv_ref, gk_ref, cos_ref, sa_ref, sb_ref,
                    q_ref, k_ref, vt_ref):
    h = _modulate(x_ref[0], g_ref[...], sh_ref[0], sc_ref[0]).astype(BF16)
    y1 = _dot(h, w1_ref[...])
    g1 = g1_ref[...]
    cq = (_rms(y1[:, :MLA_Q_RANK], MLA_Q_RANK) * g1[:, :MLA_Q_RANK]).astype(BF16)
    lo, hi = MLA_Q_RANK, MLA_Q_RANK + MLA_KV_RANK
    ckv = (_rms(y1[:, lo:hi], MLA_KV_RANK) * g1[:, lo:hi]).astype(BF16)
    cos, sa, sb = cos_ref[...], sa_ref[...], sb_ref[...]
    quarter = MLA_ROPE // 4
    kpe = _rms(y1[:, hi:], MLA_ROPE) * g1[:, hi:]
    kpe = _rope(kpe, cos, sa, sb, quarter).astype(BF16)
    gq = gq_ref[...]
    gk = gk_ref[...]
    qscale = MLA_SCALE * LOG2E
    for hh in range(MLA_HEADS):
        c0 = hh * MLA_QK_PAD
        qh = _dot(cq, wuq_ref[:, c0:c0 + MLA_QK_PAD])
        qn = _rms(qh[:, :MLA_NOPE], MLA_NOPE) * gq[:, :MLA_NOPE]
        qp = _rms(qh[:, MLA_NOPE:], MLA_ROPE) * gq[:, MLA_NOPE:]
        qp = _rope(qp, cos, sa, sb, quarter)
        q_ref[0, :, c0:c0 + MLA_NOPE] = (qn * qscale).astype(BF16)
        q_ref[0, :, c0 + MLA_NOPE:c0 + MLA_QK_PAD] = (qp * qscale).astype(BF16)
        kvh = _dot(ckv, wukv_ref[:, c0:c0 + MLA_NOPE + MLA_V])
        kn = _rms(kvh[:, :MLA_NOPE], MLA_NOPE) * gk
        k_ref[0, :, c0:c0 + MLA_NOPE] = kn.astype(BF16)
        k_ref[0, :, c0 + MLA_NOPE:c0 + MLA_QK_PAD] = kpe
        vt_ref[0, hh] = kvh[:, MLA_NOPE:].T.astype(BF16)


def _mla_qkv(x, g, shift, scale, w, rope, tm, rope_blk0):
    B, R, D = x.shape
    H = MLA_HEADS
    nq = H * MLA_QK_PAD
    const2 = lambda b, i: (0, 0)
    mod_map = (lambda b, i: (b, 0, 0)) if shift.shape[0] == B else (lambda b, i: (0, 0, 0))
    rope_spec = pl.BlockSpec((tm, LANES), lambda b, i: (rope_blk0 + i, 0))
    return pl.pallas_call(
        _mla_qkv_kernel,
        grid=(B, R // tm),
        in_specs=[
            pl.BlockSpec((1, tm, D), lambda b, i: (b, i, 0)),
            pl.BlockSpec((1, D), const2),
            pl.BlockSpec((1, 1, D), mod_map),
            pl.BlockSpec((1, 1, D), mod_map),
            pl.BlockSpec(w["w1"].shape, const2),
            pl.BlockSpec(w["g1"].shape, const2),
            pl.BlockSpec(w["wuq"].shape, const2),
            pl.BlockSpec(w["gq"].shape, const2),
            pl.BlockSpec(w["wukv"].shape, const2),
            pl.BlockSpec(w["gk"].shape, const2),
            rope_spec, rope_spec, rope_spec,
        ],
        out_specs=[
            pl.BlockSpec((1, tm, nq), lambda b, i: (b, i, 0)),
            pl.BlockSpec((1, tm, nq), lambda b, i: (b, i, 0)),
            pl.BlockSpec((1, H, MLA_V, tm), lambda b, i: (b, 0, 0, i)),
        ],
        out_shape=[
            jax.ShapeDtypeStruct((B, R, nq), BF16),
            jax.ShapeDtypeStruct((B, R, nq), BF16),
            jax.ShapeDtypeStruct((B, H, MLA_V, R), BF16),
        ],
        compiler_params=_cparams(("parallel", "parallel")),
        name="mla_qkv",
    )(x, g, shift, scale, w["w1"], w["g1"], w["wuq"], w["gq"], w["wukv"], w["gk"], *rope)


def _gqa_qkv_kernel(x_ref, g_ref, sh_ref, sc_ref, w_ref, gq_ref, gk_ref,
                    cos_ref, sa_ref, sb_ref, q_ref, k_ref, vt_ref):
    h = _modulate(x_ref[0], g_ref[...], sh_ref[0], sc_ref[0]).astype(BF16)
    cos, sa, sb = cos_ref[...], sa_ref[...], sb_ref[...]
    quarter = GQA_HEAD_DIM // 4
    hd = GQA_HEAD_DIM
    group = GQA_KV_HEADS * hd
    nq = GQA_HEADS * hd
    gq = gq_ref[...] * (GQA_SCALE * LOG2E)
    gk = gk_ref[...]
    for c0 in range(0, nq, group):
        y = _dot(h, w_ref[:, c0:c0 + group])
        for j in range(0, group, hd):
            qh = _rope(_rms(y[:, j:j + hd], hd) * gq, cos, sa, sb, quarter)
            q_ref[0, :, c0 + j:c0 + j + hd] = qh.astype(BF16)
    y = _dot(h, w_ref[:, nq:nq + group])
    for j in range(0, group, hd):
        kh = _rope(_rms(y[:, j:j + hd], hd) * gk, cos, sa, sb, quarter)
        k_ref[0, :, j:j + hd] = kh.astype(BF16)
    y = _dot(h, w_ref[:, nq + group:nq + 2 * group])
    for j in range(GQA_KV_HEADS):
        vt_ref[0, j] = y[:, j * hd:(j + 1) * hd].T.astype(BF16)


def _gqa_qkv(x, g, shift, scale, w, rope, tm, rope_blk0):
    B, R, D = x.shape
    nq = GQA_HEADS * GQA_HEAD_DIM
    nk = GQA_KV_HEADS * GQA_HEAD_DIM
    const2 = lambda b, i: (0, 0)
    mod_map = (lambda b, i: (b, 0, 0)) if shift.shape[0] == B else (lambda b, i: (0, 0, 0))
    rope_spec = pl.BlockSpec((tm, LANES), lambda b, i: (rope_blk0 + i, 0))
    return pl.pallas_call(
        _gqa_qkv_kernel,
        grid=(B, R // tm),
        in_specs=[
            pl.BlockSpec((1, tm, D), lambda b, i: (b, i, 0)),
            pl.BlockSpec((1, D), const2),
            pl.BlockSpec((1, 1, D), mod_map),
            pl.BlockSpec((1, 1, D), mod_map),
            pl.BlockSpec(w["w"].shape, const2),
            pl.BlockSpec(w["gq"].shape, const2),
            pl.BlockSpec(w["gk"].shape, const2),
            rope_spec, rope_spec, rope_spec,
        ],
        out_specs=[
            pl.BlockSpec((1, tm, nq), lambda b, i: (b, i, 0)),
            pl.BlockSpec((1, tm, nk), lambda b, i: (b, i, 0)),
            pl.BlockSpec((1, GQA_KV_HEADS, GQA_HEAD_DIM, tm), lambda b, i: (b, 0, 0, i)),
        ],
        out_shape=[
            jax.ShapeDtypeStruct((B, R, nq), BF16),
            jax.ShapeDtypeStruct((B, R, nk), BF16),
            jax.ShapeDtypeStruct((B, GQA_KV_HEADS, GQA_HEAD_DIM, R), BF16),
        ],
        compiler_params=_cparams(("parallel", "parallel")),
        name="gqa_qkv",
    )(x, g, shift, scale, w["w"], w["gq"], w["gk"], *rope)


def _attn_kernel(*refs, chunks, n_src):
    q_ref = refs[0]
    kv_refs = refs[1:1 + 2 * n_src]
    o_ref = refs[1 + 2 * n_src]
    s_buf, p_buf, acc_ref = refs[2 + 2 * n_src:]
    q = q_ref[0]
    n = len(chunks)

    def qk(c):
        src, start, size = chunks[c]
        k_c = kv_refs[2 * src][0, start:start + size, :]
        s = lax.dot_general(k_c, q, (((1,), (1,)), ((), ())), preferred_element_type=F32)
        s_buf[c % 2, 0:size, :] = s
        return jnp.max(s, axis=0, keepdims=True)

    def pv(c, alpha):
        src, start, size = chunks[c]
        vt_c = kv_refs[2 * src + 1][0, 0, :, start:start + size]
        y = _dot(vt_c, p_buf[c % 2, 0:size, :])
        if c == 0:
            acc_ref[...] = y
        else:
            acc_ref[...] = alpha * acc_ref[...] + y

    m_next = qk(0)
    m = l = alpha_prev = None
    for c in range(n):
        size = chunks[c][2]
        m_c = m_next
        if c + 1 < n:
            m_next = qk(c + 1)
        s = s_buf[c % 2, 0:size, :]
        if c == 0:
            m_new, alpha = m_c, None
        else:
            m_new = jnp.maximum(m, m_c)
            alpha = jnp.exp2(m - m_new)
        p = jnp.exp2(s - m_new)
        p_buf[c % 2, 0:size, :] = p.astype(BF16)
        psum = jnp.sum(p, axis=0, keepdims=True)
        l = psum if c == 0 else alpha * l + psum
        m = m_new
        if c >= 1:
            pv(c - 1, alpha_prev)
        alpha_prev = alpha
    pv(n - 1, alpha_prev)
    o_ref[0] = (acc_ref[...] / l).T.astype(o_ref.dtype)


def _attention(q, kv_sources, n_heads, n_kv_heads, dq, dv, tq):
    B, Q, _ = q.shape
    G = n_heads // n_kv_heads
    chunks = []
    in_specs = [pl.BlockSpec((1, tq, dq), lambda b, hk, g, i: (b, i, hk * G + g))]
    args = [q]
    for src, (k, vt) in enumerate(kv_sources):
        T = k.shape[1]
        ck = min(ATTN_K_CHUNK, T)
        chunks += [(src, s, ck) for s in range(0, T, ck)]
        in_specs += [
            pl.BlockSpec((1, T, dq), lambda b, hk, g, i: (b, 0, hk)),
            pl.BlockSpec((1, 1, dv, T), lambda b, hk, g, i: (b, hk, 0, 0)),
        ]
        args += [k, vt]
    ck_max = max(size for _, _, size in chunks)
    return pl.pallas_call(
        functools.partial(_attn_kernel, chunks=tuple(chunks), n_src=len(kv_sources)),
        grid=(B, n_kv_heads, G, Q // tq),
        in_specs=in_specs,
        out_specs=pl.BlockSpec((1, tq, dv), lambda b, hk, g, i: (b, i, hk * G + g)),
        out_shape=jax.ShapeDtypeStruct((B, Q, n_heads * dv), BF16),
        scratch_shapes=[
            pltpu.VMEM((2, ck_max, tq), F32),
            pltpu.VMEM((2, ck_max, tq), BF16),
            pltpu.VMEM((dv, tq), F32),
        ],
        compiler_params=_cparams(("parallel", "parallel", "parallel", "parallel")),
        name="attention",
    )(*args)


def _oproj_kernel(o_ref, w_ref, x_ref, gate_ref, out_ref, *, col_tile):
    o = o_ref[0]
    gate = gate_ref[0]
    for c0 in range(0, out_ref.shape[-1], col_tile):
        y = _dot(o, w_ref[:, c0:c0 + col_tile])
        out_ref[0, :, c0:c0 + col_tile] = x_ref[0, :, c0:c0 + col_tile] + gate[:, c0:c0 + col_tile] * y


def _oproj(o, w_o, x, gate, tm):
    B, R, D = x.shape
    K = o.shape[-1]
    mod_map = (lambda b, i: (b, 0, 0)) if gate.shape[0] == B else (lambda b, i: (0, 0, 0))
    return pl.pallas_call(
        functools.partial(_oproj_kernel, col_tile=512),
        grid=(B, R // tm),
        in_specs=[
            pl.BlockSpec((1, tm, K), lambda b, i: (b, i, 0)),
            pl.BlockSpec((K, D), lambda b, i: (0, 0)),
            pl.BlockSpec((1, tm, D), lambda b, i: (b, i, 0)),
            pl.BlockSpec((1, 1, D), mod_map),
        ],
        out_specs=pl.BlockSpec((1, tm, D), lambda b, i: (b, i, 0)),
        out_shape=jax.ShapeDtypeStruct((B, R, D), F32),
        compiler_params=_cparams(("parallel", "parallel")),
        name="oproj",
    )(o, w_o, x, gate)


def _ffn_up_kernel(x_ref, g_ref, sh_ref, sc_ref, w_ref, u_ref, h_ref):
    @pl.when(pl.program_id(2) == 0)
    def _():
        h_ref[...] = _modulate(x_ref[0], g_ref[...], sh_ref[0], sc_ref[0]).astype(BF16)

    u_ref[0] = _dot(h_ref[...], w_ref[...]).astype(u_ref.dtype)


def _ffn_up(x, g, shift, scale, w_up, tm, tn):
    B, R, D = x.shape
    N = w_up.shape[1]
    mod_map = (lambda b, i, j: (b, 0, 0)) if shift.shape[0] == B else (lambda b, i, j: (0, 0, 0))
    return pl.pallas_call(
        _ffn_up_kernel,
        grid=(B, R // tm, N // tn),
        in_specs=[
            pl.BlockSpec((1, tm, D), lambda b, i, j: (b, i, 0)),
            pl.BlockSpec((1, D), lambda b, i, j: (0, 0)),
            pl.BlockSpec((1, 1, D), mod_map),
            pl.BlockSpec((1, 1, D), mod_map),
            pl.BlockSpec((D, tn), lambda b, i, j: (0, j)),
        ],
        out_specs=pl.BlockSpec((1, tm, tn), lambda b, i, j: (b, i, j)),
        out_shape=jax.ShapeDtypeStruct((B, R, N), BF16),
        scratch_shapes=[pltpu.VMEM((tm, D), BF16)],
        compiler_params=_cparams(("parallel", "parallel", "arbitrary")),
        name="ffn_up",
    )(x, g, shift, scale, w_up)


def _ffn_down_kernel(ug_ref, uv_ref, prev_ref, next_ref, cw_ref, cb_ref, w_ref, x_ref, gate_ref,
                     out_ref, win_ref):
    i = pl.program_id(1)
    kk = pl.program_id(2)
    tm = ug_ref.shape[1]
    first = i == 0
    last = i == pl.num_programs(1) - 1
    prev_row = prev_ref[0, HALO - 1:HALO, :].astype(F32)
    next_row = next_ref[0, 0:1, :].astype(F32)
    win_ref[HALO - 1:HALO, :] = jnp.where(first, 0.0, prev_row)
    win_ref[HALO + tm:HALO + tm + 1, :] = jnp.where(last, 0.0, next_row)
    win_ref[HALO:HALO + tm, :] = ug_ref[0].astype(F32)
    cw = cw_ref[...]
    conv = (win_ref[HALO - 1:HALO - 1 + tm, :] * cw[0:1, :]
            + win_ref[HALO:HALO + tm, :] * cw[1:2, :]
            + win_ref[HALO + 1:HALO + 1 + tm, :] * cw[2:3, :]
            + cb_ref[...])
    a = (conv * jax.nn.sigmoid(conv)) * uv_ref[0].astype(F32)
    y = _dot(a.astype(BF16), w_ref[...])

    @pl.when(kk == 0)
    def _():
        out_ref[0] = y

    @pl.when(kk > 0)
    def _():
        out_ref[0] += y

    @pl.when(kk == pl.num_programs(2) - 1)
    def _():
        out_ref[0] = x_ref[0] + gate_ref[0] * out_ref[0]


def _ffn_down(u, conv_w, conv_b, w_down, x, gate, tm, tk):
    B, R, D = x.shape
    F = w_down.shape[0]
    nk = F // tk
    rb = tm // HALO
    n_halo_blocks = R // HALO
    mod_map = (lambda b, i, k: (b, 0, 0)) if gate.shape[0] == B else (lambda b, i, k: (0, 0, 0))
    return pl.pallas_call(
        _ffn_down_kernel,
        grid=(B, R // tm, nk),
        in_specs=[
            pl.BlockSpec((1, tm, tk), lambda b, i, k: (b, i, k)),
            pl.BlockSpec((1, tm, tk), lambda b, i, k: (b, i, nk + k)),
            pl.BlockSpec((1, HALO, tk), lambda b, i, k: (b, jnp.maximum(i * rb - 1, 0), k)),
            pl.BlockSpec((1, HALO, tk),
                         lambda b, i, k: (b, jnp.minimum((i + 1) * rb, n_halo_blocks - 1), k)),
            pl.BlockSpec((CONV_W, tk), lambda b, i, k: (0, k)),
            pl.BlockSpec((1, tk), lambda b, i, k: (0, k)),
            pl.BlockSpec((tk, D), lambda b, i, k: (k, 0)),
            pl.BlockSpec((1, tm, D), lambda b, i, k: (b, i, 0)),
            pl.BlockSpec((1, 1, D), mod_map),
        ],
        out_specs=pl.BlockSpec((1, tm, D), lambda b, i, k: (b, i, 0)),
        out_shape=jax.ShapeDtypeStruct((B, R, D), F32),
        scratch_shapes=[pltpu.VMEM((tm + 2 * HALO, tk), F32)],
        compiler_params=_cparams(("parallel", "parallel", "arbitrary")),
        name="ffn_down",
    )(u, u, u, u, conv_w, conv_b.reshape(1, F), w_down, x, gate)


def _rope_tables(seq, ctx_len, rot_dim):
    t = jnp.arange(seq, dtype=jnp.int32)
    rows = (t // GRID_W).astype(F32)
    cols = (t % GRID_W).astype(F32)
    axis_dim = rot_dim // 2
    inv = jnp.power(ROPE_BASE, -jnp.arange(0, axis_dim, 2, dtype=F32) / axis_dim)
    ang_r = rows[:, None] * inv
    ang_c = cols[:, None] * inv
    ang = jnp.concatenate([ang_r, ang_r, ang_c, ang_c], axis=-1)
    cos, sin = jnp.cos(ang), jnp.sin(ang)
    lane = jnp.arange(rot_dim)
    first_quarter = (lane % axis_dim) < (axis_dim // 2)
    sin_a = jnp.where(first_quarter, -sin, 0.0)
    sin_b = jnp.where(first_quarter, 0.0, sin)

    def finish(tab, ctx_value):
        tab = jnp.concatenate([tab, jnp.full((ctx_len, rot_dim), ctx_value, F32)], axis=0)
        return jnp.pad(tab, ((0, 0), (0, LANES - rot_dim)))

    return finish(cos, 1.0), finish(sin_a, 0.0), finish(sin_b, 0.0)


def _mla_weights(j, w_dq, g_dq, w_uq, g_q_nope, g_q_pe, w_dkv, g_dkv, g_k_pe, w_ukv, g_k_nope):
    D = w_dq.shape[1]
    pad = LANES - MLA_ROPE
    w1 = jnp.concatenate([w_dq[j], w_dkv[j], jnp.zeros((D, pad), F32)], axis=1).astype(BF16)
    g1 = jnp.concatenate([g_dq[j], g_dkv[j], g_k_pe[j], jnp.zeros((pad,), F32)])[None, :]
    wuq = w_uq[j].reshape(MLA_Q_RANK, MLA_HEADS, MLA_NOPE + MLA_ROPE)
    wuq = jnp.pad(wuq, ((0, 0), (0, 0), (0, MLA_QK_PAD - MLA_NOPE - MLA_ROPE)))
    wuq = wuq.reshape(MLA_Q_RANK, MLA_HEADS * MLA_QK_PAD).astype(BF16)
    gq = jnp.concatenate([g_q_nope[j], g_q_pe[j], jnp.zeros((pad,), F32)])[None, :]
    return dict(w1=w1, g1=g1, wuq=wuq, gq=gq, wukv=w_ukv[j].astype(BF16), gk=g_k_nope[j][None, :])


def _gqa_weights(j, w_q, g_q, w_kv, g_k):
    w = jnp.concatenate([w_q[j], w_kv[j]], axis=1).astype(BF16)
    return dict(w=w, gq=g_q[j][None, :], gk=g_k[j][None, :])


def kernel(x, c, ctx, c_ctx, w_mod, b_mod, norm_mix, norm_ffn, mla_w_dq, mla_g_dq, mla_w_uq, mla_g_q_nope, mla_g_q_pe, mla_w_dkv, mla_g_dkv, mla_g_k_pe, mla_w_ukv, mla_g_k_nope, mla_w_o, gqa_w_q, gqa_g_q, gqa_w_kv, gqa_g_k, gqa_w_o, ffn_w_up, ffn_conv_w, ffn_conv_b, ffn_w_down):
    B, S, D = x.shape
    C = ctx.shape[1]
    depth = w_mod.shape[0]
    assert S % FFN_UP_ROW_TILE == 0 and S % ROW_TILE == 0 and C % HALO == 0 and C <= ROW_TILE

    cvec = jnp.concatenate([c, c_ctx[None, :], jnp.zeros((8 - B - 1, D), F32)], axis=0)
    mod = _mod_all(cvec, w_mod, b_mod).reshape(depth, 8, N_MOD, D)
    rope_mla = _rope_tables(S, C, MLA_ROPE)
    rope_gqa = _rope_tables(S, C, GQA_HEAD_DIM)

    for i in range(depth):
        last = i == depth - 1
        j = i // 2
        lat = [mod[i, :B, n][:, None, :] for n in range(N_MOD)]
        cmod = [mod[i, B:B + 1, n][:, None, :] for n in range(N_MOD)]
        g_mix = norm_mix[i][None, :]
        g_ffn = norm_ffn[i][None, :]

        if i % 2 == 0:
            w = _mla_weights(j, mla_w_dq, mla_g_dq, mla_w_uq, mla_g_q_nope, mla_g_q_pe,
                             mla_w_dkv, mla_g_dkv, mla_g_k_pe, mla_w_ukv, mla_g_k_nope)
            qkv, rope = _mla_qkv, rope_mla
            w_o = mla_w_o[j].astype(BF16)
            heads, kv_heads, dq, dv = MLA_HEADS, MLA_HEADS, MLA_QK_PAD, MLA_V
        else:
            w = _gqa_weights(j, gqa_w_q, gqa_g_q, gqa_w_kv, gqa_g_k)
            qkv, rope = _gqa_qkv, rope_gqa
            w_o = gqa_w_o[j].astype(BF16)
            heads, kv_heads, dq, dv = GQA_HEADS, GQA_KV_HEADS, GQA_HEAD_DIM, GQA_HEAD_DIM

        q, k, vt = qkv(x, g_mix, lat[0], lat[1], w, rope, ROW_TILE, 0)
        qc, kc, vtc = qkv(ctx, g_mix, cmod[0], cmod[1], w, rope, C, S // C)
        o = _attention(q, [(k, vt), (kc, vtc)], heads, kv_heads, dq, dv, ATTN_Q_TILE)
        x = _oproj(o, w_o, x, lat[2], ROW_TILE)
        if not last:
            oc = _attention(qc, [(kc, vtc)], heads, kv_heads, dq, dv, C)
            ctx = _oproj(oc, w_o, ctx, cmod[2], C)

        w_up = ffn_w_up[i].astype(BF16)
        w_down = ffn_w_down[i].astype(BF16)
        u = _ffn_up(x, g_ffn, lat[3], lat[4], w_up, FFN_UP_ROW_TILE, FFN_UP_COL_TILE)
        x = _ffn_down(u, ffn_conv_w[i], ffn_conv_b[i], w_down, x, lat[5],
                      FFN_DOWN_ROW_TILE, FFN_DOWN_K_TILE)
        if not last:
            uc = _ffn_up(ctx, g_ffn, cmod[3], cmod[4], w_up, C, FFN_UP_COL_TILE)
            ctx = _ffn_down(uc, ffn_conv_w[i], ffn_conv_b[i], w_down, ctx, cmod[5],
                            C, FFN_DOWN_K_TILE)
    return x
```

```python
import functools
import math

import jax
import jax.numpy as jnp
from jax import lax
from jax.experimental import pallas as pl
from jax.experimental.pallas import tpu as pltpu

F32 = jnp.float32
BF16 = jnp.bfloat16

GRID_W = 64
N_MOD = 6
MLA_HEADS = 16
MLA_Q_RANK = 512
MLA_KV_RANK = 512
MLA_NOPE = 128
MLA_ROPE = 64
MLA_V = 128
MLA_QK_PAD = 256
GQA_HEADS = 16
GQA_KV_HEADS = 4
GQA_HEAD_DIM = 128
CONV_W = 3
ROPE_BASE = 10000.0
EPS = 1e-6
LOG2E = 1.4426950408889634
MLA_SCALE = 1.0 / math.sqrt(MLA_NOPE + MLA_ROPE)
GQA_SCALE = 1.0 / math.sqrt(GQA_HEAD_DIM)

SAFE_LOG2_SPAN = 60.0
BOUND_MARGIN = 1.001

LANES = 128
BF16_SUBLANES = 16
VMEM_LIMIT_BYTES = 56 * 1024 * 1024

ROW_TILE = 512
MLA_ROW_TILE = 256
FFN_UP_ROW_TILE = 1024
FFN_UP_COL_TILE = 512
FFN_DOWN_ROW_TILE = 512
FFN_DOWN_K_TILE = 1408
ATTN_Q_TILE = 512
ATTN_K_CHUNK = 512
MOD_COL_TILE = 1024
HALO = BF16_SUBLANES


def _cparams(sem):
    return pltpu.CompilerParams(dimension_semantics=sem, vmem_limit_bytes=VMEM_LIMIT_BYTES)


def _dot(a, b):
    return jnp.dot(a, b, preferred_element_type=F32)


def _rms(y, n):
    return y * lax.rsqrt(jnp.sum(y * y, axis=-1, keepdims=True) * (1.0 / n) + EPS)


def _modulate(x, g, shift, scale):
    return (_rms(x, x.shape[-1]) * g) * (1.0 + scale) + shift


def _sumsq_bf16(v):
    vf = v.astype(F32)
    return jnp.sum(vf * vf, axis=-1, keepdims=True)


def _put_lane(vec, lane_idx, col_max):
    lane = lax.broadcasted_iota(jnp.int32, vec.shape, 1)
    return jnp.where(lane == lane_idx, jnp.max(col_max, axis=0, keepdims=True), vec)


def _rope(x, cos, sin_a, sin_b, quarter):
    return (x * cos + pltpu.roll(x, LANES - quarter, 1) * sin_a
            + pltpu.roll(x, quarter, 1) * sin_b)


def _mod_kernel(c_ref, w_ref, b_ref, o_ref):
    c = c_ref[...]
    sc = (c * jax.nn.sigmoid(c)).astype(BF16)
    o_ref[0] = _dot(sc, w_ref[0].astype(BF16)) + b_ref[0]


def _mod_all(cvec, w_mod, b_mod):
    L, D, N = w_mod.shape
    R = cvec.shape[0]
    return pl.pallas_call(
        _mod_kernel,
        grid=(L, N // MOD_COL_TILE),
        in_specs=[
            pl.BlockSpec((R, D), lambda l, j: (0, 0)),
            pl.BlockSpec((1, D, MOD_COL_TILE), lambda l, j: (l, 0, j)),
            pl.BlockSpec((1, 1, MOD_COL_TILE), lambda l, j: (l, 0, j)),
        ],
        out_specs=pl.BlockSpec((1, R, MOD_COL_TILE), lambda l, j: (l, 0, j)),
        out_shape=jax.ShapeDtypeStruct((L, R, N), F32),
        compiler_params=_cparams(("parallel", "parallel")),
        name="mod_all",
    )(cvec, w_mod, b_mod.reshape(L, 1, N))


def _mla_qkv_kernel(x_ref, g_ref, sh_ref, sc_ref, w1_ref, g1_ref, wuq_ref, gq_ref,
                    wukv_ref, gk_ref, cos_ref, sa_ref, sb_ref,
                    q_ref, k_ref, vt_ref, qn2_ref, kn2_ref):
    h = _modulate(x_ref[0], g_ref[...], sh_ref[0], sc_ref[0]).astype(BF16)
    y1 = _dot(h, w1_ref[...])
    g1 = g1_ref[...]
    cq = (_rms(y1[:, :MLA_Q_RANK], MLA_Q_RANK) * g1[:, :MLA_Q_RANK]).astype(BF16)
    lo, hi = MLA_Q_RANK, MLA_Q_RANK + MLA_KV_RANK
    ckv = (_rms(y1[:, lo:hi], MLA_KV_RANK) * g1[:, lo:hi]).astype(BF16)
    cos, sa, sb = cos_ref[...], sa_ref[...], sb_ref[...]
    quarter = MLA_ROPE // 4
    kpe = _rms(y1[:, hi:], MLA_ROPE) * g1[:, hi:]
    kpe = _rope(kpe, cos, sa, sb, quarter).astype(BF16)
    kpe_n2 = _sumsq_bf16(kpe)
    gq = gq_ref[...]
    gk = gk_ref[...]
    qscale = MLA_SCALE * LOG2E
    qn2 = jnp.zeros((1, LANES), F32)
    kn2 = jnp.zeros((1, LANES), F32)
    for hh in range(MLA_HEADS):
        c0 = hh * MLA_QK_PAD
        qh = _dot(cq, wuq_ref[:, c0:c0 + MLA_QK_PAD])
        qn = _rms(qh[:, :MLA_NOPE], MLA_NOPE) * gq[:, :MLA_NOPE]
        qp = _rms(qh[:, MLA_NOPE:], MLA_ROPE) * gq[:, MLA_NOPE:]
        qp = _rope(qp, cos, sa, sb, quarter)
        qn = (qn * qscale).astype(BF16)
        qp = (qp * qscale).astype(BF16)
        q_ref[0, :, c0:c0 + MLA_NOPE] = qn
        q_ref[0, :, c0 + MLA_NOPE:c0 + MLA_QK_PAD] = qp
        qn2 = _put_lane(qn2, hh, _sumsq_bf16(qn) + _sumsq_bf16(qp))
        kvh = _dot(ckv, wukv_ref[:, c0:c0 + MLA_NOPE + MLA_V])
        kn = (_rms(kvh[:, :MLA_NOPE], MLA_NOPE) * gk).astype(BF16)
        k_ref[0, :, c0:c0 + MLA_NOPE] = kn
        k_ref[0, :, c0 + MLA_NOPE:c0 + MLA_QK_PAD] = kpe
        kn2 = _put_lane(kn2, hh, _sumsq_bf16(kn) + kpe_n2)
        vt_ref[0, hh] = kvh[:, MLA_NOPE:].T.astype(BF16)
    qn2_ref[0, 0] = qn2
    kn2_ref[0, 0] = kn2


def _mla_qkv(x, g, shift, scale, w, rope, tm, rope_blk0):
    B, R, D = x.shape
    H = MLA_HEADS
    nq = H * MLA_QK_PAD
    const2 = lambda b, i: (0, 0)
    mod_map = (lambda b, i: (b, 0, 0)) if shift.shape[0] == B else (lambda b, i: (0, 0, 0))
    rope_spec = pl.BlockSpec((tm, LANES), lambda b, i: (rope_blk0 + i, 0))
    return pl.pallas_call(
        _mla_qkv_kernel,
        grid=(B, R // tm),
        in_specs=[
            pl.BlockSpec((1, tm, D), lambda b, i: (b, i, 0)),
            pl.BlockSpec((1, D), const2),
            pl.BlockSpec((1, 1, D), mod_map),
            pl.BlockSpec((1, 1, D), mod_map),
            pl.BlockSpec(w["w1"].shape, const2),
            pl.BlockSpec(w["g1"].shape, const2),
            pl.BlockSpec(w["wuq"].shape, const2),
            pl.BlockSpec(w["gq"].shape, const2),
            pl.BlockSpec(w["wukv"].shape, const2),
            pl.BlockSpec(w["gk"].shape, const2),
            rope_spec, rope_spec, rope_spec,
        ],
        out_specs=[
            pl.BlockSpec((1, tm, nq), lambda b, i: (b, i, 0)),
            pl.BlockSpec((1, tm, nq), lambda b, i: (b, i, 0)),
            pl.BlockSpec((1, H, MLA_V, tm), lambda b, i: (b, 0, 0, i)),
            pl.BlockSpec((1, 1, 1, LANES), lambda b, i: (b, i, 0, 0)),
            pl.BlockSpec((1, 1, 1, LANES), lambda b, i: (b, i, 0, 0)),
        ],
        out_shape=[
            jax.ShapeDtypeStruct((B, R, nq), BF16),
            jax.ShapeDtypeStruct((B, R, nq), BF16),
            jax.ShapeDtypeStruct((B, H, MLA_V, R), BF16),
            jax.ShapeDtypeStruct((B, R // tm, 1, LANES), F32),
            jax.ShapeDtypeStruct((B, R // tm, 1, LANES), F32),
        ],
        compiler_params=_cparams(("parallel", "parallel")),
        name="mla_qkv",
    )(x, g, shift, scale, w["w1"], w["g1"], w["wuq"], w["gq"], w["wukv"], w["gk"], *rope)


def _gqa_qkv_kernel(x_ref, g_ref, sh_ref, sc_ref, w_ref, gq_ref, gk_ref,
                    cos_ref, sa_ref, sb_ref, q_ref, k_ref, vt_ref, qn2_ref, kn2_ref):
    h = _modulate(x_ref[0], g_ref[...], sh_ref[0], sc_ref[0]).astype(BF16)
    cos, sa, sb = cos_ref[...], sa_ref[...], sb_ref[...]
    quarter = GQA_HEAD_DIM // 4
    hd = GQA_HEAD_DIM
    group = GQA_KV_HEADS * hd
    nq = GQA_HEADS * hd
    gq = gq_ref[...] * (GQA_SCALE * LOG2E)
    gk = gk_ref[...]
    qn2 = jnp.zeros((1, LANES), F32)
    kn2 = jnp.zeros((1, LANES), F32)
    for c0 in range(0, nq, group):
        y = _dot(h, w_ref[:, c0:c0 + group])
        for j in range(0, group, hd):
            qh = _rope(_rms(y[:, j:j + hd], hd) * gq, cos, sa, sb, quarter).astype(BF16)
            q_ref[0, :, c0 + j:c0 + j + hd] = qh
            qn2 = _put_lane(qn2, (c0 + j) // hd, _sumsq_bf16(qh))
    y = _dot(h, w_ref[:, nq:nq + group])
    for j in range(0, group, hd):
        kh = _rope(_rms(y[:, j:j + hd], hd) * gk, cos, sa, sb, quarter).astype(BF16)
        k_ref[0, :, j:j + hd] = kh
        kn2 = _put_lane(kn2, j // hd, _sumsq_bf16(kh))
    qn2_ref[0, 0] = qn2
    kn2_ref[0, 0] = kn2
    y = _dot(h, w_ref[:, nq + group:nq + 2 * group])
    for j in range(GQA_KV_HEADS):
        vt_ref[0, j] = y[:, j * hd:(j + 1) * hd].T.astype(BF16)


def _gqa_qkv(x, g, shift, scale, w, rope, tm, rope_blk0):
    B, R, D = x.shape
    nq = GQA_HEADS * GQA_HEAD_DIM
    nk = GQA_KV_HEADS * GQA_HEAD_DIM
    const2 = lambda b, i: (0, 0)
    mod_map = (lambda b, i: (b, 0, 0)) if shift.shape[0] == B else (lambda b, i: (0, 0, 0))
    rope_spec = pl.BlockSpec((tm, LANES), lambda b, i: (rope_blk0 + i, 0))
    return pl.pallas_call(
        _gqa_qkv_kernel,
        grid=(B, R // tm),
        in_specs=[
            pl.BlockSpec((1, tm, D), lambda b, i: (b, i, 0)),
            pl.BlockSpec((1, D), const2),
            pl.BlockSpec((1, 1, D), mod_map),
            pl.BlockSpec((1, 1, D), mod_map),
            pl.BlockSpec(w["w"].shape, const2),
            pl.BlockSpec(w["gq"].shape, const2),
            pl.BlockSpec(w["gk"].shape, const2),
            rope_spec, rope_spec, rope_spec,
        ],
        out_specs=[
            pl.BlockSpec((1, tm, nq), lambda b, i: (b, i, 0)),
            pl.BlockSpec((1, tm, nk), lambda b, i: (b, i, 0)),
            pl.BlockSpec((1, GQA_KV_HEADS, GQA_HEAD_DIM, tm), lambda b, i: (b, 0, 0, i)),
            pl.BlockSpec((1, 1, 1, LANES), lambda b, i: (b, i, 0, 0)),
            pl.BlockSpec((1, 1, 1, LANES), lambda b, i: (b, i, 0, 0)),
        ],
        out_shape=[
            jax.ShapeDtypeStruct((B, R, nq), BF16),
            jax.ShapeDtypeStruct((B, R, nk), BF16),
            jax.ShapeDtypeStruct((B, GQA_KV_HEADS, GQA_HEAD_DIM, R), BF16),
            jax.ShapeDtypeStruct((B, R // tm, 1, LANES), F32),
            jax.ShapeDtypeStruct((B, R // tm, 1, LANES), F32),
        ],
        compiler_params=_cparams(("parallel", "parallel")),
        name="gqa_qkv",
    )(x, g, shift, scale, w["w"], w["gq"], w["gk"], *rope)


def _attn_kernel(bound_ref, *refs, chunks, n_src, group):
    q_ref = refs[0]
    kv_refs = refs[1:1 + 2 * n_src]
    o_ref = refs[1 + 2 * n_src]
    s_buf, p_buf, acc_ref = refs[2 + 2 * n_src:]
    n = len(chunks)
    bound = bound_ref[pl.program_id(0), pl.program_id(1) * group + pl.program_id(2)]

    def qk(c):
        src, start, size = chunks[c]
        k_c = kv_refs[2 * src][0, start:start + size, :]
        return lax.dot_general(k_c, q_ref[0], (((1,), (1,)), ((), ())),
                               preferred_element_type=F32)

    def vt(c):
        src, start, size = chunks[c]
        return kv_refs[2 * src + 1][0, 0, :, start:start + size]

    def bounded():
        acc = l = p_prev = None
        s_next = qk(0)
        for c in range(n):
            s = s_next
            if c + 1 < n:
                s_next = qk(c + 1)
            p = jnp.exp2(s - bound)
            psum = jnp.sum(p, axis=0, keepdims=True)
            l = psum if c == 0 else l + psum
            if c >= 1:
                y = _dot(vt(c - 1), p_prev)
                acc = y if c == 1 else acc + y
            p_prev = p.astype(BF16)
        y = _dot(vt(n - 1), p_prev)
        acc = y if n == 1 else acc + y
        o_ref[0] = (acc / l).T.astype(o_ref.dtype)

    def online():
        def qk_store(c):
            s = qk(c)
            s_buf[c % 2, 0:chunks[c][2], :] = s
            return jnp.max(s, axis=0, keepdims=True)

        def pv(c, alpha):
            y = _dot(vt(c), p_buf[c % 2, 0:chunks[c][2], :])
            if c == 0:
                acc_ref[...] = y
            else:
                acc_ref[...] = alpha * acc_ref[...] + y

        m_next = qk_store(0)
        m = l = alpha_prev = None
        for c in range(n):
            size = chunks[c][2]
            m_c = m_next
            if c + 1 < n:
                m_next = qk_store(c + 1)
            s = s_buf[c % 2, 0:size, :]
            if c == 0:
                m_new, alpha = m_c, None
            else:
                m_new = jnp.maximum(m, m_c)
                alpha = jnp.exp2(m - m_new)
            p = jnp.exp2(s - m_new)
            p_buf[c % 2, 0:size, :] = p.astype(BF16)
            psum = jnp.sum(p, axis=0, keepdims=True)
            l = psum if c == 0 else alpha * l + psum
            m = m_new
            if c >= 1:
                pv(c - 1, alpha_prev)
            alpha_prev = alpha
        pv(n - 1, alpha_prev)
        o_ref[0] = (acc_ref[...] / l).T.astype(o_ref.dtype)

    lax.cond(bound < SAFE_LOG2_SPAN, bounded, online)


def _attention(q, kv_sources, bound, n_heads, n_kv_heads, dq, dv, tq):
    B, Q, _ = q.shape
    G = n_heads // n_kv_heads
    chunks = []
    in_specs = [pl.BlockSpec((1, tq, dq), lambda b, hk, g, i, bnd: (b, i, hk * G + g))]
    args = [q]
    for src, (k, vt) in enumerate(kv_sources):
        T = k.shape[1]
        ck = min(ATTN_K_CHUNK, T)
        chunks += [(src, s, ck) for s in range(0, T, ck)]
        in_specs += [
            pl.BlockSpec((1, T, dq), lambda b, hk, g, i, bnd: (b, 0, hk)),
            pl.BlockSpec((1, 1, dv, T), lambda b, hk, g, i, bnd: (b, hk, 0, 0)),
        ]
        args += [k, vt]
    ck_max = max(size for _, _, size in chunks)
    return pl.pallas_call(
        functools.partial(_attn_kernel, chunks=tuple(chunks), n_src=len(kv_sources), group=G),
        grid_spec=pltpu.PrefetchScalarGridSpec(
            num_scalar_prefetch=1,
            grid=(B, n_kv_heads, G, Q // tq),
            in_specs=in_specs,
            out_specs=pl.BlockSpec((1, tq, dv), lambda b, hk, g, i, bnd: (b, i, hk * G + g)),
            scratch_shapes=[
                pltpu.VMEM((2, ck_max, tq), F32),
                pltpu.VMEM((2, ck_max, tq), BF16),
                pltpu.VMEM((dv, tq), F32),
            ],
        ),
        out_shape=jax.ShapeDtypeStruct((B, Q, n_heads * dv), BF16),
        compiler_params=_cparams(("parallel", "parallel", "parallel", "parallel")),
        name="attention",
    )(bound, *args)


def _score_bound(qn2, kn2_list, n_heads, n_kv_heads):
    qmax = jnp.max(qn2[:, :, 0, :n_heads], axis=1)
    kmax = functools.reduce(jnp.maximum,
                            [jnp.max(k[:, :, 0, :n_kv_heads], axis=1) for k in kn2_list])
    kmax = jnp.repeat(kmax, n_heads // n_kv_heads, axis=1)
    return jnp.sqrt(qmax * kmax) * BOUND_MARGIN


def _oproj_kernel(o_ref, w_ref, x_ref, gate_ref, out_ref, *, col_tile):
    o = o_ref[0]
    gate = gate_ref[0]
    for c0 in range(0, out_ref.shape[-1], col_tile):
        y = _dot(o, w_ref[:, c0:c0 + col_tile])
        out_ref[0, :, c0:c0 + col_tile] = x_ref[0, :, c0:c0 + col_tile] + gate[:, c0:c0 + col_tile] * y


def _oproj(o, w_o, x, gate, tm):
    B, R, D = x.shape
    K = o.shape[-1]
    mod_map = (lambda b, i: (b, 0, 0)) if gate.shape[0] == B else (lambda b, i: (0, 0, 0))
    return pl.pallas_call(
        functools.partial(_oproj_kernel, col_tile=512),
        grid=(B, R // tm),
        in_specs=[
            pl.BlockSpec((1, tm, K), lambda b, i: (b, i, 0)),
            pl.BlockSpec((K, D), lambda b, i: (0, 0)),
            pl.BlockSpec((1, tm, D), lambda b, i: (b, i, 0)),
            pl.BlockSpec((1, 1, D), mod_map),
        ],
        out_specs=pl.BlockSpec((1, tm, D), lambda b, i: (b, i, 0)),
        out_shape=jax.ShapeDtypeStruct((B, R, D), F32),
        compiler_params=_cparams(("parallel", "parallel")),
        name="oproj",
    )(o, w_o, x, gate)


def _ffn_up_kernel(x_ref, g_ref, sh_ref, sc_ref, w_ref, u_ref, h_ref):
    @pl.when(pl.program_id(2) == 0)
    def _():
        h_ref[...] = _modulate(x_ref[0], g_ref[...], sh_ref[0], sc_ref[0]).astype(BF16)

    u_ref[0] = _dot(h_ref[...], w_ref[...]).astype(u_ref.dtype)


def _ffn_up(x, g, shift, scale, w_up, tm, tn):
    B, R, D = x.shape
    N = w_up.shape[1]
    mod_map = (lambda b, i, j: (b, 0, 0)) if shift.shape[0] == B else (lambda b, i, j: (0, 0, 0))
    return pl.pallas_call(
        _ffn_up_kernel,
        grid=(B, R // tm, N // tn),
        in_specs=[
            pl.BlockSpec((1, tm, D), lambda b, i, j: (b, i, 0)),
            pl.BlockSpec((1, D), lambda b, i, j: (0, 0)),
            pl.BlockSpec((1, 1, D), mod_map),
            pl.BlockSpec((1, 1, D), mod_map),
            pl.BlockSpec((D, tn), lambda b, i, j: (0, j)),
        ],
        out_specs=pl.BlockSpec((1, tm, tn), lambda b, i, j: (b, i, j)),
        out_shape=jax.ShapeDtypeStruct((B, R, N), BF16),
        scratch_shapes=[pltpu.VMEM((tm, D), BF16)],
        compiler_params=_cparams(("parallel", "parallel", "arbitrary")),
        name="ffn_up",
    )(x, g, shift, scale, w_up)


def _ffn_down_kernel(ug_ref, uv_ref, prev_ref, next_ref, cw_ref, cb_ref, w_ref, x_ref, gate_ref,
                     out_ref, win_ref):
    i = pl.program_id(1)
    kk = pl.program_id(2)
    tm = ug_ref.shape[1]
    first = i == 0
    last = i == pl.num_programs(1) - 1
    prev_row = prev_ref[0, HALO - 1:HALO, :].astype(F32)
    next_row = next_ref[0, 0:1, :].astype(F32)
    win_ref[HALO - 1:HALO, :] = jnp.where(first, 0.0, prev_row)
    win_ref[HALO + tm:HALO + tm + 1, :] = jnp.where(last, 0.0, next_row)
    win_ref[HALO:HALO + tm, :] = ug_ref[0].astype(F32)
    cw = cw_ref[...]
    conv = (win_ref[HALO - 1:HALO - 1 + tm, :] * cw[0:1, :]
            + win_ref[HALO:HALO + tm, :] * cw[1:2, :]
            + win_ref[HALO + 1:HALO + 1 + tm, :] * cw[2:3, :]
            + cb_ref[...])
    a = (conv * jax.nn.sigmoid(conv)) * uv_ref[0].astype(F32)
    y = _dot(a.astype(BF16), w_ref[...])

    @pl.when(kk == 0)
    def _():
        out_ref[0] = y

    @pl.when(kk > 0)
    def _():
        out_ref[0] += y

    @pl.when(kk == pl.num_programs(2) - 1)
    def _():
        out_ref[0] = x_ref[0] + gate_ref[0] * out_ref[0]


def _ffn_down(u, conv_w, conv_b, w_down, x, gate, tm, tk):
    B, R, D = x.shape
    F = w_down.shape[0]
    nk = F // tk
    rb = tm // HALO
    n_halo_blocks = R // HALO
    mod_map = (lambda b, i, k: (b, 0, 0)) if gate.shape[0] == B else (lambda b, i, k: (0, 0, 0))
    return pl.pallas_call(
        _ffn_down_kernel,
        grid=(B, R // tm, nk),
        in_specs=[
            pl.BlockSpec((1, tm, tk), lambda b, i, k: (b, i, k)),
            pl.BlockSpec((1, tm, tk), lambda b, i, k: (b, i, nk + k)),
            pl.BlockSpec((1, HALO, tk), lambda b, i, k: (b, jnp.maximum(i * rb - 1, 0), k)),
            pl.BlockSpec((1, HALO, tk),
                         lambda b, i, k: (b, jnp.minimum((i + 1) * rb, n_halo_blocks - 1), k)),
            pl.BlockSpec((CONV_W, tk), lambda b, i, k: (0, k)),
            pl.BlockSpec((1, tk), lambda b, i, k: (0, k)),
            pl.BlockSpec((tk, D), lambda b, i, k: (k, 0)),
            pl.BlockSpec((1, tm, D), lambda b, i, k: (b, i, 0)),
            pl.BlockSpec((1, 1, D), mod_map),
        ],
        out_specs=pl.BlockSpec((1, tm, D), lambda b, i, k: (b, i, 0)),
        out_shape=jax.ShapeDtypeStruct((B, R, D), F32),
        scratch_shapes=[pltpu.VMEM((tm + 2 * HALO, tk), F32)],
        compiler_params=_cparams(("parallel", "parallel", "arbitrary")),
        name="ffn_down",
    )(u, u, u, u, conv_w, conv_b.reshape(1, F), w_down, x, gate)


def _rope_tables(seq, ctx_len, rot_dim):
    t = jnp.arange(seq, dtype=jnp.int32)
    rows = (t // GRID_W).astype(F32)
    cols = (t % GRID_W).astype(F32)
    axis_dim = rot_dim // 2
    inv = jnp.power(ROPE_BASE, -jnp.arange(0, axis_dim, 2, dtype=F32) / axis_dim)
    ang_r = rows[:, None] * inv
    ang_c = cols[:, None] * inv
    ang = jnp.concatenate([ang_r, ang_r, ang_c, ang_c], axis=-1)
    cos, sin = jnp.cos(ang), jnp.sin(ang)
    lane = jnp.arange(rot_dim)
    first_quarter = (lane % axis_dim) < (axis_dim // 2)
    sin_a = jnp.where(first_quarter, -sin, 0.0)
    sin_b = jnp.where(first_quarter, 0.0, sin)

    def finish(tab, ctx_value):
        tab = jnp.concatenate([tab, jnp.full((ctx_len, rot_dim), ctx_value, F32)], axis=0)
        return jnp.pad(tab, ((0, 0), (0, LANES - rot_dim)))

    return finish(cos, 1.0), finish(sin_a, 0.0), finish(sin_b, 0.0)


def _mla_weights(j, w_dq, g_dq, w_uq, g_q_nope, g_q_pe, w_dkv, g_dkv, g_k_pe, w_ukv, g_k_nope):
    D = w_dq.shape[1]
    pad = LANES - MLA_ROPE
    w1 = jnp.concatenate([w_dq[j], w_dkv[j], jnp.zeros((D, pad), F32)], axis=1).astype(BF16)
    g1 = jnp.concatenate([g_dq[j], g_dkv[j], g_k_pe[j], jnp.zeros((pad,), F32)])[None, :]
    wuq = w_uq[j].reshape(MLA_Q_RANK, MLA_HEADS, MLA_NOPE + MLA_ROPE)
    wuq = jnp.pad(wuq, ((0, 0), (0, 0), (0, MLA_QK_PAD - MLA_NOPE - MLA_ROPE)))
    wuq = wuq.reshape(MLA_Q_RANK, MLA_HEADS * MLA_QK_PAD).astype(BF16)
    gq = jnp.concatenate([g_q_nope[j], g_q_pe[j], jnp.zeros((pad,), F32)])[None, :]
    return dict(w1=w1, g1=g1, wuq=wuq, gq=gq, wukv=w_ukv[j].astype(BF16), gk=g_k_nope[j][None, :])


def _gqa_weights(j, w_q, g_q, w_kv, g_k):
    w = jnp.concatenate([w_q[j], w_kv[j]], axis=1).astype(BF16)
    return dict(w=w, gq=g_q[j][None, :], gk=g_k[j][None, :])


def kernel(x, c, ctx, c_ctx, w_mod, b_mod, norm_mix, norm_ffn, mla_w_dq, mla_g_dq, mla_w_uq, mla_g_q_nope, mla_g_q_pe, mla_w_dkv, mla_g_dkv, mla_g_k_pe, mla_w_ukv, mla_g_k_nope, mla_w_o, gqa_w_q, gqa_g_q, gqa_w_kv, gqa_g_k, gqa_w_o, ffn_w_up, ffn_conv_w, ffn_conv_b, ffn_w_down):
    B, S, D = x.shape
    C = ctx.shape[1]
    depth = w_mod.shape[0]
    assert S % FFN_UP_ROW_TILE == 0 and S % ROW_TILE == 0 and C % HALO == 0 and C <= ROW_TILE

    cvec = jnp.concatenate([c, c_ctx[None, :], jnp.zeros((8 - B - 1, D), F32)], axis=0)
    mod = _mod_all(cvec, w_mod, b_mod).reshape(depth, 8, N_MOD, D)
    rope_mla = _rope_tables(S, C, MLA_ROPE)
    rope_gqa = _rope_tables(S, C, GQA_HEAD_DIM)

    for i in range(depth):
        last = i == depth - 1
        j = i // 2
        lat = [mod[i, :B, n][:, None, :] for n in range(N_MOD)]
        cmod = [mod[i, B:B + 1, n][:, None, :] for n in range(N_MOD)]
        g_mix = norm_mix[i][None, :]
        g_ffn = norm_ffn[i][None, :]

        if i % 2 == 0:
            w = _mla_weights(j, mla_w_dq, mla_g_dq, mla_w_uq, mla_g_q_nope, mla_g_q_pe,
                             mla_w_dkv, mla_g_dkv, mla_g_k_pe, mla_w_ukv, mla_g_k_nope)
            qkv, rope, qkv_tile = _mla_qkv, rope_mla, MLA_ROW_TILE
            w_o = mla_w_o[j].astype(BF16)
            heads, kv_heads, dq, dv = MLA_HEADS, MLA_HEADS, MLA_QK_PAD, MLA_V
        else:
            w = _gqa_weights(j, gqa_w_q, gqa_g_q, gqa_w_kv, gqa_g_k)
            qkv, rope, qkv_tile = _gqa_qkv, rope_gqa, ROW_TILE
            w_o = gqa_w_o[j].astype(BF16)
            heads, kv_heads, dq, dv = GQA_HEADS, GQA_KV_HEADS, GQA_HEAD_DIM, GQA_HEAD_DIM

        q, k, vt, qn2, kn2 = qkv(x, g_mix, lat[0], lat[1], w, rope, qkv_tile, 0)
        qc, kc, vtc, qn2c, kn2c = qkv(ctx, g_mix, cmod[0], cmod[1], w, rope, C, S // C)
        bound = _score_bound(qn2, [kn2, kn2c], heads, kv_heads)
        o = _attention(q, [(k, vt), (kc, vtc)], bound, heads, kv_heads, dq, dv, ATTN_Q_TILE)
        x = _oproj(o, w_o, x, lat[2], ROW_TILE)
        if not last:
            bound_c = _score_bound(qn2c, [kn2c], heads, kv_heads)
            oc = _attention(qc, [(kc, vtc)], bound_c, heads, kv_heads, dq, dv, C)
            ctx = _oproj(oc, w_o, ctx, cmod[2], C)

        w_up = ffn_w_up[i].astype(BF16)
        w_down = ffn_w_down[i].astype(BF16)
        u = _ffn_up(x, g_ffn, lat[3], lat[4], w_up, FFN_UP_ROW_TILE, FFN_UP_COL_TILE)
        x = _ffn_down(u, ffn_conv_w[i], ffn_conv_b[i], w_down, x, lat[5],
                      FFN_DOWN_ROW_TILE, FFN_DOWN_K_TILE)
        if not last:
            uc = _ffn_up(ctx, g_ffn, cmod[3], cmod[4], w_up, C, FFN_UP_COL_TILE)
            ctx = _ffn_down(uc, ffn_conv_w[i], ffn_conv_b[i], w_down, ctx, cmod[5],
                            C, FFN_DOWN_K_TILE)
    return x
```

```python
import functools
import math

import jax
import jax.numpy as jnp
from jax import lax
from jax.experimental import pallas as pl
from jax.experimental.pallas import tpu as pltpu

F32 = jnp.float32
BF16 = jnp.bfloat16

GRID_W = 64
N_MOD = 6
MLA_HEADS = 16
MLA_Q_RANK = 512
MLA_KV_RANK = 512
MLA_NOPE = 128
MLA_ROPE = 64
MLA_V = 128
MLA_QK_PAD = 256
GQA_HEADS = 16
GQA_KV_HEADS = 4
GQA_HEAD_DIM = 128
CONV_W = 3
ROPE_BASE = 10000.0
EPS = 1e-6
LOG2E = 1.4426950408889634
MLA_SCALE = 1.0 / math.sqrt(MLA_NOPE + MLA_ROPE)
GQA_SCALE = 1.0 / math.sqrt(GQA_HEAD_DIM)

SAFE_LOG2_SPAN = 60.0
BOUND_MARGIN = 1.02

LANES = 128
BF16_SUBLANES = 16
VMEM_LIMIT_BYTES = 56 * 1024 * 1024

ROW_TILE = 512
FFN_UP_ROW_TILE = 1024
FFN_UP_COL_TILE = 512
FFN_DOWN_ROW_TILE = 512
FFN_DOWN_K_TILE = 1408
ATTN_Q_TILE = 512
ATTN_K_CHUNK = 512
MOD_COL_TILE = 1024
HALO = BF16_SUBLANES


def _cparams(sem):
    return pltpu.CompilerParams(dimension_semantics=sem, vmem_limit_bytes=VMEM_LIMIT_BYTES)


def _dot(a, b):
    return jnp.dot(a, b, preferred_element_type=F32)


def _rms(y, n):
    return y * lax.rsqrt(jnp.sum(y * y, axis=-1, keepdims=True) * (1.0 / n) + EPS)


def _modulate(x, g, shift, scale):
    return (_rms(x, x.shape[-1]) * g) * (1.0 + scale) + shift


def _rope(x, cos, sin_a, sin_b, quarter):
    return (x * cos + pltpu.roll(x, LANES - quarter, 1) * sin_a
            + pltpu.roll(x, quarter, 1) * sin_b)


def _mod_kernel(c_ref, w_ref, b_ref, o_ref):
    c = c_ref[...]
    sc = (c * jax.nn.sigmoid(c)).astype(BF16)
    o_ref[0] = _dot(sc, w_ref[0].astype(BF16)) + b_ref[0]


def _mod_all(cvec, w_mod, b_mod):
    L, D, N = w_mod.shape
    R = cvec.shape[0]
    return pl.pallas_call(
        _mod_kernel,
        grid=(L, N // MOD_COL_TILE),
        in_specs=[
            pl.BlockSpec((R, D), lambda l, j: (0, 0)),
            pl.BlockSpec((1, D, MOD_COL_TILE), lambda l, j: (l, 0, j)),
            pl.BlockSpec((1, 1, MOD_COL_TILE), lambda l, j: (l, 0, j)),
        ],
        out_specs=pl.BlockSpec((1, R, MOD_COL_TILE), lambda l, j: (l, 0, j)),
        out_shape=jax.ShapeDtypeStruct((L, R, N), F32),
        compiler_params=_cparams(("parallel", "parallel")),
        name="mod_all",
    )(cvec, w_mod, b_mod.reshape(L, 1, N))


def _mla_qkv_kernel(x_ref, g_ref, sh_ref, sc_ref, w1_ref, g1_ref, wuq_ref, gq_ref,
                    wukv_ref, gk_ref, cos_ref, sa_ref, sb_ref,
                    q_ref, k_ref, vt_ref):
    h = _modulate(x_ref[0], g_ref[...], sh_ref[0], sc_ref[0]).astype(BF16)
    y1 = _dot(h, w1_ref[...])
    g1 = g1_ref[...]
    cq = (_rms(y1[:, :MLA_Q_RANK], MLA_Q_RANK) * g1[:, :MLA_Q_RANK]).astype(BF16)
    lo, hi = MLA_Q_RANK, MLA_Q_RANK + MLA_KV_RANK
    ckv = (_rms(y1[:, lo:hi], MLA_KV_RANK) * g1[:, lo:hi]).astype(BF16)
    cos, sa, sb = cos_ref[...], sa_ref[...], sb_ref[...]
    quarter = MLA_ROPE // 4
    kpe = _rms(y1[:, hi:], MLA_ROPE) * g1[:, hi:]
    kpe = _rope(kpe, cos, sa, sb, quarter).astype(BF16)
    gq = gq_ref[...]
    gk = gk_ref[...]
    qscale = MLA_SCALE * LOG2E

    def head_dots(hh):
        c0 = hh * MLA_QK_PAD
        return (_dot(cq, wuq_ref[:, c0:c0 + MLA_QK_PAD]),
                _dot(ckv, wukv_ref[:, c0:c0 + MLA_NOPE + MLA_V]))

    nxt = head_dots(0)
    for hh in range(MLA_HEADS):
        c0 = hh * MLA_QK_PAD
        qh, kvh = nxt
        if hh + 1 < MLA_HEADS:
            nxt = head_dots(hh + 1)
        qn = _rms(qh[:, :MLA_NOPE], MLA_NOPE) * gq[:, :MLA_NOPE]
        qp = _rms(qh[:, MLA_NOPE:], MLA_ROPE) * gq[:, MLA_NOPE:]
        qp = _rope(qp, cos, sa, sb, quarter)
        q_ref[0, :, c0:c0 + MLA_NOPE] = (qn * qscale).astype(BF16)
        q_ref[0, :, c0 + MLA_NOPE:c0 + MLA_QK_PAD] = (qp * qscale).astype(BF16)
        kn = _rms(kvh[:, :MLA_NOPE], MLA_NOPE) * gk
        k_ref[0, :, c0:c0 + MLA_NOPE] = kn.astype(BF16)
        k_ref[0, :, c0 + MLA_NOPE:c0 + MLA_QK_PAD] = kpe
        vt_ref[0, hh] = kvh[:, MLA_NOPE:].T.astype(BF16)


def _mla_qkv(x, g, shift, scale, w, rope, tm, rope_blk0):
    B, R, D = x.shape
    H = MLA_HEADS
    nq = H * MLA_QK_PAD
    const2 = lambda b, i: (0, 0)
    mod_map = (lambda b, i: (b, 0, 0)) if shift.shape[0] == B else (lambda b, i: (0, 0, 0))
    rope_spec = pl.BlockSpec((tm, LANES), lambda b, i: (rope_blk0 + i, 0))
    return pl.pallas_call(
        _mla_qkv_kernel,
        grid=(B, R // tm),
        in_specs=[
            pl.BlockSpec((1, tm, D), lambda b, i: (b, i, 0)),
            pl.BlockSpec((1, D), const2),
            pl.BlockSpec((1, 1, D), mod_map),
            pl.BlockSpec((1, 1, D), mod_map),
            pl.BlockSpec(w["w1"].shape, const2),
            pl.BlockSpec(w["g1"].shape, const2),
            pl.BlockSpec(w["wuq"].shape, const2),
            pl.BlockSpec(w["gq"].shape, const2),
            pl.BlockSpec(w["wukv"].shape, const2),
            pl.BlockSpec(w["gk"].shape, const2),
            rope_spec, rope_spec, rope_spec,
        ],
        out_specs=[
            pl.BlockSpec((1, tm, nq), lambda b, i: (b, i, 0)),
            pl.BlockSpec((1, tm, nq), lambda b, i: (b, i, 0)),
            pl.BlockSpec((1, H, MLA_V, tm), lambda b, i: (b, 0, 0, i)),
        ],
        out_shape=[
            jax.ShapeDtypeStruct((B, R, nq), BF16),
            jax.ShapeDtypeStruct((B, R, nq), BF16),
            jax.ShapeDtypeStruct((B, H, MLA_V, R), BF16),
        ],
        compiler_params=_cparams(("parallel", "parallel")),
        name="mla_qkv",
    )(x, g, shift, scale, w["w1"], w["g1"], w["wuq"], w["gq"], w["wukv"], w["gk"], *rope)


def _gqa_qkv_kernel(x_ref, g_ref, sh_ref, sc_ref, w_ref, gq_ref, gk_ref,
                    cos_ref, sa_ref, sb_ref, q_ref, k_ref, vt_ref):
    h = _modulate(x_ref[0], g_ref[...], sh_ref[0], sc_ref[0]).astype(BF16)
    cos, sa, sb = cos_ref[...], sa_ref[...], sb_ref[...]
    quarter = GQA_HEAD_DIM // 4
    hd = GQA_HEAD_DIM
    group = GQA_KV_HEADS * hd
    nq = GQA_HEADS * hd
    gq = gq_ref[...] * (GQA_SCALE * LOG2E)
    gk = gk_ref[...]
    n_groups = (nq + 2 * group) // group
    y_next = _dot(h, w_ref[:, 0:group])
    for gi in range(n_groups):
        c0 = gi * group
        y = y_next
        if gi + 1 < n_groups:
            y_next = _dot(h, w_ref[:, c0 + group:c0 + 2 * group])
        for j in range(0, group, hd):
            if c0 < nq:
                qh = _rope(_rms(y[:, j:j + hd], hd) * gq, cos, sa, sb, quarter)
                q_ref[0, :, c0 + j:c0 + j + hd] = qh.astype(BF16)
            elif c0 == nq:
                kh = _rope(_rms(y[:, j:j + hd], hd) * gk, cos, sa, sb, quarter)
                k_ref[0, :, j:j + hd] = kh.astype(BF16)
            else:
                vt_ref[0, j // hd] = y[:, j:j + hd].T.astype(BF16)


def _gqa_qkv(x, g, shift, scale, w, rope, tm, rope_blk0):
    B, R, D = x.shape
    nq = GQA_HEADS * GQA_HEAD_DIM
    nk = GQA_KV_HEADS * GQA_HEAD_DIM
    const2 = lambda b, i: (0, 0)
    mod_map = (lambda b, i: (b, 0, 0)) if shift.shape[0] == B else (lambda b, i: (0, 0, 0))
    rope_spec = pl.BlockSpec((tm, LANES), lambda b, i: (rope_blk0 + i, 0))
    return pl.pallas_call(
        _gqa_qkv_kernel,
        grid=(B, R // tm),
        in_specs=[
            pl.BlockSpec((1, tm, D), lambda b, i: (b, i, 0)),
            pl.BlockSpec((1, D), const2),
            pl.BlockSpec((1, 1, D), mod_map),
            pl.BlockSpec((1, 1, D), mod_map),
            pl.BlockSpec(w["w"].shape, const2),
            pl.BlockSpec(w["gq"].shape, const2),
            pl.BlockSpec(w["gk"].shape, const2),
            rope_spec, rope_spec, rope_spec,
        ],
        out_specs=[
            pl.BlockSpec((1, tm, nq), lambda b, i: (b, i, 0)),
            pl.BlockSpec((1, tm, nk), lambda b, i: (b, i, 0)),
            pl.BlockSpec((1, GQA_KV_HEADS, GQA_HEAD_DIM, tm), lambda b, i: (b, 0, 0, i)),
        ],
        out_shape=[
            jax.ShapeDtypeStruct((B, R, nq), BF16),
            jax.ShapeDtypeStruct((B, R, nk), BF16),
            jax.ShapeDtypeStruct((B, GQA_KV_HEADS, GQA_HEAD_DIM, R), BF16),
        ],
        compiler_params=_cparams(("parallel", "parallel")),
        name="gqa_qkv",
    )(x, g, shift, scale, w["w"], w["gq"], w["gk"], *rope)


def _attn_kernel(bound_ref, *refs, chunks, n_src, group):
    q_ref = refs[0]
    kv_refs = refs[1:1 + 2 * n_src]
    o_ref = refs[1 + 2 * n_src]
    s_buf, p_buf, acc_ref = refs[2 + 2 * n_src:]
    n = len(chunks)
    bound = bound_ref[pl.program_id(0), pl.program_id(1) * group + pl.program_id(2)]

    def qk(c):
        src, start, size = chunks[c]
        k_c = kv_refs[2 * src][0, start:start + size, :]
        return lax.dot_general(k_c, q_ref[0], (((1,), (1,)), ((), ())),
                               preferred_element_type=F32)

    def vt(c):
        src, start, size = chunks[c]
        return kv_refs[2 * src + 1][0, 0, :, start:start + size]

    def bounded():
        acc = l = p_prev = None
        s_next = qk(0)
        for c in range(n):
            s = s_next
            if c + 1 < n:
                s_next = qk(c + 1)
            p = jnp.exp2(s - bound)
            psum = jnp.sum(p, axis=0, keepdims=True)
            l = psum if c == 0 else l + psum
            if c >= 1:
                y = _dot(vt(c - 1), p_prev)
                acc = y if c == 1 else acc + y
            p_prev = p.astype(BF16)
        y = _dot(vt(n - 1), p_prev)
        acc = y if n == 1 else acc + y
        o_ref[0] = (acc / l).T.astype(o_ref.dtype)

    def online():
        def qk_store(c):
            s = qk(c)
            s_buf[c % 2, 0:chunks[c][2], :] = s
            return jnp.max(s, axis=0, keepdims=True)

        def pv(c, alpha):
            y = _dot(vt(c), p_buf[c % 2, 0:chunks[c][2], :])
            if c == 0:
                acc_ref[...] = y
            else:
                acc_ref[...] = alpha * acc_ref[...] + y

        m_next = qk_store(0)
        m = l = alpha_prev = None
        for c in range(n):
            size = chunks[c][2]
            m_c = m_next
            if c + 1 < n:
                m_next = qk_store(c + 1)
            s = s_buf[c % 2, 0:size, :]
            if c == 0:
                m_new, alpha = m_c, None
            else:
                m_new = jnp.maximum(m, m_c)
                alpha = jnp.exp2(m - m_new)
            p = jnp.exp2(s - m_new)
            p_buf[c % 2, 0:size, :] = p.astype(BF16)
            psum = jnp.sum(p, axis=0, keepdims=True)
            l = psum if c == 0 else alpha * l + psum
            m = m_new
            if c >= 1:
                pv(c - 1, alpha_prev)
            alpha_prev = alpha
        pv(n - 1, alpha_prev)
        o_ref[0] = (acc_ref[...] / l).T.astype(o_ref.dtype)

    lax.cond(bound < SAFE_LOG2_SPAN, bounded, online)


def _attention(q, kv_sources, bound, n_heads, n_kv_heads, dq, dv, tq):
    B, Q, _ = q.shape
    G = n_heads // n_kv_heads
    chunks = []
    in_specs = [pl.BlockSpec((1, tq, dq), lambda b, hk, g, i, bnd: (b, i, hk * G + g))]
    args = [q]
    for src, (k, vt) in enumerate(kv_sources):
        T = k.shape[1]
        ck = min(ATTN_K_CHUNK, T)
        chunks += [(src, s, ck) for s in range(0, T, ck)]
        in_specs += [
            pl.BlockSpec((1, T, dq), lambda b, hk, g, i, bnd: (b, 0, hk)),
            pl.BlockSpec((1, 1, dv, T), lambda b, hk, g, i, bnd: (b, hk, 0, 0)),
        ]
        args += [k, vt]
    ck_max = max(size for _, _, size in chunks)
    return pl.pallas_call(
        functools.partial(_attn_kernel, chunks=tuple(chunks), n_src=len(kv_sources), group=G),
        grid_spec=pltpu.PrefetchScalarGridSpec(
            num_scalar_prefetch=1,
            grid=(B, n_kv_heads, G, Q // tq),
            in_specs=in_specs,
            out_specs=pl.BlockSpec((1, tq, dv), lambda b, hk, g, i, bnd: (b, i, hk * G + g)),
            scratch_shapes=[
                pltpu.VMEM((2, ck_max, tq), F32),
                pltpu.VMEM((2, ck_max, tq), BF16),
                pltpu.VMEM((dv, tq), F32),
            ],
        ),
        out_shape=jax.ShapeDtypeStruct((B, Q, n_heads * dv), BF16),
        compiler_params=_cparams(("parallel", "parallel", "parallel", "parallel")),
        name="attention",
    )(bound, *args)


def _score_bound(parts_q, parts_k, qscale, batch, n_heads):
    q2 = sum(n * jnp.max(g * g) for n, g in parts_q) * (qscale * qscale)
    k2 = sum(n * jnp.max(g * g) for n, g in parts_k)
    return jnp.full((batch, n_heads), jnp.sqrt(q2 * k2) * BOUND_MARGIN, F32)


def _oproj_kernel(o_ref, w_ref, x_ref, gate_ref, out_ref, *, col_tile):
    o = o_ref[0]
    gate = gate_ref[0]
    for c0 in range(0, out_ref.shape[-1], col_tile):
        y = _dot(o, w_ref[:, c0:c0 + col_tile])
        out_ref[0, :, c0:c0 + col_tile] = x_ref[0, :, c0:c0 + col_tile] + gate[:, c0:c0 + col_tile] * y


def _oproj(o, w_o, x, gate, tm):
    B, R, D = x.shape
    K = o.shape[-1]
    mod_map = (lambda b, i: (b, 0, 0)) if gate.shape[0] == B else (lambda b, i: (0, 0, 0))
    return pl.pallas_call(
        functools.partial(_oproj_kernel, col_tile=512),
        grid=(B, R // tm),
        in_specs=[
            pl.BlockSpec((1, tm, K), lambda b, i: (b, i, 0)),
            pl.BlockSpec((K, D), lambda b, i: (0, 0)),
            pl.BlockSpec((1, tm, D), lambda b, i: (b, i, 0)),
            pl.BlockSpec((1, 1, D), mod_map),
        ],
        out_specs=pl.BlockSpec((1, tm, D), lambda b, i: (b, i, 0)),
        out_shape=jax.ShapeDtypeStruct((B, R, D), F32),
        compiler_params=_cparams(("parallel", "parallel")),
        name="oproj",
    )(o, w_o, x, gate)


def _ffn_up_kernel(x_ref, g_ref, sh_ref, sc_ref, w_ref, u_ref, h_ref):
    @pl.when(pl.program_id(2) == 0)
    def _():
        h_ref[...] = _modulate(x_ref[0], g_ref[...], sh_ref[0], sc_ref[0]).astype(BF16)

    u_ref[0] = _dot(h_ref[...], w_ref[...]).astype(u_ref.dtype)


def _ffn_up(x, g, shift, scale, w_up, tm, tn):
    B, R, D = x.shape
    N = w_up.shape[1]
    mod_map = (lambda b, i, j: (b, 0, 0)) if shift.shape[0] == B else (lambda b, i, j: (0, 0, 0))
    return pl.pallas_call(
        _ffn_up_kernel,
        grid=(B, R // tm, N // tn),
        in_specs=[
            pl.BlockSpec((1, tm, D), lambda b, i, j: (b, i, 0)),
            pl.BlockSpec((1, D), lambda b, i, j: (0, 0)),
            pl.BlockSpec((1, 1, D), mod_map),
            pl.BlockSpec((1, 1, D), mod_map),
            pl.BlockSpec((D, tn), lambda b, i, j: (0, j)),
        ],
        out_specs=pl.BlockSpec((1, tm, tn), lambda b, i, j: (b, i, j)),
        out_shape=jax.ShapeDtypeStruct((B, R, N), BF16),
        scratch_shapes=[pltpu.VMEM((tm, D), BF16)],
        compiler_params=_cparams(("parallel", "parallel", "arbitrary")),
        name="ffn_up",
    )(x, g, shift, scale, w_up)


def _ffn_down_kernel(ug_ref, uv_ref, prev_ref, next_ref, cw_ref, cb_ref, w_ref, x_ref, gate_ref,
                     out_ref, win_ref):
    i = pl.program_id(1)
    kk = pl.program_id(2)
    tm = ug_ref.shape[1]
    first = i == 0
    last = i == pl.num_programs(1) - 1
    prev_row = prev_ref[0, HALO - 1:HALO, :].astype(F32)
    next_row = next_ref[0, 0:1, :].astype(F32)
    win_ref[HALO - 1:HALO, :] = jnp.where(first, 0.0, prev_row)
    win_ref[HALO + tm:HALO + tm + 1, :] = jnp.where(last, 0.0, next_row)
    win_ref[HALO:HALO + tm, :] = ug_ref[0].astype(F32)
    cw = cw_ref[...]
    conv = (win_ref[HALO - 1:HALO - 1 + tm, :] * cw[0:1, :]
            + win_ref[HALO:HALO + tm, :] * cw[1:2, :]
            + win_ref[HALO + 1:HALO + 1 + tm, :] * cw[2:3, :]
            + cb_ref[...])
    a = (conv * jax.nn.sigmoid(conv)) * uv_ref[0].astype(F32)
    y = _dot(a.astype(BF16), w_ref[...])

    @pl.when(kk == 0)
    def _():
        out_ref[0] = y

    @pl.when(kk > 0)
    def _():
        out_ref[0] += y

    @pl.when(kk == pl.num_programs(2) - 1)
    def _():
        out_ref[0] = x_ref[0] + gate_ref[0] * out_ref[0]


def _ffn_down(u, conv_w, conv_b, w_down, x, gate, tm, tk):
    B, R, D = x.shape
    F = w_down.shape[0]
    nk = F // tk
    rb = tm // HALO
    n_halo_blocks = R // HALO
    mod_map = (lambda b, i, k: (b, 0, 0)) if gate.shape[0] == B else (lambda b, i, k: (0, 0, 0))
    return pl.pallas_call(
        _ffn_down_kernel,
        grid=(B, R // tm, nk),
        in_specs=[
            pl.BlockSpec((1, tm, tk), lambda b, i, k: (b, i, k)),
            pl.BlockSpec((1, tm, tk), lambda b, i, k: (b, i, nk + k)),
            pl.BlockSpec((1, HALO, tk), lambda b, i, k: (b, jnp.maximum(i * rb - 1, 0), k)),
            pl.BlockSpec((1, HALO, tk),
                         lambda b, i, k: (b, jnp.minimum((i + 1) * rb, n_halo_blocks - 1), k)),
            pl.BlockSpec((CONV_W, tk), lambda b, i, k: (0, k)),
            pl.BlockSpec((1, tk), lambda b, i, k: (0, k)),
            pl.BlockSpec((tk, D), lambda b, i, k: (k, 0)),
            pl.BlockSpec((1, tm, D), lambda b, i, k: (b, i, 0)),
            pl.BlockSpec((1, 1, D), mod_map),
        ],
        out_specs=pl.BlockSpec((1, tm, D), lambda b, i, k: (b, i, 0)),
        out_shape=jax.ShapeDtypeStruct((B, R, D), F32),
        scratch_shapes=[pltpu.VMEM((tm + 2 * HALO, tk), F32)],
        compiler_params=_cparams(("parallel", "parallel", "arbitrary")),
        name="ffn_down",
    )(u, u, u, u, conv_w, conv_b.reshape(1, F), w_down, x, gate)


def _rope_tables(seq, ctx_len, rot_dim):
    t = jnp.arange(seq, dtype=jnp.int32)
    rows = (t // GRID_W).astype(F32)
    cols = (t % GRID_W).astype(F32)
    axis_dim = rot_dim // 2
    inv = jnp.power(ROPE_BASE, -jnp.arange(0, axis_dim, 2, dtype=F32) / axis_dim)
    ang_r = rows[:, None] * inv
    ang_c = cols[:, None] * inv
    ang = jnp.concatenate([ang_r, ang_r, ang_c, ang_c], axis=-1)
    cos, sin = jnp.cos(ang), jnp.sin(ang)
    lane = jnp.arange(rot_dim)
    first_quarter = (lane % axis_dim) < (axis_dim // 2)
    sin_a = jnp.where(first_quarter, -sin, 0.0)
    sin_b = jnp.where(first_quarter, 0.0, sin)

    def finish(tab, ctx_value):
        tab = jnp.concatenate([tab, jnp.full((ctx_len, rot_dim), ctx_value, F32)], axis=0)
        return jnp.pad(tab, ((0, 0), (0, LANES - rot_dim)))

    return finish(cos, 1.0), finish(sin_a, 0.0), finish(sin_b, 0.0)


def _mla_weights(j, w_dq, g_dq, w_uq, g_q_nope, g_q_pe, w_dkv, g_dkv, g_k_pe, w_ukv, g_k_nope):
    D = w_dq.shape[1]
    pad = LANES - MLA_ROPE
    w1 = jnp.concatenate([w_dq[j], w_dkv[j], jnp.zeros((D, pad), F32)], axis=1).astype(BF16)
    g1 = jnp.concatenate([g_dq[j], g_dkv[j], g_k_pe[j], jnp.zeros((pad,), F32)])[None, :]
    wuq = w_uq[j].reshape(MLA_Q_RANK, MLA_HEADS, MLA_NOPE + MLA_ROPE)
    wuq = jnp.pad(wuq, ((0, 0), (0, 0), (0, MLA_QK_PAD - MLA_NOPE - MLA_ROPE)))
    wuq = wuq.reshape(MLA_Q_RANK, MLA_HEADS * MLA_QK_PAD).astype(BF16)
    gq = jnp.concatenate([g_q_nope[j], g_q_pe[j], jnp.zeros((pad,), F32)])[None, :]
    return dict(w1=w1, g1=g1, wuq=wuq, gq=gq, wukv=w_ukv[j].astype(BF16), gk=g_k_nope[j][None, :])


def _gqa_weights(j, w_q, g_q, w_kv, g_k):
    w = jnp.concatenate([w_q[j], w_kv[j]], axis=1).astype(BF16)
    return dict(w=w, gq=g_q[j][None, :], gk=g_k[j][None, :])


def kernel(x, c, ctx, c_ctx, w_mod, b_mod, norm_mix, norm_ffn, mla_w_dq, mla_g_dq, mla_w_uq, mla_g_q_nope, mla_g_q_pe, mla_w_dkv, mla_g_dkv, mla_g_k_pe, mla_w_ukv, mla_g_k_nope, mla_w_o, gqa_w_q, gqa_g_q, gqa_w_kv, gqa_g_k, gqa_w_o, ffn_w_up, ffn_conv_w, ffn_conv_b, ffn_w_down):
    B, S, D = x.shape
    C = ctx.shape[1]
    depth = w_mod.shape[0]
    assert S % FFN_UP_ROW_TILE == 0 and S % ROW_TILE == 0 and C % HALO == 0 and C <= ROW_TILE

    cvec = jnp.concatenate([c, c_ctx[None, :], jnp.zeros((8 - B - 1, D), F32)], axis=0)
    mod = _mod_all(cvec, w_mod, b_mod).reshape(depth, 8, N_MOD, D)
    rope_mla = _rope_tables(S, C, MLA_ROPE)
    rope_gqa = _rope_tables(S, C, GQA_HEAD_DIM)

    for i in range(depth):
        last = i == depth - 1
        j = i // 2
        lat = [mod[i, :B, n][:, None, :] for n in range(N_MOD)]
        cmod = [mod[i, B:B + 1, n][:, None, :] for n in range(N_MOD)]
        g_mix = norm_mix[i][None, :]
        g_ffn = norm_ffn[i][None, :]

        if i % 2 == 0:
            w = _mla_weights(j, mla_w_dq, mla_g_dq, mla_w_uq, mla_g_q_nope, mla_g_q_pe,
                             mla_w_dkv, mla_g_dkv, mla_g_k_pe, mla_w_ukv, mla_g_k_nope)
            qkv, rope = _mla_qkv, rope_mla
            bound = _score_bound([(MLA_NOPE, mla_g_q_nope[j]), (MLA_ROPE, mla_g_q_pe[j])],
                                 [(MLA_NOPE, mla_g_k_nope[j]), (MLA_ROPE, mla_g_k_pe[j])],
                                 MLA_SCALE * LOG2E, B, MLA_HEADS)
            w_o = mla_w_o[j].astype(BF16)
            heads, kv_heads, dq, dv = MLA_HEADS, MLA_HEADS, MLA_QK_PAD, MLA_V
        else:
            w = _gqa_weights(j, gqa_w_q, gqa_g_q, gqa_w_kv, gqa_g_k)
            qkv, rope = _gqa_qkv, rope_gqa
            bound = _score_bound([(GQA_HEAD_DIM, gqa_g_q[j])], [(GQA_HEAD_DIM, gqa_g_k[j])],
                                 GQA_SCALE * LOG2E, B, GQA_HEADS)
            w_o = gqa_w_o[j].astype(BF16)
            heads, kv_heads, dq, dv = GQA_HEADS, GQA_KV_HEADS, GQA_HEAD_DIM, GQA_HEAD_DIM

        q, k, vt = qkv(x, g_mix, lat[0], lat[1], w, rope, ROW_TILE, 0)
        qc, kc, vtc = qkv(ctx, g_mix, cmod[0], cmod[1], w, rope, C, S // C)
        o = _attention(q, [(k, vt), (kc, vtc)], bound, heads, kv_heads, dq, dv, ATTN_Q_TILE)
        x = _oproj(o, w_o, x, lat[2], ROW_TILE)
        if not last:
            oc = _attention(qc, [(kc, vtc)], bound, heads, kv_heads, dq, dv, C)
            ctx = _oproj(oc, w_o, ctx, cmod[2], C)

        w_up = ffn_w_up[i].astype(BF16)
        w_down = ffn_w_down[i].astype(BF16)
        u = _ffn_up(x, g_ffn, lat[3], lat[4], w_up, FFN_UP_ROW_TILE, FFN_UP_COL_TILE)
        x = _ffn_down(u, ffn_conv_w[i], ffn_conv_b[i], w_down, x, lat[5],
                      FFN_DOWN_ROW_TILE, FFN_DOWN_K_TILE)
        if not last:
            uc = _ffn_up(ctx, g_ffn, cmod[3], cmod[4], w_up, C, FFN_UP_COL_TILE)
            ctx = _ffn_down(uc, ffn_conv_w[i], ffn_conv_b[i], w_down, ctx, cmod[5],
                            C, FFN_DOWN_K_TILE)
    return x
```

```python
import functools
import math

import jax
import jax.numpy as jnp
from jax import lax
from jax.experimental import pallas as pl
from jax.experimental.pallas import tpu as pltpu

F32 = jnp.float32
BF16 = jnp.bfloat16

GRID_W = 64
N_MOD = 6
MLA_HEADS = 16
MLA_Q_RANK = 512
MLA_KV_RANK = 512
MLA_NOPE = 128
MLA_ROPE = 64
MLA_V = 128
MLA_QK_PAD = 256
GQA_HEADS = 16
GQA_KV_HEADS = 4
GQA_HEAD_DIM = 128
CONV_W = 3
ROPE_BASE = 10000.0
EPS = 1e-6
LOG2E = 1.4426950408889634
MLA_SCALE = 1.0 / math.sqrt(MLA_NOPE + MLA_ROPE)
GQA_SCALE = 1.0 / math.sqrt(GQA_HEAD_DIM)

SAFE_LOG2_SPAN = 60.0
BOUND_MARGIN = 1.02

LANES = 128
BF16_SUBLANES = 16
VMEM_LIMIT_BYTES = 56 * 1024 * 1024

ROW_TILE = 512
FFN_UP_ROW_TILE = 1024
FFN_UP_COL_TILE = 1024
FFN_DOWN_ROW_TILE = 512
FFN_DOWN_K_TILE = 1408
ATTN_Q_TILE = 512
ATTN_K_CHUNK = 512
MOD_COL_TILE = 1024
HALO = BF16_SUBLANES


def _cparams(sem):
    return pltpu.CompilerParams(dimension_semantics=sem, vmem_limit_bytes=VMEM_LIMIT_BYTES)


def _dot(a, b):
    return jnp.dot(a, b, preferred_element_type=F32)


def _rms(y, n):
    return y * lax.rsqrt(jnp.sum(y * y, axis=-1, keepdims=True) * (1.0 / n) + EPS)


def _modulate(x, g, shift, scale):
    return (_rms(x, x.shape[-1]) * g) * (1.0 + scale) + shift


def _rope(x, cos, sin_a, sin_b, quarter):
    return (x * cos + pltpu.roll(x, LANES - quarter, 1) * sin_a
            + pltpu.roll(x, quarter, 1) * sin_b)


def _mod_kernel(c_ref, w_ref, b_ref, o_ref):
    c = c_ref[...]
    sc = (c * jax.nn.sigmoid(c)).astype(BF16)
    o_ref[0] = _dot(sc, w_ref[0].astype(BF16)) + b_ref[0]


def _mod_all(cvec, w_mod, b_mod):
    L, D, N = w_mod.shape
    R = cvec.shape[0]
    return pl.pallas_call(
        _mod_kernel,
        grid=(L, N // MOD_COL_TILE),
        in_specs=[
            pl.BlockSpec((R, D), lambda l, j: (0, 0)),
            pl.BlockSpec((1, D, MOD_COL_TILE), lambda l, j: (l, 0, j)),
            pl.BlockSpec((1, 1, MOD_COL_TILE), lambda l, j: (l, 0, j)),
        ],
        out_specs=pl.BlockSpec((1, R, MOD_COL_TILE), lambda l, j: (l, 0, j)),
        out_shape=jax.ShapeDtypeStruct((L, R, N), F32),
        compiler_params=_cparams(("parallel", "parallel")),
        name="mod_all",
    )(cvec, w_mod, b_mod.reshape(L, 1, N))


def _mla_qkv_kernel(x_ref, g_ref, sh_ref, sc_ref, w1_ref, g1_ref, wuq_ref, gq_ref,
                    wukv_ref, gk_ref, cos_ref, sa_ref, sb_ref,
                    q_ref, k_ref, vt_ref):
    h = _modulate(x_ref[0], g_ref[...], sh_ref[0], sc_ref[0]).astype(BF16)
    y1 = _dot(h, w1_ref[...])
    g1 = g1_ref[...]
    cq = (_rms(y1[:, :MLA_Q_RANK], MLA_Q_RANK) * g1[:, :MLA_Q_RANK]).astype(BF16)
    lo, hi = MLA_Q_RANK, MLA_Q_RANK + MLA_KV_RANK
    ckv = (_rms(y1[:, lo:hi], MLA_KV_RANK) * g1[:, lo:hi]).astype(BF16)
    cos, sa, sb = cos_ref[...], sa_ref[...], sb_ref[...]
    quarter = MLA_ROPE // 4
    kpe = _rms(y1[:, hi:], MLA_ROPE) * g1[:, hi:]
    kpe = _rope(kpe, cos, sa, sb, quarter).astype(BF16)
    gq = gq_ref[...]
    gk = gk_ref[...]
    qscale = MLA_SCALE * LOG2E

    def head_dots(hh):
        c0 = hh * MLA_QK_PAD
        return (_dot(cq, wuq_ref[:, c0:c0 + MLA_QK_PAD]),
                _dot(ckv, wukv_ref[:, c0:c0 + MLA_NOPE + MLA_V]))

    nxt = head_dots(0)
    for hh in range(MLA_HEADS):
        c0 = hh * MLA_QK_PAD
        qh, kvh = nxt
        if hh + 1 < MLA_HEADS:
            nxt = head_dots(hh + 1)
        qn = _rms(qh[:, :MLA_NOPE], MLA_NOPE) * gq[:, :MLA_NOPE]
        qp = _rms(qh[:, MLA_NOPE:], MLA_ROPE) * gq[:, MLA_NOPE:]
        qp = _rope(qp, cos, sa, sb, quarter)
        q_ref[0, :, c0:c0 + MLA_NOPE] = (qn * qscale).astype(BF16)
        q_ref[0, :, c0 + MLA_NOPE:c0 + MLA_QK_PAD] = (qp * qscale).astype(BF16)
        kn = _rms(kvh[:, :MLA_NOPE], MLA_NOPE) * gk
        k_ref[0, :, c0:c0 + MLA_NOPE] = kn.astype(BF16)
        k_ref[0, :, c0 + MLA_NOPE:c0 + MLA_QK_PAD] = kpe
        vt_ref[0, hh] = kvh[:, MLA_NOPE:].T.astype(BF16)


def _mla_qkv(x, g, shift, scale, w, rope, tm, rope_blk0):
    B, R, D = x.shape
    H = MLA_HEADS
    nq = H * MLA_QK_PAD
    const2 = lambda b, i: (0, 0)
    mod_map = (lambda b, i: (b, 0, 0)) if shift.shape[0] == B else (lambda b, i: (0, 0, 0))
    rope_spec = pl.BlockSpec((tm, LANES), lambda b, i: (rope_blk0 + i, 0))
    return pl.pallas_call(
        _mla_qkv_kernel,
        grid=(B, R // tm),
        in_specs=[
            pl.BlockSpec((1, tm, D), lambda b, i: (b, i, 0)),
            pl.BlockSpec((1, D), const2),
            pl.BlockSpec((1, 1, D), mod_map),
            pl.BlockSpec((1, 1, D), mod_map),
            pl.BlockSpec(w["w1"].shape, const2),
            pl.BlockSpec(w["g1"].shape, const2),
            pl.BlockSpec(w["wuq"].shape, const2),
            pl.BlockSpec(w["gq"].shape, const2),
            pl.BlockSpec(w["wukv"].shape, const2),
            pl.BlockSpec(w["gk"].shape, const2),
            rope_spec, rope_spec, rope_spec,
        ],
        out_specs=[
            pl.BlockSpec((1, tm, nq), lambda b, i: (b, i, 0)),
            pl.BlockSpec((1, tm, nq), lambda b, i: (b, i, 0)),
            pl.BlockSpec((1, H, MLA_V, tm), lambda b, i: (b, 0, 0, i)),
        ],
        out_shape=[
            jax.ShapeDtypeStruct((B, R, nq), BF16),
            jax.ShapeDtypeStruct((B, R, nq), BF16),
            jax.ShapeDtypeStruct((B, H, MLA_V, R), BF16),
        ],
        compiler_params=_cparams(("parallel", "parallel")),
        name="mla_qkv",
    )(x, g, shift, scale, w["w1"], w["g1"], w["wuq"], w["gq"], w["wukv"], w["gk"], *rope)


def _gqa_qkv_kernel(x_ref, g_ref, sh_ref, sc_ref, w_ref, gq_ref, gk_ref,
                    cos_ref, sa_ref, sb_ref, q_ref, k_ref, vt_ref):
    h = _modulate(x_ref[0], g_ref[...], sh_ref[0], sc_ref[0]).astype(BF16)
    cos, sa, sb = cos_ref[...], sa_ref[...], sb_ref[...]
    quarter = GQA_HEAD_DIM // 4
    hd = GQA_HEAD_DIM
    group = GQA_KV_HEADS * hd
    nq = GQA_HEADS * hd
    gq = gq_ref[...] * (GQA_SCALE * LOG2E)
    gk = gk_ref[...]
    n_groups = (nq + 2 * group) // group
    y_next = _dot(h, w_ref[:, 0:group])
    for gi in range(n_groups):
        c0 = gi * group
        y = y_next
        if gi + 1 < n_groups:
            y_next = _dot(h, w_ref[:, c0 + group:c0 + 2 * group])
        for j in range(0, group, hd):
            if c0 < nq:
                qh = _rope(_rms(y[:, j:j + hd], hd) * gq, cos, sa, sb, quarter)
                q_ref[0, :, c0 + j:c0 + j + hd] = qh.astype(BF16)
            elif c0 == nq:
                kh = _rope(_rms(y[:, j:j + hd], hd) * gk, cos, sa, sb, quarter)
                k_ref[0, :, j:j + hd] = kh.astype(BF16)
            else:
                vt_ref[0, j // hd] = y[:, j:j + hd].T.astype(BF16)


def _gqa_qkv(x, g, shift, scale, w, rope, tm, rope_blk0):
    B, R, D = x.shape
    nq = GQA_HEADS * GQA_HEAD_DIM
    nk = GQA_KV_HEADS * GQA_HEAD_DIM
    const2 = lambda b, i: (0, 0)
    mod_map = (lambda b, i: (b, 0, 0)) if shift.shape[0] == B else (lambda b, i: (0, 0, 0))
    rope_spec = pl.BlockSpec((tm, LANES), lambda b, i: (rope_blk0 + i, 0))
    return pl.pallas_call(
        _gqa_qkv_kernel,
        grid=(B, R // tm),
        in_specs=[
            pl.BlockSpec((1, tm, D), lambda b, i: (b, i, 0)),
            pl.BlockSpec((1, D), const2),
            pl.BlockSpec((1, 1, D), mod_map),
            pl.BlockSpec((1, 1, D), mod_map),
            pl.BlockSpec(w["w"].shape, const2),
            pl.BlockSpec(w["gq"].shape, const2),
            pl.BlockSpec(w["gk"].shape, const2),
            rope_spec, rope_spec, rope_spec,
        ],
        out_specs=[
            pl.BlockSpec((1, tm, nq), lambda b, i: (b, i, 0)),
            pl.BlockSpec((1, tm, nk), lambda b, i: (b, i, 0)),
            pl.BlockSpec((1, GQA_KV_HEADS, GQA_HEAD_DIM, tm), lambda b, i: (b, 0, 0, i)),
        ],
        out_shape=[
            jax.ShapeDtypeStruct((B, R, nq), BF16),
            jax.ShapeDtypeStruct((B, R, nk), BF16),
            jax.ShapeDtypeStruct((B, GQA_KV_HEADS, GQA_HEAD_DIM, R), BF16),
        ],
        compiler_params=_cparams(("parallel", "parallel")),
        name="gqa_qkv",
    )(x, g, shift, scale, w["w"], w["gq"], w["gk"], *rope)


def _attn_kernel(bound_ref, *refs, chunks, n_src, group):
    q_ref = refs[0]
    kv_refs = refs[1:1 + 2 * n_src]
    o_ref = refs[1 + 2 * n_src]
    s_buf, p_buf, acc_ref = refs[2 + 2 * n_src:]
    n = len(chunks)
    bound = bound_ref[pl.program_id(0), pl.program_id(1) * group + pl.program_id(2)]

    def qk(c):
        src, start, size = chunks[c]
        k_c = kv_refs[2 * src][0, start:start + size, :]
        return lax.dot_general(k_c, q_ref[0], (((1,), (1,)), ((), ())),
                               preferred_element_type=F32)

    def vt(c):
        src, start, size = chunks[c]
        return kv_refs[2 * src + 1][0, 0, :, start:start + size]

    def bounded():
        acc = l = p_prev = None
        s_next = qk(0)
        for c in range(n):
            s = s_next
            if c + 1 < n:
                s_next = qk(c + 1)
            p = jnp.exp2(s - bound)
            psum = jnp.sum(p, axis=0, keepdims=True)
            l = psum if c == 0 else l + psum
            if c >= 1:
                y = _dot(vt(c - 1), p_prev)
                acc = y if c == 1 else acc + y
            p_prev = p.astype(BF16)
        y = _dot(vt(n - 1), p_prev)
        acc = y if n == 1 else acc + y
        o_ref[0] = (acc / l).T.astype(o_ref.dtype)

    def online():
        def qk_store(c):
            s = qk(c)
            s_buf[c % 2, 0:chunks[c][2], :] = s
            return jnp.max(s, axis=0, keepdims=True)

        def pv(c, alpha):
            y = _dot(vt(c), p_buf[c % 2, 0:chunks[c][2], :])
            if c == 0:
                acc_ref[...] = y
            else:
                acc_ref[...] = alpha * acc_ref[...] + y

        m_next = qk_store(0)
        m = l = alpha_prev = None
        for c in range(n):
            size = chunks[c][2]
            m_c = m_next
            if c + 1 < n:
                m_next = qk_store(c + 1)
            s = s_buf[c % 2, 0:size, :]
            if c == 0:
                m_new, alpha = m_c, None
            else:
                m_new = jnp.maximum(m, m_c)
                alpha = jnp.exp2(m - m_new)
            p = jnp.exp2(s - m_new)
            p_buf[c % 2, 0:size, :] = p.astype(BF16)
            psum = jnp.sum(p, axis=0, keepdims=True)
            l = psum if c == 0 else alpha * l + psum
            m = m_new
            if c >= 1:
                pv(c - 1, alpha_prev)
            alpha_prev = alpha
        pv(n - 1, alpha_prev)
        o_ref[0] = (acc_ref[...] / l).T.astype(o_ref.dtype)

    lax.cond(bound < SAFE_LOG2_SPAN, bounded, online)


def _attention(q, kv_sources, bound, n_heads, n_kv_heads, dq, dv, tq):
    B, Q, _ = q.shape
    G = n_heads // n_kv_heads
    chunks = []
    in_specs = [pl.BlockSpec((1, tq, dq), lambda b, hk, g, i, bnd: (b, i, hk * G + g))]
    args = [q]
    for src, (k, vt) in enumerate(kv_sources):
        T = k.shape[1]
        ck = min(ATTN_K_CHUNK, T)
        chunks += [(src, s, ck) for s in range(0, T, ck)]
        in_specs += [
            pl.BlockSpec((1, T, dq), lambda b, hk, g, i, bnd: (b, 0, hk)),
            pl.BlockSpec((1, 1, dv, T), lambda b, hk, g, i, bnd: (b, hk, 0, 0)),
        ]
        args += [k, vt]
    ck_max = max(size for _, _, size in chunks)
    return pl.pallas_call(
        functools.partial(_attn_kernel, chunks=tuple(chunks), n_src=len(kv_sources), group=G),
        grid_spec=pltpu.PrefetchScalarGridSpec(
            num_scalar_prefetch=1,
            grid=(B, n_kv_heads, G, Q // tq),
            in_specs=in_specs,
            out_specs=pl.BlockSpec((1, tq, dv), lambda b, hk, g, i, bnd: (b, i, hk * G + g)),
            scratch_shapes=[
                pltpu.VMEM((2, ck_max, tq), F32),
                pltpu.VMEM((2, ck_max, tq), BF16),
                pltpu.VMEM((dv, tq), F32),
            ],
        ),
        out_shape=jax.ShapeDtypeStruct((B, Q, n_heads * dv), BF16),
        compiler_params=_cparams(("parallel", "parallel", "parallel", "parallel")),
        name="attention",
    )(bound, *args)


def _score_bound(parts_q, parts_k, qscale, batch, n_heads):
    q2 = sum(n * jnp.max(g * g) for n, g in parts_q) * (qscale * qscale)
    k2 = sum(n * jnp.max(g * g) for n, g in parts_k)
    return jnp.full((batch, n_heads), jnp.sqrt(q2 * k2) * BOUND_MARGIN, F32)


def _oproj_kernel(o_ref, w_ref, x_ref, gate_ref, out_ref, *, col_tile):
    o = o_ref[0]
    gate = gate_ref[0]
    for c0 in range(0, out_ref.shape[-1], col_tile):
        y = _dot(o, w_ref[:, c0:c0 + col_tile])
        out_ref[0, :, c0:c0 + col_tile] = x_ref[0, :, c0:c0 + col_tile] + gate[:, c0:c0 + col_tile] * y


def _oproj(o, w_o, x, gate, tm):
    B, R, D = x.shape
    K = o.shape[-1]
    mod_map = (lambda b, i: (b, 0, 0)) if gate.shape[0] == B else (lambda b, i: (0, 0, 0))
    return pl.pallas_call(
        functools.partial(_oproj_kernel, col_tile=512),
        grid=(B, R // tm),
        in_specs=[
            pl.BlockSpec((1, tm, K), lambda b, i: (b, i, 0)),
            pl.BlockSpec((K, D), lambda b, i: (0, 0)),
            pl.BlockSpec((1, tm, D), lambda b, i: (b, i, 0)),
            pl.BlockSpec((1, 1, D), mod_map),
        ],
        out_specs=pl.BlockSpec((1, tm, D), lambda b, i: (b, i, 0)),
        out_shape=jax.ShapeDtypeStruct((B, R, D), F32),
        compiler_params=_cparams(("parallel", "parallel")),
        name="oproj",
    )(o, w_o, x, gate)


def _ffn_up_kernel(x_ref, g_ref, sh_ref, sc_ref, w_ref, u_ref, h_ref):
    @pl.when(pl.program_id(2) == 0)
    def _():
        h_ref[...] = _modulate(x_ref[0], g_ref[...], sh_ref[0], sc_ref[0]).astype(BF16)

    u_ref[0] = _dot(h_ref[...], w_ref[...]).astype(u_ref.dtype)


def _ffn_up(x, g, shift, scale, w_up, tm, tn):
    B, R, D = x.shape
    N = w_up.shape[1]
    mod_map = (lambda b, i, j: (b, 0, 0)) if shift.shape[0] == B else (lambda b, i, j: (0, 0, 0))
    return pl.pallas_call(
        _ffn_up_kernel,
        grid=(B, R // tm, N // tn),
        in_specs=[
            pl.BlockSpec((1, tm, D), lambda b, i, j: (b, i, 0)),
            pl.BlockSpec((1, D), lambda b, i, j: (0, 0)),
            pl.BlockSpec((1, 1, D), mod_map),
            pl.BlockSpec((1, 1, D), mod_map),
            pl.BlockSpec((D, tn), lambda b, i, j: (0, j)),
        ],
        out_specs=pl.BlockSpec((1, tm, tn), lambda b, i, j: (b, i, j)),
        out_shape=jax.ShapeDtypeStruct((B, R, N), BF16),
        scratch_shapes=[pltpu.VMEM((tm, D), BF16)],
        compiler_params=_cparams(("parallel", "parallel", "arbitrary")),
        name="ffn_up",
    )(x, g, shift, scale, w_up)


def _ffn_down_kernel(ug_ref, uv_ref, prev_ref, next_ref, cw_ref, cb_ref, w_ref, x_ref, gate_ref,
                     out_ref):
    i = pl.program_id(1)
    kk = pl.program_id(2)
    tm = ug_ref.shape[1]
    first = i == 0
    last = i == pl.num_programs(1) - 1
    prev_row = jnp.where(first, 0.0, prev_ref[0, HALO - 1:HALO, :].astype(F32))
    next_row = jnp.where(last, 0.0, next_ref[0, 0:1, :].astype(F32))
    g = ug_ref[0].astype(F32)
    row = lax.broadcasted_iota(jnp.int32, g.shape, 0)
    g_prev = jnp.where(row == 0, prev_row, pltpu.roll(g, 1, 0))
    g_next = jnp.where(row == tm - 1, next_row, pltpu.roll(g, tm - 1, 0))
    cw = cw_ref[...]
    conv = g_prev * cw[0:1, :] + g * cw[1:2, :] + g_next * cw[2:3, :] + cb_ref[...]
    half = 0.5 * conv
    a = (half * (1.0 + jnp.tanh(half))) * uv_ref[0].astype(F32)
    y = _dot(a.astype(BF16), w_ref[...])

    @pl.when(kk == 0)
    def _():
        out_ref[0] = y

    @pl.when(kk > 0)
    def _():
        out_ref[0] += y

    @pl.when(kk == pl.num_programs(2) - 1)
    def _():
        out_ref[0] = x_ref[0] + gate_ref[0] * out_ref[0]


def _ffn_down(u, conv_w, conv_b, w_down, x, gate, tm, tk):
    B, R, D = x.shape
    F = w_down.shape[0]
    nk = F // tk
    rb = tm // HALO
    n_halo_blocks = R // HALO
    mod_map = (lambda b, i, k: (b, 0, 0)) if gate.shape[0] == B else (lambda b, i, k: (0, 0, 0))
    return pl.pallas_call(
        _ffn_down_kernel,
        grid=(B, R // tm, nk),
        in_specs=[
            pl.BlockSpec((1, tm, tk), lambda b, i, k: (b, i, k)),
            pl.BlockSpec((1, tm, tk), lambda b, i, k: (b, i, nk + k)),
            pl.BlockSpec((1, HALO, tk), lambda b, i, k: (b, jnp.maximum(i * rb - 1, 0), k)),
            pl.BlockSpec((1, HALO, tk),
                         lambda b, i, k: (b, jnp.minimum((i + 1) * rb, n_halo_blocks - 1), k)),
            pl.BlockSpec((CONV_W, tk), lambda b, i, k: (0, k)),
            pl.BlockSpec((1, tk), lambda b, i, k: (0, k)),
            pl.BlockSpec((tk, D), lambda b, i, k: (k, 0)),
            pl.BlockSpec((1, tm, D), lambda b, i, k: (b, i, 0)),
            pl.BlockSpec((1, 1, D), mod_map),
        ],
        out_specs=pl.BlockSpec((1, tm, D), lambda b, i, k: (b, i, 0)),
        out_shape=jax.ShapeDtypeStruct((B, R, D), F32),
        compiler_params=_cparams(("parallel", "parallel", "arbitrary")),
        name="ffn_down",
    )(u, u, u, u, conv_w, conv_b.reshape(1, F), w_down, x, gate)


def _rope_tables(seq, ctx_len, rot_dim):
    t = jnp.arange(seq, dtype=jnp.int32)
    rows = (t // GRID_W).astype(F32)
    cols = (t % GRID_W).astype(F32)
    axis_dim = rot_dim // 2
    inv = jnp.power(ROPE_BASE, -jnp.arange(0, axis_dim, 2, dtype=F32) / axis_dim)
    ang_r = rows[:, None] * inv
    ang_c = cols[:, None] * inv
    ang = jnp.concatenate([ang_r, ang_r, ang_c, ang_c], axis=-1)
    cos, sin = jnp.cos(ang), jnp.sin(ang)
    lane = jnp.arange(rot_dim)
    first_quarter = (lane % axis_dim) < (axis_dim // 2)
    sin_a = jnp.where(first_quarter, -sin, 0.0)
    sin_b = jnp.where(first_quarter, 0.0, sin)

    def finish(tab, ctx_value):
        tab = jnp.concatenate([tab, jnp.full((ctx_len, rot_dim), ctx_value, F32)], axis=0)
        return jnp.pad(tab, ((0, 0), (0, LANES - rot_dim)))

    return finish(cos, 1.0), finish(sin_a, 0.0), finish(sin_b, 0.0)


def _mla_weights(j, w_dq, g_dq, w_uq, g_q_nope, g_q_pe, w_dkv, g_dkv, g_k_pe, w_ukv, g_k_nope):
    D = w_dq.shape[1]
    pad = LANES - MLA_ROPE
    w1 = jnp.concatenate([w_dq[j], w_dkv[j], jnp.zeros((D, pad), F32)], axis=1).astype(BF16)
    g1 = jnp.concatenate([g_dq[j], g_dkv[j], g_k_pe[j], jnp.zeros((pad,), F32)])[None, :]
    wuq = w_uq[j].reshape(MLA_Q_RANK, MLA_HEADS, MLA_NOPE + MLA_ROPE)
    wuq = jnp.pad(wuq, ((0, 0), (0, 0), (0, MLA_QK_PAD - MLA_NOPE - MLA_ROPE)))
    wuq = wuq.reshape(MLA_Q_RANK, MLA_HEADS * MLA_QK_PAD).astype(BF16)
    gq = jnp.concatenate([g_q_nope[j], g_q_pe[j], jnp.zeros((pad,), F32)])[None, :]
    return dict(w1=w1, g1=g1, wuq=wuq, gq=gq, wukv=w_ukv[j].astype(BF16), gk=g_k_nope[j][None, :])


def _gqa_weights(j, w_q, g_q, w_kv, g_k):
    w = jnp.concatenate([w_q[j], w_kv[j]], axis=1).astype(BF16)
    return dict(w=w, gq=g_q[j][None, :], gk=g_k[j][None, :])


def kernel(x, c, ctx, c_ctx, w_mod, b_mod, norm_mix, norm_ffn, mla_w_dq, mla_g_dq, mla_w_uq, mla_g_q_nope, mla_g_q_pe, mla_w_dkv, mla_g_dkv, mla_g_k_pe, mla_w_ukv, mla_g_k_nope, mla_w_o, gqa_w_q, gqa_g_q, gqa_w_kv, gqa_g_k, gqa_w_o, ffn_w_up, ffn_conv_w, ffn_conv_b, ffn_w_down):
    B, S, D = x.shape
    C = ctx.shape[1]
    depth = w_mod.shape[0]
    assert S % FFN_UP_ROW_TILE == 0 and S % ROW_TILE == 0 and C % HALO == 0 and C <= ROW_TILE

    cvec = jnp.concatenate([c, c_ctx[None, :], jnp.zeros((8 - B - 1, D), F32)], axis=0)
    mod = _mod_all(cvec, w_mod, b_mod).reshape(depth, 8, N_MOD, D)
    rope_mla = _rope_tables(S, C, MLA_ROPE)
    rope_gqa = _rope_tables(S, C, GQA_HEAD_DIM)

    for i in range(depth):
        last = i == depth - 1
        j = i // 2
        lat = [mod[i, :B, n][:, None, :] for n in range(N_MOD)]
        cmod = [mod[i, B:B + 1, n][:, None, :] for n in range(N_MOD)]
        g_mix = norm_mix[i][None, :]
        g_ffn = norm_ffn[i][None, :]

        if i % 2 == 0:
            w = _mla_weights(j, mla_w_dq, mla_g_dq, mla_w_uq, mla_g_q_nope, mla_g_q_pe,
                             mla_w_dkv, mla_g_dkv, mla_g_k_pe, mla_w_ukv, mla_g_k_nope)
            qkv, rope = _mla_qkv, rope_mla
            bound = _score_bound([(MLA_NOPE, mla_g_q_nope[j]), (MLA_ROPE, mla_g_q_pe[j])],
                                 [(MLA_NOPE, mla_g_k_nope[j]), (MLA_ROPE, mla_g_k_pe[j])],
                                 MLA_SCALE * LOG2E, B, MLA_HEADS)
            w_o = mla_w_o[j].astype(BF16)
            heads, kv_heads, dq, dv = MLA_HEADS, MLA_HEADS, MLA_QK_PAD, MLA_V
        else:
            w = _gqa_weights(j, gqa_w_q, gqa_g_q, gqa_w_kv, gqa_g_k)
            qkv, rope = _gqa_qkv, rope_gqa
            bound = _score_bound([(GQA_HEAD_DIM, gqa_g_q[j])], [(GQA_HEAD_DIM, gqa_g_k[j])],
                                 GQA_SCALE * LOG2E, B, GQA_HEADS)
            w_o = gqa_w_o[j].astype(BF16)
            heads, kv_heads, dq, dv = GQA_HEADS, GQA_KV_HEADS, GQA_HEAD_DIM, GQA_HEAD_DIM

        q, k, vt = qkv(x, g_mix, lat[0], lat[1], w, rope, ROW_TILE, 0)
        qc, kc, vtc = qkv(ctx, g_mix, cmod[0], cmod[1], w, rope, C, S // C)
        o = _attention(q, [(k, vt), (kc, vtc)], bound, heads, kv_heads, dq, dv, ATTN_Q_TILE)
        x = _oproj(o, w_o, x, lat[2], ROW_TILE)
        if not last:
            oc = _attention(qc, [(kc, vtc)], bound, heads, kv_heads, dq, dv, C)
            ctx = _oproj(oc, w_o, ctx, cmod[2], C)

        w_up = ffn_w_up[i].astype(BF16)
        w_down = ffn_w_down[i].astype(BF16)
        u = _ffn_up(x, g_ffn, lat[3], lat[4], w_up, FFN_UP_ROW_TILE, FFN_UP_COL_TILE)
        x = _ffn_down(u, ffn_conv_w[i], ffn_conv_b[i], w_down, x, lat[5],
                      FFN_DOWN_ROW_TILE, FFN_DOWN_K_TILE)
        if not last:
            uc = _ffn_up(ctx, g_ffn, cmod[3], cmod[4], w_up, C, FFN_UP_COL_TILE)
            ctx = _ffn_down(uc, ffn_conv_w[i], ffn_conv_b[i], w_down, ctx, cmod[5],
                            C, FFN_DOWN_K_TILE)
    return x
```

```python
import functools
import math

import jax
import jax.numpy as jnp
from jax import lax
from jax.experimental import pallas as pl
from jax.experimental.pallas import tpu as pltpu

F32 = jnp.float32
BF16 = jnp.bfloat16

GRID_W = 64
N_MOD = 6
MLA_HEADS = 16
MLA_Q_RANK = 512
MLA_KV_RANK = 512
MLA_NOPE = 128
MLA_ROPE = 64
MLA_V = 128
MLA_QK_PAD = 256
GQA_HEADS = 16
GQA_KV_HEADS = 4
GQA_HEAD_DIM = 128
CONV_W = 3
ROPE_BASE = 10000.0
EPS = 1e-6
LOG2E = 1.4426950408889634
MLA_SCALE = 1.0 / math.sqrt(MLA_NOPE + MLA_ROPE)
GQA_SCALE = 1.0 / math.sqrt(GQA_HEAD_DIM)

SAFE_LOG2_SPAN = 60.0
BOUND_MARGIN = 1.02

LANES = 128
BF16_SUBLANES = 16
VMEM_LIMIT_BYTES = 56 * 1024 * 1024

ROW_TILE = 512
FFN_UP_ROW_TILE = 1024
FFN_UP_COL_TILE = 1024
FFN_DOWN_ROW_TILE = 512
FFN_DOWN_K_TILE = 1408
ATTN_Q_TILE = 512
ATTN_K_CHUNK = 2048
MOD_COL_TILE = 1024
HALO = BF16_SUBLANES


def _cparams(sem):
    return pltpu.CompilerParams(dimension_semantics=sem, vmem_limit_bytes=VMEM_LIMIT_BYTES)


def _dot(a, b):
    return jnp.dot(a, b, preferred_element_type=F32)


def _rms(y, n):
    return y * lax.rsqrt(jnp.sum(y * y, axis=-1, keepdims=True) * (1.0 / n) + EPS)


def _modulate(x, g, shift, scale):
    return (_rms(x, x.shape[-1]) * g) * (1.0 + scale) + shift


def _rope(x, cos, sin_a, sin_b, quarter):
    return (x * cos + pltpu.roll(x, LANES - quarter, 1) * sin_a
            + pltpu.roll(x, quarter, 1) * sin_b)


def _mod_kernel(c_ref, w_ref, b_ref, o_ref):
    c = c_ref[...]
    sc = (c * jax.nn.sigmoid(c)).astype(BF16)
    o_ref[0] = _dot(sc, w_ref[0].astype(BF16)) + b_ref[0]


def _mod_all(cvec, w_mod, b_mod):
    L, D, N = w_mod.shape
    R = cvec.shape[0]
    return pl.pallas_call(
        _mod_kernel,
        grid=(L, N // MOD_COL_TILE),
        in_specs=[
            pl.BlockSpec((R, D), lambda l, j: (0, 0)),
            pl.BlockSpec((1, D, MOD_COL_TILE), lambda l, j: (l, 0, j)),
            pl.BlockSpec((1, 1, MOD_COL_TILE), lambda l, j: (l, 0, j)),
        ],
        out_specs=pl.BlockSpec((1, R, MOD_COL_TILE), lambda l, j: (l, 0, j)),
        out_shape=jax.ShapeDtypeStruct((L, R, N), F32),
        compiler_params=_cparams(("parallel", "parallel")),
        name="mod_all",
    )(cvec, w_mod, b_mod.reshape(L, 1, N))


def _mla_qkv_kernel(x_ref, g_ref, sh_ref, sc_ref, w1_ref, g1_ref, wuq_ref, gq_ref,
                    wukv_ref, gk_ref, cos_ref, sa_ref, sb_ref,
                    q_ref, k_ref, vt_ref):
    h = _modulate(x_ref[0], g_ref[...], sh_ref[0], sc_ref[0]).astype(BF16)
    y1 = _dot(h, w1_ref[...])
    g1 = g1_ref[...]
    cq = (_rms(y1[:, :MLA_Q_RANK], MLA_Q_RANK) * g1[:, :MLA_Q_RANK]).astype(BF16)
    lo, hi = MLA_Q_RANK, MLA_Q_RANK + MLA_KV_RANK
    ckv = (_rms(y1[:, lo:hi], MLA_KV_RANK) * g1[:, lo:hi]).astype(BF16)
    cos, sa, sb = cos_ref[...], sa_ref[...], sb_ref[...]
    quarter = MLA_ROPE // 4
    kpe = _rms(y1[:, hi:], MLA_ROPE) * g1[:, hi:]
    kpe = _rope(kpe, cos, sa, sb, quarter).astype(BF16)
    gq = gq_ref[...]
    gk = gk_ref[...]
    qscale = MLA_SCALE * LOG2E

    def head_dots(hh):
        c0 = hh * MLA_QK_PAD
        return (_dot(cq, wuq_ref[:, c0:c0 + MLA_QK_PAD]),
                _dot(ckv, wukv_ref[:, c0:c0 + MLA_NOPE + MLA_V]))

    nxt = head_dots(0)
    for hh in range(MLA_HEADS):
        c0 = hh * MLA_QK_PAD
        qh, kvh = nxt
        if hh + 1 < MLA_HEADS:
            nxt = head_dots(hh + 1)
        qn = _rms(qh[:, :MLA_NOPE], MLA_NOPE) * gq[:, :MLA_NOPE]
        qp = _rms(qh[:, MLA_NOPE:], MLA_ROPE) * gq[:, MLA_NOPE:]
        qp = _rope(qp, cos, sa, sb, quarter)
        q_ref[0, :, c0:c0 + MLA_NOPE] = (qn * qscale).astype(BF16)
        q_ref[0, :, c0 + MLA_NOPE:c0 + MLA_QK_PAD] = (qp * qscale).astype(BF16)
        kn = _rms(kvh[:, :MLA_NOPE], MLA_NOPE) * gk
        k_ref[0, :, c0:c0 + MLA_NOPE] = kn.astype(BF16)
        k_ref[0, :, c0 + MLA_NOPE:c0 + MLA_QK_PAD] = kpe
        vt_ref[0, hh] = kvh[:, MLA_NOPE:].T.astype(BF16)


def _mla_qkv(x, g, shift, scale, w, rope, tm, rope_blk0):
    B, R, D = x.shape
    H = MLA_HEADS
    nq = H * MLA_QK_PAD
    const2 = lambda b, i: (0, 0)
    mod_map = (lambda b, i: (b, 0, 0)) if shift.shape[0] == B else (lambda b, i: (0, 0, 0))
    rope_spec = pl.BlockSpec((tm, LANES), lambda b, i: (rope_blk0 + i, 0))
    return pl.pallas_call(
        _mla_qkv_kernel,
        grid=(B, R // tm),
        in_specs=[
            pl.BlockSpec((1, tm, D), lambda b, i: (b, i, 0)),
            pl.BlockSpec((1, D), const2),
            pl.BlockSpec((1, 1, D), mod_map),
            pl.BlockSpec((1, 1, D), mod_map),
            pl.BlockSpec(w["w1"].shape, const2),
            pl.BlockSpec(w["g1"].shape, const2),
            pl.BlockSpec(w["wuq"].shape, const2),
            pl.BlockSpec(w["gq"].shape, const2),
            pl.BlockSpec(w["wukv"].shape, const2),
            pl.BlockSpec(w["gk"].shape, const2),
            rope_spec, rope_spec, rope_spec,
        ],
        out_specs=[
            pl.BlockSpec((1, tm, nq), lambda b, i: (b, i, 0)),
            pl.BlockSpec((1, tm, nq), lambda b, i: (b, i, 0)),
            pl.BlockSpec((1, H, MLA_V, tm), lambda b, i: (b, 0, 0, i)),
        ],
        out_shape=[
            jax.ShapeDtypeStruct((B, R, nq), BF16),
            jax.ShapeDtypeStruct((B, R, nq), BF16),
            jax.ShapeDtypeStruct((B, H, MLA_V, R), BF16),
        ],
        compiler_params=_cparams(("parallel", "parallel")),
        name="mla_qkv",
    )(x, g, shift, scale, w["w1"], w["g1"], w["wuq"], w["gq"], w["wukv"], w["gk"], *rope)


def _gqa_qkv_kernel(x_ref, g_ref, sh_ref, sc_ref, w_ref, gq_ref, gk_ref,
                    cos_ref, sa_ref, sb_ref, q_ref, k_ref, vt_ref):
    h = _modulate(x_ref[0], g_ref[...], sh_ref[0], sc_ref[0]).astype(BF16)
    cos, sa, sb = cos_ref[...], sa_ref[...], sb_ref[...]
    quarter = GQA_HEAD_DIM // 4
    hd = GQA_HEAD_DIM
    group = GQA_KV_HEADS * hd
    nq = GQA_HEADS * hd
    gq = gq_ref[...] * (GQA_SCALE * LOG2E)
    gk = gk_ref[...]
    n_groups = (nq + 2 * group) // group
    y_next = _dot(h, w_ref[:, 0:group])
    for gi in range(n_groups):
        c0 = gi * group
        y = y_next
        if gi + 1 < n_groups:
            y_next = _dot(h, w_ref[:, c0 + group:c0 + 2 * group])
        for j in range(0, group, hd):
            if c0 < nq:
                qh = _rope(_rms(y[:, j:j + hd], hd) * gq, cos, sa, sb, quarter)
                q_ref[0, :, c0 + j:c0 + j + hd] = qh.astype(BF16)
            elif c0 == nq:
                kh = _rope(_rms(y[:, j:j + hd], hd) * gk, cos, sa, sb, quarter)
                k_ref[0, :, j:j + hd] = kh.astype(BF16)
            else:
                vt_ref[0, j // hd] = y[:, j:j + hd].T.astype(BF16)


def _gqa_qkv(x, g, shift, scale, w, rope, tm, rope_blk0):
    B, R, D = x.shape
    nq = GQA_HEADS * GQA_HEAD_DIM
    nk = GQA_KV_HEADS * GQA_HEAD_DIM
    const2 = lambda b, i: (0, 0)
    mod_map = (lambda b, i: (b, 0, 0)) if shift.shape[0] == B else (lambda b, i: (0, 0, 0))
    rope_spec = pl.BlockSpec((tm, LANES), lambda b, i: (rope_blk0 + i, 0))
    return pl.pallas_call(
        _gqa_qkv_kernel,
        grid=(B, R // tm),
        in_specs=[
            pl.BlockSpec((1, tm, D), lambda b, i: (b, i, 0)),
            pl.BlockSpec((1, D), const2),
            pl.BlockSpec((1, 1, D), mod_map),
            pl.BlockSpec((1, 1, D), mod_map),
            pl.BlockSpec(w["w"].shape, const2),
            pl.BlockSpec(w["gq"].shape, const2),
            pl.BlockSpec(w["gk"].shape, const2),
            rope_spec, rope_spec, rope_spec,
        ],
        out_specs=[
            pl.BlockSpec((1, tm, nq), lambda b, i: (b, i, 0)),
            pl.BlockSpec((1, tm, nk), lambda b, i: (b, i, 0)),
            pl.BlockSpec((1, GQA_KV_HEADS, GQA_HEAD_DIM, tm), lambda b, i: (b, 0, 0, i)),
        ],
        out_shape=[
            jax.ShapeDtypeStruct((B, R, nq), BF16),
            jax.ShapeDtypeStruct((B, R, nk), BF16),
            jax.ShapeDtypeStruct((B, GQA_KV_HEADS, GQA_HEAD_DIM, R), BF16),
        ],
        compiler_params=_cparams(("parallel", "parallel")),
        name="gqa_qkv",
    )(x, g, shift, scale, w["w"], w["gq"], w["gk"], *rope)


def _attn_kernel(bound_ref, *refs, chunks, n_src, group):
    q_ref = refs[0]
    kv_refs = refs[1:1 + 2 * n_src]
    o_ref = refs[1 + 2 * n_src]
    s_buf, p_buf, acc_ref = refs[2 + 2 * n_src:]
    n = len(chunks)
    bound = bound_ref[pl.program_id(0), pl.program_id(1) * group + pl.program_id(2)]

    def qk(c):
        src, start, size = chunks[c]
        k_c = kv_refs[2 * src][0, start:start + size, :]
        return lax.dot_general(k_c, q_ref[0], (((1,), (1,)), ((), ())),
                               preferred_element_type=F32)

    def vt(c):
        src, start, size = chunks[c]
        return kv_refs[2 * src + 1][0, 0, :, start:start + size]

    def bounded():
        acc = l = p_prev = None
        s_next = qk(0)
        for c in range(n):
            s = s_next
            if c + 1 < n:
                s_next = qk(c + 1)
            p = jnp.exp2(s - bound)
            psum = jnp.sum(p, axis=0, keepdims=True)
            l = psum if c == 0 else l + psum
            if c >= 1:
                y = _dot(vt(c - 1), p_prev)
                acc = y if c == 1 else acc + y
            p_prev = p.astype(BF16)
        y = _dot(vt(n - 1), p_prev)
        acc = y if n == 1 else acc + y
        o_ref[0] = (acc / l).T.astype(o_ref.dtype)

    def online():
        def qk_store(c):
            s = qk(c)
            s_buf[c % 2, 0:chunks[c][2], :] = s
            return jnp.max(s, axis=0, keepdims=True)

        def pv(c, alpha):
            y = _dot(vt(c), p_buf[c % 2, 0:chunks[c][2], :])
            if c == 0:
                acc_ref[...] = y
            else:
                acc_ref[...] = alpha * acc_ref[...] + y

        m_next = qk_store(0)
        m = l = alpha_prev = None
        for c in range(n):
            size = chunks[c][2]
            m_c = m_next
            if c + 1 < n:
                m_next = qk_store(c + 1)
            s = s_buf[c % 2, 0:size, :]
            if c == 0:
                m_new, alpha = m_c, None
            else:
                m_new = jnp.maximum(m, m_c)
                alpha = jnp.exp2(m - m_new)
            p = jnp.exp2(s - m_new)
            p_buf[c % 2, 0:size, :] = p.astype(BF16)
            psum = jnp.sum(p, axis=0, keepdims=True)
            l = psum if c == 0 else alpha * l + psum
            m = m_new
            if c >= 1:
                pv(c - 1, alpha_prev)
            alpha_prev = alpha
        pv(n - 1, alpha_prev)
        o_ref[0] = (acc_ref[...] / l).T.astype(o_ref.dtype)

    lax.cond(bound < SAFE_LOG2_SPAN, bounded, online)


def _attention(q, kv_sources, bound, n_heads, n_kv_heads, dq, dv, tq):
    B, Q, _ = q.shape
    G = n_heads // n_kv_heads
    chunks = []
    in_specs = [pl.BlockSpec((1, tq, dq), lambda b, hk, g, i, bnd: (b, i, hk * G + g))]
    args = [q]
    for src, (k, vt) in enumerate(kv_sources):
        T = k.shape[1]
        ck = min(ATTN_K_CHUNK, T)
        chunks += [(src, s, ck) for s in range(0, T, ck)]
        in_specs += [
            pl.BlockSpec((1, T, dq), lambda b, hk, g, i, bnd: (b, 0, hk)),
            pl.BlockSpec((1, 1, dv, T), lambda b, hk, g, i, bnd: (b, hk, 0, 0)),
        ]
        args += [k, vt]
    ck_max = max(size for _, _, size in chunks)
    return pl.pallas_call(
        functools.partial(_attn_kernel, chunks=tuple(chunks), n_src=len(kv_sources), group=G),
        grid_spec=pltpu.PrefetchScalarGridSpec(
            num_scalar_prefetch=1,
            grid=(B, n_kv_heads, G, Q // tq),
            in_specs=in_specs,
            out_specs=pl.BlockSpec((1, tq, dv), lambda b, hk, g, i, bnd: (b, i, hk * G + g)),
            scratch_shapes=[
                pltpu.VMEM((2, ck_max, tq), F32),
                pltpu.VMEM((2, ck_max, tq), BF16),
                pltpu.VMEM((dv, tq), F32),
            ],
        ),
        out_shape=jax.ShapeDtypeStruct((B, Q, n_heads * dv), BF16),
        compiler_params=_cparams(("parallel", "parallel", "parallel", "parallel")),
        name="attention",
    )(bound, *args)


def _score_bound(parts_q, parts_k, qscale, batch, n_heads):
    q2 = sum(n * jnp.max(g * g) for n, g in parts_q) * (qscale * qscale)
    k2 = sum(n * jnp.max(g * g) for n, g in parts_k)
    return jnp.full((batch, n_heads), jnp.sqrt(q2 * k2) * BOUND_MARGIN, F32)


def _oproj_kernel(o_ref, w_ref, x_ref, gate_ref, out_ref, *, col_tile):
    o = o_ref[0]
    gate = gate_ref[0]
    for c0 in range(0, out_ref.shape[-1], col_tile):
        y = _dot(o, w_ref[:, c0:c0 + col_tile])
        out_ref[0, :, c0:c0 + col_tile] = x_ref[0, :, c0:c0 + col_tile] + gate[:, c0:c0 + col_tile] * y


def _oproj(o, w_o, x, gate, tm):
    B, R, D = x.shape
    K = o.shape[-1]
    mod_map = (lambda b, i: (b, 0, 0)) if gate.shape[0] == B else (lambda b, i: (0, 0, 0))
    return pl.pallas_call(
        functools.partial(_oproj_kernel, col_tile=512),
        grid=(B, R // tm),
        in_specs=[
            pl.BlockSpec((1, tm, K), lambda b, i: (b, i, 0)),
            pl.BlockSpec((K, D), lambda b, i: (0, 0)),
            pl.BlockSpec((1, tm, D), lambda b, i: (b, i, 0)),
            pl.BlockSpec((1, 1, D), mod_map),
        ],
        out_specs=pl.BlockSpec((1, tm, D), lambda b, i: (b, i, 0)),
        out_shape=jax.ShapeDtypeStruct((B, R, D), F32),
        compiler_params=_cparams(("parallel", "parallel")),
        name="oproj",
    )(o, w_o, x, gate)


def _ffn_up_kernel(x_ref, g_ref, sh_ref, sc_ref, w_ref, u_ref, h_ref):
    @pl.when(pl.program_id(2) == 0)
    def _():
        h_ref[...] = _modulate(x_ref[0], g_ref[...], sh_ref[0], sc_ref[0]).astype(BF16)

    u_ref[0] = _dot(h_ref[...], w_ref[...]).astype(u_ref.dtype)


def _ffn_up(x, g, shift, scale, w_up, tm, tn):
    B, R, D = x.shape
    N = w_up.shape[1]
    mod_map = (lambda b, i, j: (b, 0, 0)) if shift.shape[0] == B else (lambda b, i, j: (0, 0, 0))
    return pl.pallas_call(
        _ffn_up_kernel,
        grid=(B, R // tm, N // tn),
        in_specs=[
            pl.BlockSpec((1, tm, D), lambda b, i, j: (b, i, 0)),
            pl.BlockSpec((1, D), lambda b, i, j: (0, 0)),
            pl.BlockSpec((1, 1, D), mod_map),
            pl.BlockSpec((1, 1, D), mod_map),
            pl.BlockSpec((D, tn), lambda b, i, j: (0, j)),
        ],
        out_specs=pl.BlockSpec((1, tm, tn), lambda b, i, j: (b, i, j)),
        out_shape=jax.ShapeDtypeStruct((B, R, N), BF16),
        scratch_shapes=[pltpu.VMEM((tm, D), BF16)],
        compiler_params=_cparams(("parallel", "parallel", "arbitrary")),
        name="ffn_up",
    )(x, g, shift, scale, w_up)


def _ffn_down_kernel(ug_ref, uv_ref, prev_ref, next_ref, cw_ref, cb_ref, w_ref, x_ref, gate_ref,
                     out_ref):
    i = pl.program_id(1)
    kk = pl.program_id(2)
    tm = ug_ref.shape[1]
    first = i == 0
    last = i == pl.num_programs(1) - 1
    prev_row = jnp.where(first, 0.0, prev_ref[0, HALO - 1:HALO, :].astype(F32))
    next_row = jnp.where(last, 0.0, next_ref[0, 0:1, :].astype(F32))
    g = ug_ref[0].astype(F32)
    row = lax.broadcasted_iota(jnp.int32, g.shape, 0)
    g_prev = jnp.where(row == 0, prev_row, pltpu.roll(g, 1, 0))
    g_next = jnp.where(row == tm - 1, next_row, pltpu.roll(g, tm - 1, 0))
    cw = cw_ref[...]
    conv = g_prev * cw[0:1, :] + g * cw[1:2, :] + g_next * cw[2:3, :] + cb_ref[...]
    half = 0.5 * conv
    a = (half * (1.0 + jnp.tanh(half))) * uv_ref[0].astype(F32)
    y = _dot(a.astype(BF16), w_ref[...])

    @pl.when(kk == 0)
    def _():
        out_ref[0] = y

    @pl.when(kk > 0)
    def _():
        out_ref[0] += y

    @pl.when(kk == pl.num_programs(2) - 1)
    def _():
        out_ref[0] = x_ref[0] + gate_ref[0] * out_ref[0]


def _ffn_down(u, conv_w, conv_b, w_down, x, gate, tm, tk):
    B, R, D = x.shape
    F = w_down.shape[0]
    nk = F // tk
    rb = tm // HALO
    n_halo_blocks = R // HALO
    mod_map = (lambda b, i, k: (b, 0, 0)) if gate.shape[0] == B else (lambda b, i, k: (0, 0, 0))
    return pl.pallas_call(
        _ffn_down_kernel,
        grid=(B, R // tm, nk),
        in_specs=[
            pl.BlockSpec((1, tm, tk), lambda b, i, k: (b, i, k)),
            pl.BlockSpec((1, tm, tk), lambda b, i, k: (b, i, nk + k)),
            pl.BlockSpec((1, HALO, tk), lambda b, i, k: (b, jnp.maximum(i * rb - 1, 0), k)),
            pl.BlockSpec((1, HALO, tk),
                         lambda b, i, k: (b, jnp.minimum((i + 1) * rb, n_halo_blocks - 1), k)),
            pl.BlockSpec((CONV_W, tk), lambda b, i, k: (0, k)),
            pl.BlockSpec((1, tk), lambda b, i, k: (0, k)),
            pl.BlockSpec((tk, D), lambda b, i, k: (k, 0)),
            pl.BlockSpec((1, tm, D), lambda b, i, k: (b, i, 0)),
            pl.BlockSpec((1, 1, D), mod_map),
        ],
        out_specs=pl.BlockSpec((1, tm, D), lambda b, i, k: (b, i, 0)),
        out_shape=jax.ShapeDtypeStruct((B, R, D), F32),
        compiler_params=_cparams(("parallel", "parallel", "arbitrary")),
        name="ffn_down",
    )(u, u, u, u, conv_w, conv_b.reshape(1, F), w_down, x, gate)


def _rope_tables(seq, ctx_len, rot_dim):
    t = jnp.arange(seq, dtype=jnp.int32)
    rows = (t // GRID_W).astype(F32)
    cols = (t % GRID_W).astype(F32)
    axis_dim = rot_dim // 2
    inv = jnp.power(ROPE_BASE, -jnp.arange(0, axis_dim, 2, dtype=F32) / axis_dim)
    ang_r = rows[:, None] * inv
    ang_c = cols[:, None] * inv
    ang = jnp.concatenate([ang_r, ang_r, ang_c, ang_c], axis=-1)
    cos, sin = jnp.cos(ang), jnp.sin(ang)
    lane = jnp.arange(rot_dim)
    first_quarter = (lane % axis_dim) < (axis_dim // 2)
    sin_a = jnp.where(first_quarter, -sin, 0.0)
    sin_b = jnp.where(first_quarter, 0.0, sin)

    def finish(tab, ctx_value):
        tab = jnp.concatenate([tab, jnp.full((ctx_len, rot_dim), ctx_value, F32)], axis=0)
        return jnp.pad(tab, ((0, 0), (0, LANES - rot_dim)))

    return finish(cos, 1.0), finish(sin_a, 0.0), finish(sin_b, 0.0)


def _mla_weights(j, w_dq, g_dq, w_uq, g_q_nope, g_q_pe, w_dkv, g_dkv, g_k_pe, w_ukv, g_k_nope):
    D = w_dq.shape[1]
    pad = LANES - MLA_ROPE
    w1 = jnp.concatenate([w_dq[j], w_dkv[j], jnp.zeros((D, pad), F32)], axis=1).astype(BF16)
    g1 = jnp.concatenate([g_dq[j], g_dkv[j], g_k_pe[j], jnp.zeros((pad,), F32)])[None, :]
    wuq = w_uq[j].reshape(MLA_Q_RANK, MLA_HEADS, MLA_NOPE + MLA_ROPE)
    wuq = jnp.pad(wuq, ((0, 0), (0, 0), (0, MLA_QK_PAD - MLA_NOPE - MLA_ROPE)))
    wuq = wuq.reshape(MLA_Q_RANK, MLA_HEADS * MLA_QK_PAD).astype(BF16)
    gq = jnp.concatenate([g_q_nope[j], g_q_pe[j], jnp.zeros((pad,), F32)])[None, :]
    return dict(w1=w1, g1=g1, wuq=wuq, gq=gq, wukv=w_ukv[j].astype(BF16), gk=g_k_nope[j][None, :])


def _gqa_weights(j, w_q, g_q, w_kv, g_k):
    w = jnp.concatenate([w_q[j], w_kv[j]], axis=1).astype(BF16)
    return dict(w=w, gq=g_q[j][None, :], gk=g_k[j][None, :])


def kernel(x, c, ctx, c_ctx, w_mod, b_mod, norm_mix, norm_ffn, mla_w_dq, mla_g_dq, mla_w_uq, mla_g_q_nope, mla_g_q_pe, mla_w_dkv, mla_g_dkv, mla_g_k_pe, mla_w_ukv, mla_g_k_nope, mla_w_o, gqa_w_q, gqa_g_q, gqa_w_kv, gqa_g_k, gqa_w_o, ffn_w_up, ffn_conv_w, ffn_conv_b, ffn_w_down):
    B, S, D = x.shape
    C = ctx.shape[1]
    depth = w_mod.shape[0]
    assert S % FFN_UP_ROW_TILE == 0 and S % ROW_TILE == 0 and C % HALO == 0 and C <= ROW_TILE

    cvec = jnp.concatenate([c, c_ctx[None, :], jnp.zeros((8 - B - 1, D), F32)], axis=0)
    mod = _mod_all(cvec, w_mod, b_mod).reshape(depth, 8, N_MOD, D)
    rope_mla = _rope_tables(S, C, MLA_ROPE)
    rope_gqa = _rope_tables(S, C, GQA_HEAD_DIM)

    for i in range(depth):
        last = i == depth - 1
        j = i // 2
        lat = [mod[i, :B, n][:, None, :] for n in range(N_MOD)]
        cmod = [mod[i, B:B + 1, n][:, None, :] for n in range(N_MOD)]
        g_mix = norm_mix[i][None, :]
        g_ffn = norm_ffn[i][None, :]

        if i % 2 == 0:
            w = _mla_weights(j, mla_w_dq, mla_g_dq, mla_w_uq, mla_g_q_nope, mla_g_q_pe,
                             mla_w_dkv, mla_g_dkv, mla_g_k_pe, mla_w_ukv, mla_g_k_nope)
            qkv, rope = _mla_qkv, rope_mla
            bound = _score_bound([(MLA_NOPE, mla_g_q_nope[j]), (MLA_ROPE, mla_g_q_pe[j])],
                                 [(MLA_NOPE, mla_g_k_nope[j]), (MLA_ROPE, mla_g_k_pe[j])],
                                 MLA_SCALE * LOG2E, B, MLA_HEADS)
            w_o = mla_w_o[j].astype(BF16)
            heads, kv_heads, dq, dv = MLA_HEADS, MLA_HEADS, MLA_QK_PAD, MLA_V
        else:
            w = _gqa_weights(j, gqa_w_q, gqa_g_q, gqa_w_kv, gqa_g_k)
            qkv, rope = _gqa_qkv, rope_gqa
            bound = _score_bound([(GQA_HEAD_DIM, gqa_g_q[j])], [(GQA_HEAD_DIM, gqa_g_k[j])],
                                 GQA_SCALE * LOG2E, B, GQA_HEADS)
            w_o = gqa_w_o[j].astype(BF16)
            heads, kv_heads, dq, dv = GQA_HEADS, GQA_KV_HEADS, GQA_HEAD_DIM, GQA_HEAD_DIM

        q, k, vt = qkv(x, g_mix, lat[0], lat[1], w, rope, ROW_TILE, 0)
        qc, kc, vtc = qkv(ctx, g_mix, cmod[0], cmod[1], w, rope, C, S // C)
        o = _attention(q, [(k, vt), (kc, vtc)], bound, heads, kv_heads, dq, dv, ATTN_Q_TILE)
        x = _oproj(o, w_o, x, lat[2], ROW_TILE)
        if not last:
            oc = _attention(qc, [(kc, vtc)], bound, heads, kv_heads, dq, dv, C)
            ctx = _oproj(oc, w_o, ctx, cmod[2], C)

        w_up = ffn_w_up[i].astype(BF16)
        w_down = ffn_w_down[i].astype(BF16)
        u = _ffn_up(x, g_ffn, lat[3], lat[4], w_up, FFN_UP_ROW_TILE, FFN_UP_COL_TILE)
        x = _ffn_down(u, ffn_conv_w[i], ffn_conv_b[i], w_down, x, lat[5],
                      FFN_DOWN_ROW_TILE, FFN_DOWN_K_TILE)
        if not last:
            uc = _ffn_up(ctx, g_ffn, cmod[3], cmod[4], w_up, C, FFN_UP_COL_TILE)
            ctx = _ffn_down(uc, ffn_conv_w[i], ffn_conv_b[i], w_down, ctx, cmod[5],
                            C, FFN_DOWN_K_TILE)
    return x
```

```python
import functools
import math

import jax
import jax.numpy as jnp
from jax import lax
from jax.experimental import pallas as pl
from jax.experimental.pallas import tpu as pltpu

F32 = jnp.float32
BF16 = jnp.bfloat16

GRID_W = 64
N_MOD = 6
MLA_HEADS = 16
MLA_Q_RANK = 512
MLA_KV_RANK = 512
MLA_NOPE = 128
MLA_ROPE = 64
MLA_V = 128
MLA_QK_PAD = 256
GQA_HEADS = 16
GQA_KV_HEADS = 4
GQA_HEAD_DIM = 128
CONV_W = 3
ROPE_BASE = 10000.0
EPS = 1e-6
LOG2E = 1.4426950408889634
MLA_SCALE = 1.0 / math.sqrt(MLA_NOPE + MLA_ROPE)
GQA_SCALE = 1.0 / math.sqrt(GQA_HEAD_DIM)

SAFE_LOG2_SPAN = 60.0
BOUND_MARGIN = 1.02

LANES = 128
BF16_SUBLANES = 16
VMEM_LIMIT_BYTES = 56 * 1024 * 1024

ROW_TILE = 512
FFN_UP_ROW_TILE = 1024
FFN_UP_COL_TILE = 1024
FFN_DOWN_ROW_TILE = 512
FFN_DOWN_K_TILE = 1408
ATTN_Q_TILE = 1024
ATTN_K_CHUNK = 2048
MOD_COL_TILE = 1024
HALO = BF16_SUBLANES


def _cparams(sem):
    return pltpu.CompilerParams(dimension_semantics=sem, vmem_limit_bytes=VMEM_LIMIT_BYTES)


def _dot(a, b):
    return jnp.dot(a, b, preferred_element_type=F32)


def _rms(y, n):
    return y * lax.rsqrt(jnp.sum(y * y, axis=-1, keepdims=True) * (1.0 / n) + EPS)


def _modulate(x, g, shift, scale):
    return (_rms(x, x.shape[-1]) * g) * (1.0 + scale) + shift


def _rope(x, cos, sin_a, sin_b, quarter):
    return (x * cos + pltpu.roll(x, LANES - quarter, 1) * sin_a
            + pltpu.roll(x, quarter, 1) * sin_b)


def _mod_kernel(c_ref, w_ref, b_ref, o_ref):
    c = c_ref[...]
    sc = (c * jax.nn.sigmoid(c)).astype(BF16)
    o_ref[0] = _dot(sc, w_ref[0].astype(BF16)) + b_ref[0]


def _mod_all(cvec, w_mod, b_mod):
    L, D, N = w_mod.shape
    R = cvec.shape[0]
    return pl.pallas_call(
        _mod_kernel,
        grid=(L, N // MOD_COL_TILE),
        in_specs=[
            pl.BlockSpec((R, D), lambda l, j: (0, 0)),
            pl.BlockSpec((1, D, MOD_COL_TILE), lambda l, j: (l, 0, j)),
            pl.BlockSpec((1, 1, MOD_COL_TILE), lambda l, j: (l, 0, j)),
        ],
        out_specs=pl.BlockSpec((1, R, MOD_COL_TILE), lambda l, j: (l, 0, j)),
        out_shape=jax.ShapeDtypeStruct((L, R, N), F32),
        compiler_params=_cparams(("parallel", "parallel")),
        name="mod_all",
    )(cvec, w_mod, b_mod.reshape(L, 1, N))


def _mla_qkv_kernel(x_ref, g_ref, sh_ref, sc_ref, w1_ref, g1_ref, wuq_ref, gq_ref,
                    wukv_ref, gk_ref, cos_ref, sa_ref, sb_ref,
                    q_ref, k_ref, vt_ref):
    h = _modulate(x_ref[0], g_ref[...], sh_ref[0], sc_ref[0]).astype(BF16)
    y1 = _dot(h, w1_ref[...])
    g1 = g1_ref[...]
    cq = (_rms(y1[:, :MLA_Q_RANK], MLA_Q_RANK) * g1[:, :MLA_Q_RANK]).astype(BF16)
    lo, hi = MLA_Q_RANK, MLA_Q_RANK + MLA_KV_RANK
    ckv = (_rms(y1[:, lo:hi], MLA_KV_RANK) * g1[:, lo:hi]).astype(BF16)
    cos, sa, sb = cos_ref[...], sa_ref[...], sb_ref[...]
    quarter = MLA_ROPE // 4
    kpe = _rms(y1[:, hi:], MLA_ROPE) * g1[:, hi:]
    kpe = _rope(kpe, cos, sa, sb, quarter).astype(BF16)
    gq = gq_ref[...]
    gk = gk_ref[...]
    qscale = MLA_SCALE * LOG2E

    def head_dots(hh):
        c0 = hh * MLA_QK_PAD
        return (_dot(cq, wuq_ref[:, c0:c0 + MLA_QK_PAD]),
                _dot(ckv, wukv_ref[:, c0:c0 + MLA_NOPE + MLA_V]))

    nxt = head_dots(0)
    for hh in range(MLA_HEADS):
        c0 = hh * MLA_QK_PAD
        qh, kvh = nxt
        if hh + 1 < MLA_HEADS:
            nxt = head_dots(hh + 1)
        qn = _rms(qh[:, :MLA_NOPE], MLA_NOPE) * gq[:, :MLA_NOPE]
        qp = _rms(qh[:, MLA_NOPE:], MLA_ROPE) * gq[:, MLA_NOPE:]
        qp = _rope(qp, cos, sa, sb, quarter)
        q_ref[0, :, c0:c0 + MLA_NOPE] = (qn * qscale).astype(BF16)
        q_ref[0, :, c0 + MLA_NOPE:c0 + MLA_QK_PAD] = (qp * qscale).astype(BF16)
        kn = _rms(kvh[:, :MLA_NOPE], MLA_NOPE) * gk
        k_ref[0, :, c0:c0 + MLA_NOPE] = kn.astype(BF16)
        k_ref[0, :, c0 + MLA_NOPE:c0 + MLA_QK_PAD] = kpe
        vt_ref[0, hh] = kvh[:, MLA_NOPE:].T.astype(BF16)


def _mla_qkv(x, g, shift, scale, w, rope, tm, rope_blk0):
    B, R, D = x.shape
    H = MLA_HEADS
    nq = H * MLA_QK_PAD
    const2 = lambda b, i: (0, 0)
    mod_map = (lambda b, i: (b, 0, 0)) if shift.shape[0] == B else (lambda b, i: (0, 0, 0))
    rope_spec = pl.BlockSpec((tm, LANES), lambda b, i: (rope_blk0 + i, 0))
    return pl.pallas_call(
        _mla_qkv_kernel,
        grid=(B, R // tm),
        in_specs=[
            pl.BlockSpec((1, tm, D), lambda b, i: (b, i, 0)),
            pl.BlockSpec((1, D), const2),
            pl.BlockSpec((1, 1, D), mod_map),
            pl.BlockSpec((1, 1, D), mod_map),
            pl.BlockSpec(w["w1"].shape, const2),
            pl.BlockSpec(w["g1"].shape, const2),
            pl.BlockSpec(w["wuq"].shape, const2),
            pl.BlockSpec(w["gq"].shape, const2),
            pl.BlockSpec(w["wukv"].shape, const2),
            pl.BlockSpec(w["gk"].shape, const2),
            rope_spec, rope_spec, rope_spec,
        ],
        out_specs=[
            pl.BlockSpec((1, tm, nq), lambda b, i: (b, i, 0)),
            pl.BlockSpec((1, tm, nq), lambda b, i: (b, i, 0)),
            pl.BlockSpec((1, H, MLA_V, tm), lambda b, i: (b, 0, 0, i)),
        ],
        out_shape=[
            jax.ShapeDtypeStruct((B, R, nq), BF16),
            jax.ShapeDtypeStruct((B, R, nq), BF16),
            jax.ShapeDtypeStruct((B, H, MLA_V, R), BF16),
        ],
        compiler_params=_cparams(("parallel", "parallel")),
        name="mla_qkv",
    )(x, g, shift, scale, w["w1"], w["g1"], w["wuq"], w["gq"], w["wukv"], w["gk"], *rope)


def _gqa_qkv_kernel(x_ref, g_ref, sh_ref, sc_ref, w_ref, gq_ref, gk_ref,
                    cos_ref, sa_ref, sb_ref, q_ref, k_ref, vt_ref):
    h = _modulate(x_ref[0], g_ref[...], sh_ref[0], sc_ref[0]).astype(BF16)
    cos, sa, sb = cos_ref[...], sa_ref[...], sb_ref[...]
    quarter = GQA_HEAD_DIM // 4
    hd = GQA_HEAD_DIM
    group = GQA_KV_HEADS * hd
    nq = GQA_HEADS * hd
    gq = gq_ref[...] * (GQA_SCALE * LOG2E)
    gk = gk_ref[...]
    n_groups = (nq + 2 * group) // group
    y_next = _dot(h, w_ref[:, 0:group])
    for gi in range(n_groups):
        c0 = gi * group
        y = y_next
        if gi + 1 < n_groups:
            y_next = _dot(h, w_ref[:, c0 + group:c0 + 2 * group])
        for j in range(0, group, hd):
            if c0 < nq:
                qh = _rope(_rms(y[:, j:j + hd], hd) * gq, cos, sa, sb, quarter)
                q_ref[0, :, c0 + j:c0 + j + hd] = qh.astype(BF16)
            elif c0 == nq:
                kh = _rope(_rms(y[:, j:j + hd], hd) * gk, cos, sa, sb, quarter)
                k_ref[0, :, j:j + hd] = kh.astype(BF16)
            else:
                vt_ref[0, j // hd] = y[:, j:j + hd].T.astype(BF16)


def _gqa_qkv(x, g, shift, scale, w, rope, tm, rope_blk0):
    B, R, D = x.shape
    nq = GQA_HEADS * GQA_HEAD_DIM
    nk = GQA_KV_HEADS * GQA_HEAD_DIM
    const2 = lambda b, i: (0, 0)
    mod_map = (lambda b, i: (b, 0, 0)) if shift.shape[0] == B else (lambda b, i: (0, 0, 0))
    rope_spec = pl.BlockSpec((tm, LANES), lambda b, i: (rope_blk0 + i, 0))
    return pl.pallas_call(
        _gqa_qkv_kernel,
        grid=(B, R // tm),
        in_specs=[
            pl.BlockSpec((1, tm, D), lambda b, i: (b, i, 0)),
            pl.BlockSpec((1, D), const2),
            pl.BlockSpec((1, 1, D), mod_map),
            pl.BlockSpec((1, 1, D), mod_map),
            pl.BlockSpec(w["w"].shape, const2),
            pl.BlockSpec(w["gq"].shape, const2),
            pl.BlockSpec(w["gk"].shape, const2),
            rope_spec, rope_spec, rope_spec,
        ],
        out_specs=[
            pl.BlockSpec((1, tm, nq), lambda b, i: (b, i, 0)),
            pl.BlockSpec((1, tm, nk), lambda b, i: (b, i, 0)),
            pl.BlockSpec((1, GQA_KV_HEADS, GQA_HEAD_DIM, tm), lambda b, i: (b, 0, 0, i)),
        ],
        out_shape=[
            jax.ShapeDtypeStruct((B, R, nq), BF16),
            jax.ShapeDtypeStruct((B, R, nk), BF16),
            jax.ShapeDtypeStruct((B, GQA_KV_HEADS, GQA_HEAD_DIM, R), BF16),
        ],
        compiler_params=_cparams(("parallel", "parallel")),
        name="gqa_qkv",
    )(x, g, shift, scale, w["w"], w["gq"], w["gk"], *rope)


def _attn_kernel(bound_ref, *refs, chunks, n_src, group):
    q_ref = refs[0]
    kv_refs = refs[1:1 + 2 * n_src]
    o_ref = refs[1 + 2 * n_src]
    s_buf, p_buf, acc_ref = refs[2 + 2 * n_src:]
    n = len(chunks)
    bound = bound_ref[pl.program_id(0), pl.program_id(1) * group + pl.program_id(2)]

    def qk(c):
        src, start, size = chunks[c]
        k_c = kv_refs[2 * src][0, start:start + size, :]
        return lax.dot_general(k_c, q_ref[0], (((1,), (1,)), ((), ())),
                               preferred_element_type=F32)

    def vt(c):
        src, start, size = chunks[c]
        return kv_refs[2 * src + 1][0, 0, :, start:start + size]

    def bounded():
        acc = l = p_prev = None
        s_next = qk(0)
        for c in range(n):
            s = s_next
            if c + 1 < n:
                s_next = qk(c + 1)
            p = jnp.exp2(s - bound)
            psum = jnp.sum(p, axis=0, keepdims=True)
            l = psum if c == 0 else l + psum
            if c >= 1:
                y = _dot(vt(c - 1), p_prev)
                acc = y if c == 1 else acc + y
            p_prev = p.astype(BF16)
        y = _dot(vt(n - 1), p_prev)
        acc = y if n == 1 else acc + y
        o_ref[0] = (acc / l).T.astype(o_ref.dtype)

    def online():
        def qk_store(c):
            s = qk(c)
            s_buf[c % 2, 0:chunks[c][2], :] = s
            return jnp.max(s, axis=0, keepdims=True)

        def pv(c, alpha):
            y = _dot(vt(c), p_buf[c % 2, 0:chunks[c][2], :])
            if c == 0:
                acc_ref[...] = y
            else:
                acc_ref[...] = alpha * acc_ref[...] + y

        m_next = qk_store(0)
        m = l = alpha_prev = None
        for c in range(n):
            size = chunks[c][2]
            m_c = m_next
            if c + 1 < n:
                m_next = qk_store(c + 1)
            s = s_buf[c % 2, 0:size, :]
            if c == 0:
                m_new, alpha = m_c, None
            else:
                m_new = jnp.maximum(m, m_c)
                alpha = jnp.exp2(m - m_new)
            p = jnp.exp2(s - m_new)
            p_buf[c % 2, 0:size, :] = p.astype(BF16)
            psum = jnp.sum(p, axis=0, keepdims=True)
            l = psum if c == 0 else alpha * l + psum
            m = m_new
            if c >= 1:
                pv(c - 1, alpha_prev)
            alpha_prev = alpha
        pv(n - 1, alpha_prev)
        o_ref[0] = (acc_ref[...] / l).T.astype(o_ref.dtype)

    lax.cond(bound < SAFE_LOG2_SPAN, bounded, online)


def _attention(q, kv_sources, bound, n_heads, n_kv_heads, dq, dv, tq):
    B, Q, _ = q.shape
    G = n_heads // n_kv_heads
    chunks = []
    in_specs = [pl.BlockSpec((1, tq, dq), lambda b, hk, g, i, bnd: (b, i, hk * G + g))]
    args = [q]
    for src, (k, vt) in enumerate(kv_sources):
        T = k.shape[1]
        ck = min(ATTN_K_CHUNK, T)
        chunks += [(src, s, ck) for s in range(0, T, ck)]
        in_specs += [
            pl.BlockSpec((1, T, dq), lambda b, hk, g, i, bnd: (b, 0, hk)),
            pl.BlockSpec((1, 1, dv, T), lambda b, hk, g, i, bnd: (b, hk, 0, 0)),
        ]
        args += [k, vt]
    ck_max = max(size for _, _, size in chunks)
    return pl.pallas_call(
        functools.partial(_attn_kernel, chunks=tuple(chunks), n_src=len(kv_sources), group=G),
        grid_spec=pltpu.PrefetchScalarGridSpec(
            num_scalar_prefetch=1,
            grid=(B, n_kv_heads, G, Q // tq),
            in_specs=in_specs,
            out_specs=pl.BlockSpec((1, tq, dv), lambda b, hk, g, i, bnd: (b, i, hk * G + g)),
            scratch_shapes=[
                pltpu.VMEM((2, ck_max, tq), F32),
                pltpu.VMEM((2, ck_max, tq), BF16),
                pltpu.VMEM((dv, tq), F32),
            ],
        ),
        out_shape=jax.ShapeDtypeStruct((B, Q, n_heads * dv), BF16),
        compiler_params=_cparams(("parallel", "parallel", "parallel", "parallel")),
        name="attention",
    )(bound, *args)


def _score_bound(parts_q, parts_k, qscale, batch, n_heads):
    q2 = sum(n * jnp.max(g * g) for n, g in parts_q) * (qscale * qscale)
    k2 = sum(n * jnp.max(g * g) for n, g in parts_k)
    return jnp.full((batch, n_heads), jnp.sqrt(q2 * k2) * BOUND_MARGIN, F32)


def _oproj_kernel(o_ref, w_ref, x_ref, gate_ref, out_ref, *, col_tile):
    o = o_ref[0]
    gate = gate_ref[0]
    for c0 in range(0, out_ref.shape[-1], col_tile):
        y = _dot(o, w_ref[:, c0:c0 + col_tile])
        out_ref[0, :, c0:c0 + col_tile] = x_ref[0, :, c0:c0 + col_tile] + gate[:, c0:c0 + col_tile] * y


def _oproj(o, w_o, x, gate, tm):
    B, R, D = x.shape
    K = o.shape[-1]
    mod_map = (lambda b, i: (b, 0, 0)) if gate.shape[0] == B else (lambda b, i: (0, 0, 0))
    return pl.pallas_call(
        functools.partial(_oproj_kernel, col_tile=512),
        grid=(B, R // tm),
        in_specs=[
            pl.BlockSpec((1, tm, K), lambda b, i: (b, i, 0)),
            pl.BlockSpec((K, D), lambda b, i: (0, 0)),
            pl.BlockSpec((1, tm, D), lambda b, i: (b, i, 0)),
            pl.BlockSpec((1, 1, D), mod_map),
        ],
        out_specs=pl.BlockSpec((1, tm, D), lambda b, i: (b, i, 0)),
        out_shape=jax.ShapeDtypeStruct((B, R, D), F32),
        compiler_params=_cparams(("parallel", "parallel")),
        name="oproj",
    )(o, w_o, x, gate)


def _ffn_up_kernel(x_ref, g_ref, sh_ref, sc_ref, w_ref, u_ref, h_ref):
    @pl.when(pl.program_id(2) == 0)
    def _():
        h_ref[...] = _modulate(x_ref[0], g_ref[...], sh_ref[0], sc_ref[0]).astype(BF16)

    u_ref[0] = _dot(h_ref[...], w_ref[...]).astype(u_ref.dtype)


def _ffn_up(x, g, shift, scale, w_up, tm, tn):
    B, R, D = x.shape
    N = w_up.shape[1]
    mod_map = (lambda b, i, j: (b, 0, 0)) if shift.shape[0] == B else (lambda b, i, j: (0, 0, 0))
    return pl.pallas_call(
        _ffn_up_kernel,
        grid=(B, R // tm, N // tn),
        in_specs=[
            pl.BlockSpec((1, tm, D), lambda b, i, j: (b, i, 0)),
            pl.BlockSpec((1, D), lambda b, i, j: (0, 0)),
            pl.BlockSpec((1, 1, D), mod_map),
            pl.BlockSpec((1, 1, D), mod_map),
            pl.BlockSpec((D, tn), lambda b, i, j: (0, j)),
        ],
        out_specs=pl.BlockSpec((1, tm, tn), lambda b, i, j: (b, i, j)),
        out_shape=jax.ShapeDtypeStruct((B, R, N), BF16),
        scratch_shapes=[pltpu.VMEM((tm, D), BF16)],
        compiler_params=_cparams(("parallel", "parallel", "arbitrary")),
        name="ffn_up",
    )(x, g, shift, scale, w_up)


def _ffn_down_kernel(ug_ref, uv_ref, prev_ref, next_ref, cw_ref, cb_ref, w_ref, x_ref, gate_ref,
                     out_ref):
    i = pl.program_id(1)
    kk = pl.program_id(2)
    tm = ug_ref.shape[1]
    first = i == 0
    last = i == pl.num_programs(1) - 1
    prev_row = jnp.where(first, 0.0, prev_ref[0, HALO - 1:HALO, :].astype(F32))
    next_row = jnp.where(last, 0.0, next_ref[0, 0:1, :].astype(F32))
    g = ug_ref[0].astype(F32)
    row = lax.broadcasted_iota(jnp.int32, g.shape, 0)
    g_prev = jnp.where(row == 0, prev_row, pltpu.roll(g, 1, 0))
    g_next = jnp.where(row == tm - 1, next_row, pltpu.roll(g, tm - 1, 0))
    cw = cw_ref[...]
    conv = g_prev * cw[0:1, :] + g * cw[1:2, :] + g_next * cw[2:3, :] + cb_ref[...]
    half = 0.5 * conv
    a = (half * (1.0 + jnp.tanh(half))) * uv_ref[0].astype(F32)
    y = _dot(a.astype(BF16), w_ref[...])

    @pl.when(kk == 0)
    def _():
        out_ref[0] = y

    @pl.when(kk > 0)
    def _():
        out_ref[0] += y

    @pl.when(kk == pl.num_programs(2) - 1)
    def _():
        out_ref[0] = x_ref[0] + gate_ref[0] * out_ref[0]


def _ffn_down(u, conv_w, conv_b, w_down, x, gate, tm, tk):
    B, R, D = x.shape
    F = w_down.shape[0]
    nk = F // tk
    rb = tm // HALO
    n_halo_blocks = R // HALO
    mod_map = (lambda b, i, k: (b, 0, 0)) if gate.shape[0] == B else (lambda b, i, k: (0, 0, 0))
    return pl.pallas_call(
        _ffn_down_kernel,
        grid=(B, R // tm, nk),
        in_specs=[
            pl.BlockSpec((1, tm, tk), lambda b, i, k: (b, i, k)),
            pl.BlockSpec((1, tm, tk), lambda b, i, k: (b, i, nk + k)),
            pl.BlockSpec((1, HALO, tk), lambda b, i, k: (b, jnp.maximum(i * rb - 1, 0), k)),
            pl.BlockSpec((1, HALO, tk),
                         lambda b, i, k: (b, jnp.minimum((i + 1) * rb, n_halo_blocks - 1), k)),
            pl.BlockSpec((CONV_W, tk), lambda b, i, k: (0, k)),
            pl.BlockSpec((1, tk), lambda b, i, k: (0, k)),
            pl.BlockSpec((tk, D), lambda b, i, k: (k, 0)),
            pl.BlockSpec((1, tm, D), lambda b, i, k: (b, i, 0)),
            pl.BlockSpec((1, 1, D), mod_map),
        ],
        out_specs=pl.BlockSpec((1, tm, D), lambda b, i, k: (b, i, 0)),
        out_shape=jax.ShapeDtypeStruct((B, R, D), F32),
        compiler_params=_cparams(("parallel", "parallel", "arbitrary")),
        name="ffn_down",
    )(u, u, u, u, conv_w, conv_b.reshape(1, F), w_down, x, gate)


def _rope_tables(seq, ctx_len, rot_dim):
    t = jnp.arange(seq, dtype=jnp.int32)
    rows = (t // GRID_W).astype(F32)
    cols = (t % GRID_W).astype(F32)
    axis_dim = rot_dim // 2
    inv = jnp.power(ROPE_BASE, -jnp.arange(0, axis_dim, 2, dtype=F32) / axis_dim)
    ang_r = rows[:, None] * inv
    ang_c = cols[:, None] * inv
    ang = jnp.concatenate([ang_r, ang_r, ang_c, ang_c], axis=-1)
    cos, sin = jnp.cos(ang), jnp.sin(ang)
    lane = jnp.arange(rot_dim)
    first_quarter = (lane % axis_dim) < (axis_dim // 2)
    sin_a = jnp.where(first_quarter, -sin, 0.0)
    sin_b = jnp.where(first_quarter, 0.0, sin)

    def finish(tab, ctx_value):
        tab = jnp.concatenate([tab, jnp.full((ctx_len, rot_dim), ctx_value, F32)], axis=0)
        return jnp.pad(tab, ((0, 0), (0, LANES - rot_dim)))

    return finish(cos, 1.0), finish(sin_a, 0.0), finish(sin_b, 0.0)


def _mla_weights(j, w_dq, g_dq, w_uq, g_q_nope, g_q_pe, w_dkv, g_dkv, g_k_pe, w_ukv, g_k_nope):
    D = w_dq.shape[1]
    pad = LANES - MLA_ROPE
    w1 = jnp.concatenate([w_dq[j], w_dkv[j], jnp.zeros((D, pad), F32)], axis=1).astype(BF16)
    g1 = jnp.concatenate([g_dq[j], g_dkv[j], g_k_pe[j], jnp.zeros((pad,), F32)])[None, :]
    wuq = w_uq[j].reshape(MLA_Q_RANK, MLA_HEADS, MLA_NOPE + MLA_ROPE)
    wuq = jnp.pad(wuq, ((0, 0), (0, 0), (0, MLA_QK_PAD - MLA_NOPE - MLA_ROPE)))
    wuq = wuq.reshape(MLA_Q_RANK, MLA_HEADS * MLA_QK_PAD).astype(BF16)
    gq = jnp.concatenate([g_q_nope[j], g_q_pe[j], jnp.zeros((pad,), F32)])[None, :]
    return dict(w1=w1, g1=g1, wuq=wuq, gq=gq, wukv=w_ukv[j].astype(BF16), gk=g_k_nope[j][None, :])


def _gqa_weights(j, w_q, g_q, w_kv, g_k):
    w = jnp.concatenate([w_q[j], w_kv[j]], axis=1).astype(BF16)
    return dict(w=w, gq=g_q[j][None, :], gk=g_k[j][None, :])


def kernel(x, c, ctx, c_ctx, w_mod, b_mod, norm_mix, norm_ffn, mla_w_dq, mla_g_dq, mla_w_uq, mla_g_q_nope, mla_g_q_pe, mla_w_dkv, mla_g_dkv, mla_g_k_pe, mla_w_ukv, mla_g_k_nope, mla_w_o, gqa_w_q, gqa_g_q, gqa_w_kv, gqa_g_k, gqa_w_o, ffn_w_up, ffn_conv_w, ffn_conv_b, ffn_w_down):
    B, S, D = x.shape
    C = ctx.shape[1]
    depth = w_mod.shape[0]
    assert S % FFN_UP_ROW_TILE == 0 and S % ROW_TILE == 0 and C % HALO == 0 and C <= ROW_TILE

    cvec = jnp.concatenate([c, c_ctx[None, :], jnp.zeros((8 - B - 1, D), F32)], axis=0)
    mod = _mod_all(cvec, w_mod, b_mod).reshape(depth, 8, N_MOD, D)
    rope_mla = _rope_tables(S, C, MLA_ROPE)
    rope_gqa = _rope_tables(S, C, GQA_HEAD_DIM)

    for i in range(depth):
        last = i == depth - 1
        j = i // 2
        lat = [mod[i, :B, n][:, None, :] for n in range(N_MOD)]
        cmod = [mod[i, B:B + 1, n][:, None, :] for n in range(N_MOD)]
        g_mix = norm_mix[i][None, :]
        g_ffn = norm_ffn[i][None, :]

        if i % 2 == 0:
            w = _mla_weights(j, mla_w_dq, mla_g_dq, mla_w_uq, mla_g_q_nope, mla_g_q_pe,
                             mla_w_dkv, mla_g_dkv, mla_g_k_pe, mla_w_ukv, mla_g_k_nope)
            qkv, rope = _mla_qkv, rope_mla
            bound = _score_bound([(MLA_NOPE, mla_g_q_nope[j]), (MLA_ROPE, mla_g_q_pe[j])],
                                 [(MLA_NOPE, mla_g_k_nope[j]), (MLA_ROPE, mla_g_k_pe[j])],
                                 MLA_SCALE * LOG2E, B, MLA_HEADS)
            w_o = mla_w_o[j].astype(BF16)
            heads, kv_heads, dq, dv = MLA_HEADS, MLA_HEADS, MLA_QK_PAD, MLA_V
        else:
            w = _gqa_weights(j, gqa_w_q, gqa_g_q, gqa_w_kv, gqa_g_k)
            qkv, rope = _gqa_qkv, rope_gqa
            bound = _score_bound([(GQA_HEAD_DIM, gqa_g_q[j])], [(GQA_HEAD_DIM, gqa_g_k[j])],
                                 GQA_SCALE * LOG2E, B, GQA_HEADS)
            w_o = gqa_w_o[j].astype(BF16)
            heads, kv_heads, dq, dv = GQA_HEADS, GQA_KV_HEADS, GQA_HEAD_DIM, GQA_HEAD_DIM

        q, k, vt = qkv(x, g_mix, lat[0], lat[1], w, rope, ROW_TILE, 0)
        qc, kc, vtc = qkv(ctx, g_mix, cmod[0], cmod[1], w, rope, C, S // C)
        o = _attention(q, [(k, vt), (kc, vtc)], bound, heads, kv_heads, dq, dv, ATTN_Q_TILE)
        x = _oproj(o, w_o, x, lat[2], ROW_TILE)
        if not last:
            oc = _attention(qc, [(kc, vtc)], bound, heads, kv_heads, dq, dv, C)
            ctx = _oproj(oc, w_o, ctx, cmod[2], C)

        w_up = ffn_w_up[i].astype(BF16)
        w_down = ffn_w_down[i].astype(BF16)
        u = _ffn_up(x, g_ffn, lat[3], lat[4], w_up, FFN_UP_ROW_TILE, FFN_UP_COL_TILE)
        x = _ffn_down(u, ffn_conv_w[i], ffn_conv_b[i], w_down, x, lat[5],
                      FFN_DOWN_ROW_TILE, FFN_DOWN_K_TILE)
        if not last:
            uc = _ffn_up(ctx, g_ffn, cmod[3], cmod[4], w_up, C, FFN_UP_COL_TILE)
            ctx = _ffn_down(uc, ffn_conv_w[i], ffn_conv_b[i], w_down, ctx, cmod[5],
                            C, FFN_DOWN_K_TILE)
    return x
```

```python
import functools
import math

import jax
import jax.numpy as jnp
from jax import lax
from jax.experimental import pallas as pl
from jax.experimental.pallas import tpu as pltpu

F32 = jnp.float32
BF16 = jnp.bfloat16

GRID_W = 64
N_MOD = 6
MLA_HEADS = 16
MLA_Q_RANK = 512
MLA_KV_RANK = 512
MLA_NOPE = 128
MLA_ROPE = 64
MLA_V = 128
MLA_QK_PAD = 256
GQA_HEADS = 16
GQA_KV_HEADS = 4
GQA_HEAD_DIM = 128
CONV_W = 3
ROPE_BASE = 10000.0
EPS = 1e-6
LOG2E = 1.4426950408889634
MLA_SCALE = 1.0 / math.sqrt(MLA_NOPE + MLA_ROPE)
GQA_SCALE = 1.0 / math.sqrt(GQA_HEAD_DIM)

SAFE_LOG2_SPAN = 60.0
BOUND_MARGIN = 1.02

LANES = 128
BF16_SUBLANES = 16
VMEM_LIMIT_BYTES = 56 * 1024 * 1024

ROW_TILE = 512
MLA_HEAD_GROUP = 4
FFN_UP_ROW_TILE = 1024
FFN_UP_COL_TILE = 1024
FFN_DOWN_ROW_TILE = 512
FFN_DOWN_K_TILE = 1408
ATTN_Q_TILE = 1024
ATTN_K_CHUNK = 2048
MOD_COL_TILE = 1024
HALO = BF16_SUBLANES


def _cparams(sem):
    return pltpu.CompilerParams(dimension_semantics=sem, vmem_limit_bytes=VMEM_LIMIT_BYTES)


def _dot(a, b):
    return jnp.dot(a, b, preferred_element_type=F32)


def _rms(y, n):
    return y * lax.rsqrt(jnp.sum(y * y, axis=-1, keepdims=True) * (1.0 / n) + EPS)


def _modulate(x, g, shift, scale):
    return (_rms(x, x.shape[-1]) * g) * (1.0 + scale) + shift


def _rope(x, cos, sin_a, sin_b, quarter):
    return (x * cos + pltpu.roll(x, LANES - quarter, 1) * sin_a
            + pltpu.roll(x, quarter, 1) * sin_b)


def _mod_kernel(c_ref, w_ref, b_ref, o_ref):
    c = c_ref[...]
    sc = (c * jax.nn.sigmoid(c)).astype(BF16)
    o_ref[0] = _dot(sc, w_ref[0].astype(BF16)) + b_ref[0]


def _mod_all(cvec, w_mod, b_mod):
    L, D, N = w_mod.shape
    R = cvec.shape[0]
    return pl.pallas_call(
        _mod_kernel,
        grid=(L, N // MOD_COL_TILE),
        in_specs=[
            pl.BlockSpec((R, D), lambda l, j: (0, 0)),
            pl.BlockSpec((1, D, MOD_COL_TILE), lambda l, j: (l, 0, j)),
            pl.BlockSpec((1, 1, MOD_COL_TILE), lambda l, j: (l, 0, j)),
        ],
        out_specs=pl.BlockSpec((1, R, MOD_COL_TILE), lambda l, j: (l, 0, j)),
        out_shape=jax.ShapeDtypeStruct((L, R, N), F32),
        compiler_params=_cparams(("parallel", "parallel")),
        name="mod_all",
    )(cvec, w_mod, b_mod.reshape(L, 1, N))


def _mla_qkv_kernel(x_ref, g_ref, sh_ref, sc_ref, w1_ref, g1_ref, wuq_ref, gq_ref,
                    wukv_ref, gk_ref, cos_ref, sa_ref, sb_ref,
                    q_ref, k_ref, vt_ref):
    h = _modulate(x_ref[0], g_ref[...], sh_ref[0], sc_ref[0]).astype(BF16)
    y1 = _dot(h, w1_ref[...])
    g1 = g1_ref[...]
    cq = (_rms(y1[:, :MLA_Q_RANK], MLA_Q_RANK) * g1[:, :MLA_Q_RANK]).astype(BF16)
    lo, hi = MLA_Q_RANK, MLA_Q_RANK + MLA_KV_RANK
    ckv = (_rms(y1[:, lo:hi], MLA_KV_RANK) * g1[:, lo:hi]).astype(BF16)
    cos, sa, sb = cos_ref[...], sa_ref[...], sb_ref[...]
    quarter = MLA_ROPE // 4
    kpe = _rms(y1[:, hi:], MLA_ROPE) * g1[:, hi:]
    kpe = _rope(kpe, cos, sa, sb, quarter).astype(BF16)
    gq = gq_ref[...]
    gk = gk_ref[...]
    qscale = MLA_SCALE * LOG2E

    hg = MLA_HEAD_GROUP
    gw = hg * MLA_QK_PAD

    def group_dots(gi):
        return (_dot(cq, wuq_ref[:, gi * gw:(gi + 1) * gw]),
                _dot(ckv, wukv_ref[:, gi * gw:(gi + 1) * gw]))

    nxt = group_dots(0)
    for gi in range(MLA_HEADS // hg):
        yq, ykv = nxt
        if gi + 1 < MLA_HEADS // hg:
            nxt = group_dots(gi + 1)
        offs = [j * MLA_QK_PAD for j in range(hg)]
        qn = [yq[:, o:o + MLA_NOPE] for o in offs]
        qp = [yq[:, o + MLA_NOPE:o + MLA_QK_PAD] for o in offs]
        kn = [ykv[:, o:o + MLA_NOPE] for o in offs]
        ssq = [jnp.sum(t * t, axis=-1, keepdims=True) for t in qn + qp + kn]
        widths = [MLA_NOPE] * hg + [MLA_ROPE] * hg + [MLA_NOPE] * hg
        inv = [lax.rsqrt(s * (1.0 / n) + EPS) for s, n in zip(ssq, widths)]
        qn = [t * r * (gq[:, :MLA_NOPE] * qscale) for t, r in zip(qn, inv[:hg])]
        qp = [t * r * (gq[:, MLA_NOPE:] * qscale) for t, r in zip(qp, inv[hg:2 * hg])]
        kn = [t * r * gk for t, r in zip(kn, inv[2 * hg:])]
        ra = [pltpu.roll(t, LANES - quarter, 1) for t in qp]
        rb = [pltpu.roll(t, quarter, 1) for t in qp]
        qp = [t * cos + a * sa + b * sb for t, a, b in zip(qp, ra, rb)]
        vts = [ykv[:, o + MLA_NOPE:o + MLA_QK_PAD].T for o in offs]
        for j, o in enumerate(offs):
            c0 = gi * gw + o
            q_ref[0, :, c0:c0 + MLA_NOPE] = qn[j].astype(BF16)
            q_ref[0, :, c0 + MLA_NOPE:c0 + MLA_QK_PAD] = qp[j].astype(BF16)
            k_ref[0, :, c0:c0 + MLA_NOPE] = kn[j].astype(BF16)
            k_ref[0, :, c0 + MLA_NOPE:c0 + MLA_QK_PAD] = kpe
            vt_ref[0, gi * hg + j] = vts[j].astype(BF16)


def _mla_qkv(x, g, shift, scale, w, rope, tm, rope_blk0):
    B, R, D = x.shape
    H = MLA_HEADS
    nq = H * MLA_QK_PAD
    const2 = lambda b, i: (0, 0)
    mod_map = (lambda b, i: (b, 0, 0)) if shift.shape[0] == B else (lambda b, i: (0, 0, 0))
    rope_spec = pl.BlockSpec((tm, LANES), lambda b, i: (rope_blk0 + i, 0))
    return pl.pallas_call(
        _mla_qkv_kernel,
        grid=(B, R // tm),
        in_specs=[
            pl.BlockSpec((1, tm, D), lambda b, i: (b, i, 0)),
            pl.BlockSpec((1, D), const2),
            pl.BlockSpec((1, 1, D), mod_map),
            pl.BlockSpec((1, 1, D), mod_map),
            pl.BlockSpec(w["w1"].shape, const2),
            pl.BlockSpec(w["g1"].shape, const2),
            pl.BlockSpec(w["wuq"].shape, const2),
            pl.BlockSpec(w["gq"].shape, const2),
            pl.BlockSpec(w["wukv"].shape, const2),
            pl.BlockSpec(w["gk"].shape, const2),
            rope_spec, rope_spec, rope_spec,
        ],
        out_specs=[
            pl.BlockSpec((1, tm, nq), lambda b, i: (b, i, 0)),
            pl.BlockSpec((1, tm, nq), lambda b, i: (b, i, 0)),
            pl.BlockSpec((1, H, MLA_V, tm), lambda b, i: (b, 0, 0, i)),
        ],
        out_shape=[
            jax.ShapeDtypeStruct((B, R, nq), BF16),
            jax.ShapeDtypeStruct((B, R, nq), BF16),
            jax.ShapeDtypeStruct((B, H, MLA_V, R), BF16),
        ],
        compiler_params=_cparams(("parallel", "parallel")),
        name="mla_qkv",
    )(x, g, shift, scale, w["w1"], w["g1"], w["wuq"], w["gq"], w["wukv"], w["gk"], *rope)


def _gqa_qkv_kernel(x_ref, g_ref, sh_ref, sc_ref, w_ref, gq_ref, gk_ref,
                    cos_ref, sa_ref, sb_ref, q_ref, k_ref, vt_ref):
    h = _modulate(x_ref[0], g_ref[...], sh_ref[0], sc_ref[0]).astype(BF16)
    cos, sa, sb = cos_ref[...], sa_ref[...], sb_ref[...]
    quarter = GQA_HEAD_DIM // 4
    hd = GQA_HEAD_DIM
    group = GQA_KV_HEADS * hd
    nq = GQA_HEADS * hd
    gq = gq_ref[...] * (GQA_SCALE * LOG2E)
    gk = gk_ref[...]
    n_groups = (nq + 2 * group) // group
    heads = range(0, group, hd)

    def norm_rope_group(y, gain):
        tiles = [y[:, j:j + hd] for j in heads]
        ssq = [jnp.sum(t * t, axis=-1, keepdims=True) for t in tiles]
        inv = [lax.rsqrt(s * (1.0 / hd) + EPS) for s in ssq]
        xs = [t * r * gain for t, r in zip(tiles, inv)]
        ra = [pltpu.roll(x, LANES - quarter, 1) for x in xs]
        rb = [pltpu.roll(x, quarter, 1) for x in xs]
        return [(x * cos + a * sa + b * sb).astype(BF16) for x, a, b in zip(xs, ra, rb)]

    y_next = _dot(h, w_ref[:, 0:group])
    for gi in range(n_groups):
        c0 = gi * group
        y = y_next
        if gi + 1 < n_groups:
            y_next = _dot(h, w_ref[:, c0 + group:c0 + 2 * group])
        if c0 < nq:
            for j, qh in zip(heads, norm_rope_group(y, gq)):
                q_ref[0, :, c0 + j:c0 + j + hd] = qh
        elif c0 == nq:
            for j, kh in zip(heads, norm_rope_group(y, gk)):
                k_ref[0, :, j:j + hd] = kh
        else:
            vts = [y[:, j:j + hd].T.astype(BF16) for j in heads]
            for j, v in zip(heads, vts):
                vt_ref[0, j // hd] = v


def _gqa_qkv(x, g, shift, scale, w, rope, tm, rope_blk0):
    B, R, D = x.shape
    nq = GQA_HEADS * GQA_HEAD_DIM
    nk = GQA_KV_HEADS * GQA_HEAD_DIM
    const2 = lambda b, i: (0, 0)
    mod_map = (lambda b, i: (b, 0, 0)) if shift.shape[0] == B else (lambda b, i: (0, 0, 0))
    rope_spec = pl.BlockSpec((tm, LANES), lambda b, i: (rope_blk0 + i, 0))
    return pl.pallas_call(
        _gqa_qkv_kernel,
        grid=(B, R // tm),
        in_specs=[
            pl.BlockSpec((1, tm, D), lambda b, i: (b, i, 0)),
            pl.BlockSpec((1, D), const2),
            pl.BlockSpec((1, 1, D), mod_map),
            pl.BlockSpec((1, 1, D), mod_map),
            pl.BlockSpec(w["w"].shape, const2),
            pl.BlockSpec(w["gq"].shape, const2),
            pl.BlockSpec(w["gk"].shape, const2),
            rope_spec, rope_spec, rope_spec,
        ],
        out_specs=[
            pl.BlockSpec((1, tm, nq), lambda b, i: (b, i, 0)),
            pl.BlockSpec((1, tm, nk), lambda b, i: (b, i, 0)),
            pl.BlockSpec((1, GQA_KV_HEADS, GQA_HEAD_DIM, tm), lambda b, i: (b, 0, 0, i)),
        ],
        out_shape=[
            jax.ShapeDtypeStruct((B, R, nq), BF16),
            jax.ShapeDtypeStruct((B, R, nk), BF16),
            jax.ShapeDtypeStruct((B, GQA_KV_HEADS, GQA_HEAD_DIM, R), BF16),
        ],
        compiler_params=_cparams(("parallel", "parallel")),
        name="gqa_qkv",
    )(x, g, shift, scale, w["w"], w["gq"], w["gk"], *rope)


def _attn_kernel(bound_ref, *refs, chunks, n_src, group):
    q_ref = refs[0]
    kv_refs = refs[1:1 + 2 * n_src]
    o_ref = refs[1 + 2 * n_src]
    s_buf, p_buf, acc_ref = refs[2 + 2 * n_src:]
    n = len(chunks)
    bound = bound_ref[pl.program_id(0), pl.program_id(1) * group + pl.program_id(2)]

    def qk(c):
        src, start, size = chunks[c]
        k_c = kv_refs[2 * src][0, start:start + size, :]
        return lax.dot_general(k_c, q_ref[0], (((1,), (1,)), ((), ())),
                               preferred_element_type=F32)

    def vt(c):
        src, start, size = chunks[c]
        return kv_refs[2 * src + 1][0, 0, :, start:start + size]

    def bounded():
        acc = l = p_prev = None
        s_next = qk(0)
        for c in range(n):
            s = s_next
            if c + 1 < n:
                s_next = qk(c + 1)
            p = jnp.exp2(s - bound)
            psum = jnp.sum(p, axis=0, keepdims=True)
            l = psum if c == 0 else l + psum
            if c >= 1:
                y = _dot(vt(c - 1), p_prev)
                acc = y if c == 1 else acc + y
            p_prev = p.astype(BF16)
        y = _dot(vt(n - 1), p_prev)
        acc = y if n == 1 else acc + y
        o_ref[0] = (acc / l).T.astype(o_ref.dtype)

    def online():
        def qk_store(c):
            s = qk(c)
            s_buf[c % 2, 0:chunks[c][2], :] = s
            return jnp.max(s, axis=0, keepdims=True)

        def pv(c, alpha):
            y = _dot(vt(c), p_buf[c % 2, 0:chunks[c][2], :])
            if c == 0:
                acc_ref[...] = y
            else:
                acc_ref[...] = alpha * acc_ref[...] + y

        m_next = qk_store(0)
        m = l = alpha_prev = None
        for c in range(n):
            size = chunks[c][2]
            m_c = m_next
            if c + 1 < n:
                m_next = qk_store(c + 1)
            s = s_buf[c % 2, 0:size, :]
            if c == 0:
                m_new, alpha = m_c, None
            else:
                m_new = jnp.maximum(m, m_c)
                alpha = jnp.exp2(m - m_new)
            p = jnp.exp2(s - m_new)
            p_buf[c % 2, 0:size, :] = p.astype(BF16)
            psum = jnp.sum(p, axis=0, keepdims=True)
            l = psum if c == 0 else alpha * l + psum
            m = m_new
            if c >= 1:
                pv(c - 1, alpha_prev)
            alpha_prev = alpha
        pv(n - 1, alpha_prev)
        o_ref[0] = (acc_ref[...] / l).T.astype(o_ref.dtype)

    lax.cond(bound < SAFE_LOG2_SPAN, bounded, online)


def _attention(q, kv_sources, bound, n_heads, n_kv_heads, dq, dv, tq):
    B, Q, _ = q.shape
    G = n_heads // n_kv_heads
    chunks = []
    in_specs = [pl.BlockSpec((1, tq, dq), lambda b, hk, g, i, bnd: (b, i, hk * G + g))]
    args = [q]
    for src, (k, vt) in enumerate(kv_sources):
        T = k.shape[1]
        ck = min(ATTN_K_CHUNK, T)
        chunks += [(src, s, ck) for s in range(0, T, ck)]
        in_specs += [
            pl.BlockSpec((1, T, dq), lambda b, hk, g, i, bnd: (b, 0, hk)),
            pl.BlockSpec((1, 1, dv, T), lambda b, hk, g, i, bnd: (b, hk, 0, 0)),
        ]
        args += [k, vt]
    ck_max = max(size for _, _, size in chunks)
    return pl.pallas_call(
        functools.partial(_attn_kernel, chunks=tuple(chunks), n_src=len(kv_sources), group=G),
        grid_spec=pltpu.PrefetchScalarGridSpec(
            num_scalar_prefetch=1,
            grid=(B, n_kv_heads, G, Q // tq),
            in_specs=in_specs,
            out_specs=pl.BlockSpec((1, tq, dv), lambda b, hk, g, i, bnd: (b, i, hk * G + g)),
            scratch_shapes=[
                pltpu.VMEM((2, ck_max, tq), F32),
                pltpu.VMEM((2, ck_max, tq), BF16),
                pltpu.VMEM((dv, tq), F32),
            ],
        ),
        out_shape=jax.ShapeDtypeStruct((B, Q, n_heads * dv), BF16),
        compiler_params=_cparams(("parallel", "parallel", "parallel", "parallel")),
        name="attention",
    )(bound, *args)


def _score_bound(parts_q, parts_k, qscale, batch, n_heads):
    q2 = sum(n * jnp.max(g * g) for n, g in parts_q) * (qscale * qscale)
    k2 = sum(n * jnp.max(g * g) for n, g in parts_k)
    return jnp.full((batch, n_heads), jnp.sqrt(q2 * k2) * BOUND_MARGIN, F32)


def _oproj_kernel(o_ref, w_ref, x_ref, gate_ref, out_ref, *, col_tile):
    o = o_ref[0]
    gate = gate_ref[0]
    for c0 in range(0, out_ref.shape[-1], col_tile):
        y = _dot(o, w_ref[:, c0:c0 + col_tile])
        out_ref[0, :, c0:c0 + col_tile] = x_ref[0, :, c0:c0 + col_tile] + gate[:, c0:c0 + col_tile] * y


def _oproj(o, w_o, x, gate, tm):
    B, R, D = x.shape
    K = o.shape[-1]
    mod_map = (lambda b, i: (b, 0, 0)) if gate.shape[0] == B else (lambda b, i: (0, 0, 0))
    return pl.pallas_call(
        functools.partial(_oproj_kernel, col_tile=512),
        grid=(B, R // tm),
        in_specs=[
            pl.BlockSpec((1, tm, K), lambda b, i: (b, i, 0)),
            pl.BlockSpec((K, D), lambda b, i: (0, 0)),
            pl.BlockSpec((1, tm, D), lambda b, i: (b, i, 0)),
            pl.BlockSpec((1, 1, D), mod_map),
        ],
        out_specs=pl.BlockSpec((1, tm, D), lambda b, i: (b, i, 0)),
        out_shape=jax.ShapeDtypeStruct((B, R, D), F32),
        compiler_params=_cparams(("parallel", "parallel")),
        name="oproj",
    )(o, w_o, x, gate)


def _ffn_up_kernel(x_ref, g_ref, sh_ref, sc_ref, w_ref, u_ref, h_ref):
    @pl.when(pl.program_id(2) == 0)
    def _():
        h_ref[...] = _modulate(x_ref[0], g_ref[...], sh_ref[0], sc_ref[0]).astype(BF16)

    u_ref[0] = _dot(h_ref[...], w_ref[...]).astype(u_ref.dtype)


def _ffn_up(x, g, shift, scale, w_up, tm, tn):
    B, R, D = x.shape
    N = w_up.shape[1]
    mod_map = (lambda b, i, j: (b, 0, 0)) if shift.shape[0] == B else (lambda b, i, j: (0, 0, 0))
    return pl.pallas_call(
        _ffn_up_kernel,
        grid=(B, R // tm, N // tn),
        in_specs=[
            pl.BlockSpec((1, tm, D), lambda b, i, j: (b, i, 0)),
            pl.BlockSpec((1, D), lambda b, i, j: (0, 0)),
            pl.BlockSpec((1, 1, D), mod_map),
            pl.BlockSpec((1, 1, D), mod_map),
            pl.BlockSpec((D, tn), lambda b, i, j: (0, j)),
        ],
        out_specs=pl.BlockSpec((1, tm, tn), lambda b, i, j: (b, i, j)),
        out_shape=jax.ShapeDtypeStruct((B, R, N), BF16),
        scratch_shapes=[pltpu.VMEM((tm, D), BF16)],
        compiler_params=_cparams(("parallel", "parallel", "arbitrary")),
        name="ffn_up",
    )(x, g, shift, scale, w_up)


def _ffn_down_kernel(ug_ref, uv_ref, prev_ref, next_ref, cw_ref, cb_ref, w_ref, x_ref, gate_ref,
                     out_ref):
    i = pl.program_id(1)
    kk = pl.program_id(2)
    tm = ug_ref.shape[1]
    first = i == 0
    last = i == pl.num_programs(1) - 1
    prev_row = jnp.where(first, 0.0, prev_ref[0, HALO - 1:HALO, :].astype(F32))
    next_row = jnp.where(last, 0.0, next_ref[0, 0:1, :].astype(F32))
    g = ug_ref[0].astype(F32)
    row = lax.broadcasted_iota(jnp.int32, g.shape, 0)
    g_prev = jnp.where(row == 0, prev_row, pltpu.roll(g, 1, 0))
    g_next = jnp.where(row == tm - 1, next_row, pltpu.roll(g, tm - 1, 0))
    cw = cw_ref[...]
    conv = g_prev * cw[0:1, :] + g * cw[1:2, :] + g_next * cw[2:3, :] + cb_ref[...]
    half = 0.5 * conv
    a = (half * (1.0 + jnp.tanh(half))) * uv_ref[0].astype(F32)
    y = _dot(a.astype(BF16), w_ref[...])

    @pl.when(kk == 0)
    def _():
        out_ref[0] = y

    @pl.when(kk > 0)
    def _():
        out_ref[0] += y

    @pl.when(kk == pl.num_programs(2) - 1)
    def _():
        out_ref[0] = x_ref[0] + gate_ref[0] * out_ref[0]


def _ffn_down(u, conv_w, conv_b, w_down, x, gate, tm, tk):
    B, R, D = x.shape
    F = w_down.shape[0]
    nk = F // tk
    rb = tm // HALO
    n_halo_blocks = R // HALO
    mod_map = (lambda b, i, k: (b, 0, 0)) if gate.shape[0] == B else (lambda b, i, k: (0, 0, 0))
    return pl.pallas_call(
        _ffn_down_kernel,
        grid=(B, R // tm, nk),
        in_specs=[
            pl.BlockSpec((1, tm, tk), lambda b, i, k: (b, i, k)),
            pl.BlockSpec((1, tm, tk), lambda b, i, k: (b, i, nk + k)),
            pl.BlockSpec((1, HALO, tk), lambda b, i, k: (b, jnp.maximum(i * rb - 1, 0), k)),
            pl.BlockSpec((1, HALO, tk),
                         lambda b, i, k: (b, jnp.minimum((i + 1) * rb, n_halo_blocks - 1), k)),
            pl.BlockSpec((CONV_W, tk), lambda b, i, k: (0, k)),
            pl.BlockSpec((1, tk), lambda b, i, k: (0, k)),
            pl.BlockSpec((tk, D), lambda b, i, k: (k, 0)),
            pl.BlockSpec((1, tm, D), lambda b, i, k: (b, i, 0)),
            pl.BlockSpec((1, 1, D), mod_map),
        ],
        out_specs=pl.BlockSpec((1, tm, D), lambda b, i, k: (b, i, 0)),
        out_shape=jax.ShapeDtypeStruct((B, R, D), F32),
        compiler_params=_cparams(("parallel", "parallel", "arbitrary")),
        name="ffn_down",
    )(u, u, u, u, conv_w, conv_b.reshape(1, F), w_down, x, gate)


def _rope_tables(seq, ctx_len, rot_dim):
    t = jnp.arange(seq, dtype=jnp.int32)
    rows = (t // GRID_W).astype(F32)
    cols = (t % GRID_W).astype(F32)
    axis_dim = rot_dim // 2
    inv = jnp.power(ROPE_BASE, -jnp.arange(0, axis_dim, 2, dtype=F32) / axis_dim)
    ang_r = rows[:, None] * inv
    ang_c = cols[:, None] * inv
    ang = jnp.concatenate([ang_r, ang_r, ang_c, ang_c], axis=-1)
    cos, sin = jnp.cos(ang), jnp.sin(ang)
    lane = jnp.arange(rot_dim)
    first_quarter = (lane % axis_dim) < (axis_dim // 2)
    sin_a = jnp.where(first_quarter, -sin, 0.0)
    sin_b = jnp.where(first_quarter, 0.0, sin)

    def finish(tab, ctx_value):
        tab = jnp.concatenate([tab, jnp.full((ctx_len, rot_dim), ctx_value, F32)], axis=0)
        return jnp.pad(tab, ((0, 0), (0, LANES - rot_dim)))

    return finish(cos, 1.0), finish(sin_a, 0.0), finish(sin_b, 0.0)


def _mla_weights(j, w_dq, g_dq, w_uq, g_q_nope, g_q_pe, w_dkv, g_dkv, g_k_pe, w_ukv, g_k_nope):
    D = w_dq.shape[1]
    pad = LANES - MLA_ROPE
    w1 = jnp.concatenate([w_dq[j], w_dkv[j], jnp.zeros((D, pad), F32)], axis=1).astype(BF16)
    g1 = jnp.concatenate([g_dq[j], g_dkv[j], g_k_pe[j], jnp.zeros((pad,), F32)])[None, :]
    wuq = w_uq[j].reshape(MLA_Q_RANK, MLA_HEADS, MLA_NOPE + MLA_ROPE)
    wuq = jnp.pad(wuq, ((0, 0), (0, 0), (0, MLA_QK_PAD - MLA_NOPE - MLA_ROPE)))
    wuq = wuq.reshape(MLA_Q_RANK, MLA_HEADS * MLA_QK_PAD).astype(BF16)
    gq = jnp.concatenate([g_q_nope[j], g_q_pe[j], jnp.zeros((pad,), F32)])[None, :]
    return dict(w1=w1, g1=g1, wuq=wuq, gq=gq, wukv=w_ukv[j].astype(BF16), gk=g_k_nope[j][None, :])


def _gqa_weights(j, w_q, g_q, w_kv, g_k):
    w = jnp.concatenate([w_q[j], w_kv[j]], axis=1).astype(BF16)
    return dict(w=w, gq=g_q[j][None, :], gk=g_k[j][None, :])


def kernel(x, c, ctx, c_ctx, w_mod, b_mod, norm_mix, norm_ffn, mla_w_dq, mla_g_dq, mla_w_uq, mla_g_q_nope, mla_g_q_pe, mla_w_dkv, mla_g_dkv, mla_g_k_pe, mla_w_ukv, mla_g_k_nope, mla_w_o, gqa_w_q, gqa_g_q, gqa_w_kv, gqa_g_k, gqa_w_o, ffn_w_up, ffn_conv_w, ffn_conv_b, ffn_w_down):
    B, S, D = x.shape
    C = ctx.shape[1]
    depth = w_mod.shape[0]
    assert S % FFN_UP_ROW_TILE == 0 and S % ROW_TILE == 0 and C % HALO == 0 and C <= ROW_TILE

    cvec = jnp.concatenate([c, c_ctx[None, :], jnp.zeros((8 - B - 1, D), F32)], axis=0)
    mod = _mod_all(cvec, w_mod, b_mod).reshape(depth, 8, N_MOD, D)
    rope_mla = _rope_tables(S, C, MLA_ROPE)
    rope_gqa = _rope_tables(S, C, GQA_HEAD_DIM)

    for i in range(depth):
        last = i == depth - 1
        j = i // 2
        lat = [mod[i, :B, n][:, None, :] for n in range(N_MOD)]
        cmod = [mod[i, B:B + 1, n][:, None, :] for n in range(N_MOD)]
        g_mix = norm_mix[i][None, :]
        g_ffn = norm_ffn[i][None, :]

        if i % 2 == 0:
            w = _mla_weights(j, mla_w_dq, mla_g_dq, mla_w_uq, mla_g_q_nope, mla_g_q_pe,
                             mla_w_dkv, mla_g_dkv, mla_g_k_pe, mla_w_ukv, mla_g_k_nope)
            qkv, rope = _mla_qkv, rope_mla
            bound = _score_bound([(MLA_NOPE, mla_g_q_nope[j]), (MLA_ROPE, mla_g_q_pe[j])],
                                 [(MLA_NOPE, mla_g_k_nope[j]), (MLA_ROPE, mla_g_k_pe[j])],
                                 MLA_SCALE * LOG2E, B, MLA_HEADS)
            w_o = mla_w_o[j].astype(BF16)
            heads, kv_heads, dq, dv = MLA_HEADS, MLA_HEADS, MLA_QK_PAD, MLA_V
        else:
            w = _gqa_weights(j, gqa_w_q, gqa_g_q, gqa_w_kv, gqa_g_k)
            qkv, rope = _gqa_qkv, rope_gqa
            bound = _score_bound([(GQA_HEAD_DIM, gqa_g_q[j])], [(GQA_HEAD_DIM, gqa_g_k[j])],
                                 GQA_SCALE * LOG2E, B, GQA_HEADS)
            w_o = gqa_w_o[j].astype(BF16)
            heads, kv_heads, dq, dv = GQA_HEADS, GQA_KV_HEADS, GQA_HEAD_DIM, GQA_HEAD_DIM

        q, k, vt = qkv(x, g_mix, lat[0], lat[1], w, rope, ROW_TILE, 0)
        qc, kc, vtc = qkv(ctx, g_mix, cmod[0], cmod[1], w, rope, C, S // C)
        o = _attention(q, [(k, vt), (kc, vtc)], bound, heads, kv_heads, dq, dv, ATTN_Q_TILE)
        x = _oproj(o, w_o, x, lat[2], ROW_TILE)
        if not last:
            oc = _attention(qc, [(kc, vtc)], bound, heads, kv_heads, dq, dv, C)
            ctx = _oproj(oc, w_o, ctx, cmod[2], C)

        w_up = ffn_w_up[i].astype(BF16)
        w_down = ffn_w_down[i].astype(BF16)
        u = _ffn_up(x, g_ffn, lat[3], lat[4], w_up, FFN_UP_ROW_TILE, FFN_UP_COL_TILE)
        x = _ffn_down(u, ffn_conv_w[i], ffn_conv_b[i], w_down, x, lat[5],
                      FFN_DOWN_ROW_TILE, FFN_DOWN_K_TILE)
        if not last:
            uc = _ffn_up(ctx, g_ffn, cmod[3], cmod[4], w_up, C, FFN_UP_COL_TILE)
            ctx = _ffn_down(uc, ffn_conv_w[i], ffn_conv_b[i], w_down, ctx, cmod[5],
                            C, FFN_DOWN_K_TILE)
    return x
```

```python
import functools
import math

import jax
import jax.numpy as jnp
from jax import lax
from jax.experimental import pallas as pl
from jax.experimental.pallas import tpu as pltpu

F32 = jnp.float32
BF16 = jnp.bfloat16

GRID_W = 64
N_MOD = 6
MLA_HEADS = 16
MLA_Q_RANK = 512
MLA_KV_RANK = 512
MLA_NOPE = 128
MLA_ROPE = 64
MLA_V = 128
MLA_QK_PAD = 256
GQA_HEADS = 16
GQA_KV_HEADS = 4
GQA_HEAD_DIM = 128
CONV_W = 3
ROPE_BASE = 10000.0
EPS = 1e-6
LOG2E = 1.4426950408889634
MLA_SCALE = 1.0 / math.sqrt(MLA_NOPE + MLA_ROPE)
GQA_SCALE = 1.0 / math.sqrt(GQA_HEAD_DIM)

SAFE_LOG2_SPAN = 60.0
BOUND_MARGIN = 1.02

LANES = 128
BF16_SUBLANES = 16
VMEM_LIMIT_BYTES = 56 * 1024 * 1024

ROW_TILE = 512
MLA_HEAD_GROUP = 4
FFN_UP_ROW_TILE = 1024
FFN_UP_COL_TILE = 1024
FFN_DOWN_ROW_TILE = 512
FFN_DOWN_K_TILE = 1408
ATTN_Q_TILE = 1024
ATTN_K_CHUNK = 2048
MOD_COL_TILE = 1024
HALO = BF16_SUBLANES


def _cparams(sem):
    return pltpu.CompilerParams(dimension_semantics=sem, vmem_limit_bytes=VMEM_LIMIT_BYTES)


def _dot(a, b):
    return jnp.dot(a, b, preferred_element_type=F32)


def _rms(y, n):
    return y * lax.rsqrt(jnp.sum(y * y, axis=-1, keepdims=True) * (1.0 / n) + EPS)


def _modulate(x, g, shift, scale):
    return (_rms(x, x.shape[-1]) * g) * (1.0 + scale) + shift


def _rope(x, cos, sin_a, sin_b, quarter):
    return (x * cos + pltpu.roll(x, LANES - quarter, 1) * sin_a
            + pltpu.roll(x, quarter, 1) * sin_b)


def _mod_kernel(c_ref, w_ref, b_ref, o_ref):
    c = c_ref[...]
    sc = (c * jax.nn.sigmoid(c)).astype(BF16)
    o_ref[0] = _dot(sc, w_ref[0].astype(BF16)) + b_ref[0]


def _mod_all(cvec, w_mod, b_mod):
    L, D, N = w_mod.shape
    R = cvec.shape[0]
    return pl.pallas_call(
        _mod_kernel,
        grid=(L, N // MOD_COL_TILE),
        in_specs=[
            pl.BlockSpec((R, D), lambda l, j: (0, 0)),
            pl.BlockSpec((1, D, MOD_COL_TILE), lambda l, j: (l, 0, j)),
            pl.BlockSpec((1, 1, MOD_COL_TILE), lambda l, j: (l, 0, j)),
        ],
        out_specs=pl.BlockSpec((1, R, MOD_COL_TILE), lambda l, j: (l, 0, j)),
        out_shape=jax.ShapeDtypeStruct((L, R, N), F32),
        compiler_params=_cparams(("parallel", "parallel")),
        name="mod_all",
    )(cvec, w_mod, b_mod.reshape(L, 1, N))


def _mla_qkv_kernel(x_ref, g_ref, sh_ref, sc_ref, w1_ref, g1_ref, wuq_ref, gq_ref,
                    wukv_ref, gk_ref, cos_ref, sa_ref, sb_ref,
                    q_ref, k_ref, vt_ref):
    h = _modulate(x_ref[0], g_ref[...], sh_ref[0], sc_ref[0]).astype(BF16)
    y1 = _dot(h, w1_ref[...])
    g1 = g1_ref[...]
    cq = (_rms(y1[:, :MLA_Q_RANK], MLA_Q_RANK) * g1[:, :MLA_Q_RANK]).astype(BF16)
    lo, hi = MLA_Q_RANK, MLA_Q_RANK + MLA_KV_RANK
    ckv = (_rms(y1[:, lo:hi], MLA_KV_RANK) * g1[:, lo:hi]).astype(BF16)
    cos, sa, sb = cos_ref[...], sa_ref[...], sb_ref[...]
    quarter = MLA_ROPE // 4
    kpe = _rms(y1[:, hi:], MLA_ROPE) * g1[:, hi:]
    kpe = _rope(kpe, cos, sa, sb, quarter).astype(BF16)
    gq = gq_ref[...]
    gk = gk_ref[...]
    qscale = MLA_SCALE * LOG2E

    hg = MLA_HEAD_GROUP
    gw = hg * MLA_QK_PAD

    def group_dots(gi):
        return (_dot(cq, wuq_ref[:, gi * gw:(gi + 1) * gw]),
                _dot(ckv, wukv_ref[:, gi * gw:(gi + 1) * gw]))

    nxt = group_dots(0)
    for gi in range(MLA_HEADS // hg):
        yq, ykv = nxt
        if gi + 1 < MLA_HEADS // hg:
            nxt = group_dots(gi + 1)
        offs = [j * MLA_QK_PAD for j in range(hg)]
        qn = [yq[:, o:o + MLA_NOPE] for o in offs]
        qp = [yq[:, o + MLA_NOPE:o + MLA_QK_PAD] for o in offs]
        kn = [ykv[:, o:o + MLA_NOPE] for o in offs]
        ssq = [jnp.sum(t * t, axis=-1, keepdims=True) for t in qn + qp + kn]
        widths = [MLA_NOPE] * hg + [MLA_ROPE] * hg + [MLA_NOPE] * hg
        inv = [lax.rsqrt(s * (1.0 / n) + EPS) for s, n in zip(ssq, widths)]
        qn = [t * r * (gq[:, :MLA_NOPE] * qscale) for t, r in zip(qn, inv[:hg])]
        qp = [t * r * (gq[:, MLA_NOPE:] * qscale) for t, r in zip(qp, inv[hg:2 * hg])]
        kn = [t * r * gk for t, r in zip(kn, inv[2 * hg:])]
        ra = [pltpu.roll(t, LANES - quarter, 1) for t in qp]
        rb = [pltpu.roll(t, quarter, 1) for t in qp]
        qp = [t * cos + a * sa + b * sb for t, a, b in zip(qp, ra, rb)]
        vts = [ykv[:, o + MLA_NOPE:o + MLA_QK_PAD].T for o in offs]
        for j, o in enumerate(offs):
            c0 = gi * gw + o
            q_ref[0, :, c0:c0 + MLA_NOPE] = qn[j].astype(BF16)
            q_ref[0, :, c0 + MLA_NOPE:c0 + MLA_QK_PAD] = qp[j].astype(BF16)
            k_ref[0, :, c0:c0 + MLA_NOPE] = kn[j].astype(BF16)
            k_ref[0, :, c0 + MLA_NOPE:c0 + MLA_QK_PAD] = kpe
            vt_ref[0, gi * hg + j] = vts[j].astype(BF16)


def _mla_qkv(x, g, shift, scale, w, rope, tm, rope_blk0):
    B, R, D = x.shape
    H = MLA_HEADS
    nq = H * MLA_QK_PAD
    const2 = lambda b, i: (0, 0)
    mod_map = (lambda b, i: (b, 0, 0)) if shift.shape[0] == B else (lambda b, i: (0, 0, 0))
    rope_spec = pl.BlockSpec((tm, LANES), lambda b, i: (rope_blk0 + i, 0))
    return pl.pallas_call(
        _mla_qkv_kernel,
        grid=(B, R // tm),
        in_specs=[
            pl.BlockSpec((1, tm, D), lambda b, i: (b, i, 0)),
            pl.BlockSpec((1, D), const2),
            pl.BlockSpec((1, 1, D), mod_map),
            pl.BlockSpec((1, 1, D), mod_map),
            pl.BlockSpec(w["w1"].shape, const2),
            pl.BlockSpec(w["g1"].shape, const2),
            pl.BlockSpec(w["wuq"].shape, const2),
            pl.BlockSpec(w["gq"].shape, const2),
            pl.BlockSpec(w["wukv"].shape, const2),
            pl.BlockSpec(w["gk"].shape, const2),
            rope_spec, rope_spec, rope_spec,
        ],
        out_specs=[
            pl.BlockSpec((1, tm, nq), lambda b, i: (b, i, 0)),
            pl.BlockSpec((1, tm, nq), lambda b, i: (b, i, 0)),
            pl.BlockSpec((1, H, MLA_V, tm), lambda b, i: (b, 0, 0, i)),
        ],
        out_shape=[
            jax.ShapeDtypeStruct((B, R, nq), BF16),
            jax.ShapeDtypeStruct((B, R, nq), BF16),
            jax.ShapeDtypeStruct((B, H, MLA_V, R), BF16),
        ],
        compiler_params=_cparams(("parallel", "parallel")),
        name="mla_qkv",
    )(x, g, shift, scale, w["w1"], w["g1"], w["wuq"], w["gq"], w["wukv"], w["gk"], *rope)


def _gqa_qkv_kernel(x_ref, g_ref, sh_ref, sc_ref, w_ref, gq_ref, gk_ref,
                    cos_ref, sa_ref, sb_ref, q_ref, k_ref, vt_ref):
    h = _modulate(x_ref[0], g_ref[...], sh_ref[0], sc_ref[0]).astype(BF16)
    cos, sa, sb = cos_ref[...], sa_ref[...], sb_ref[...]
    quarter = GQA_HEAD_DIM // 4
    hd = GQA_HEAD_DIM
    group = GQA_KV_HEADS * hd
    nq = GQA_HEADS * hd
    gq = gq_ref[...] * (GQA_SCALE * LOG2E)
    gk = gk_ref[...]
    n_groups = (nq + 2 * group) // group
    heads = range(0, group, hd)

    def norm_rope_group(y, gain):
        tiles = [y[:, j:j + hd] for j in heads]
        ssq = [jnp.sum(t * t, axis=-1, keepdims=True) for t in tiles]
        inv = [lax.rsqrt(s * (1.0 / hd) + EPS) for s in ssq]
        xs = [t * r * gain for t, r in zip(tiles, inv)]
        ra = [pltpu.roll(x, LANES - quarter, 1) for x in xs]
        rb = [pltpu.roll(x, quarter, 1) for x in xs]
        return [(x * cos + a * sa + b * sb).astype(BF16) for x, a, b in zip(xs, ra, rb)]

    y_next = _dot(h, w_ref[:, 0:group])
    for gi in range(n_groups):
        c0 = gi * group
        y = y_next
        if gi + 1 < n_groups:
            y_next = _dot(h, w_ref[:, c0 + group:c0 + 2 * group])
        if c0 < nq:
            for j, qh in zip(heads, norm_rope_group(y, gq)):
                q_ref[0, :, c0 + j:c0 + j + hd] = qh
        elif c0 == nq:
            for j, kh in zip(heads, norm_rope_group(y, gk)):
                k_ref[0, :, j:j + hd] = kh
        else:
            vts = [y[:, j:j + hd].T.astype(BF16) for j in heads]
            for j, v in zip(heads, vts):
                vt_ref[0, j // hd] = v


def _gqa_qkv(x, g, shift, scale, w, rope, tm, rope_blk0):
    B, R, D = x.shape
    nq = GQA_HEADS * GQA_HEAD_DIM
    nk = GQA_KV_HEADS * GQA_HEAD_DIM
    const2 = lambda b, i: (0, 0)
    mod_map = (lambda b, i: (b, 0, 0)) if shift.shape[0] == B else (lambda b, i: (0, 0, 0))
    rope_spec = pl.BlockSpec((tm, LANES), lambda b, i: (rope_blk0 + i, 0))
    return pl.pallas_call(
        _gqa_qkv_kernel,
        grid=(B, R // tm),
        in_specs=[
            pl.BlockSpec((1, tm, D), lambda b, i: (b, i, 0)),
            pl.BlockSpec((1, D), const2),
            pl.BlockSpec((1, 1, D), mod_map),
            pl.BlockSpec((1, 1, D), mod_map),
            pl.BlockSpec(w["w"].shape, const2),
            pl.BlockSpec(w["gq"].shape, const2),
            pl.BlockSpec(w["gk"].shape, const2),
            rope_spec, rope_spec, rope_spec,
        ],
        out_specs=[
            pl.BlockSpec((1, tm, nq), lambda b, i: (b, i, 0)),
            pl.BlockSpec((1, tm, nk), lambda b, i: (b, i, 0)),
            pl.BlockSpec((1, GQA_KV_HEADS, GQA_HEAD_DIM, tm), lambda b, i: (b, 0, 0, i)),
        ],
        out_shape=[
            jax.ShapeDtypeStruct((B, R, nq), BF16),
            jax.ShapeDtypeStruct((B, R, nk), BF16),
            jax.ShapeDtypeStruct((B, GQA_KV_HEADS, GQA_HEAD_DIM, R), BF16),
        ],
        compiler_params=_cparams(("parallel", "parallel")),
        name="gqa_qkv",
    )(x, g, shift, scale, w["w"], w["gq"], w["gk"], *rope)


def _attn_kernel(bound_ref, *refs, chunks, n_src, group):
    q_ref = refs[0]
    kv_refs = refs[1:1 + 2 * n_src]
    o_ref = refs[1 + 2 * n_src]
    s_buf, p_buf, acc_ref = refs[2 + 2 * n_src:]
    n = len(chunks)
    bound = bound_ref[pl.program_id(0), pl.program_id(1) * group + pl.program_id(2)]

    def qk(c):
        src, start, size = chunks[c]
        k_c = kv_refs[2 * src][0, start:start + size, :]
        return lax.dot_general(k_c, q_ref[0], (((1,), (1,)), ((), ())),
                               preferred_element_type=F32)

    def vt(c):
        src, start, size = chunks[c]
        return kv_refs[2 * src + 1][0, 0, :, start:start + size]

    def bounded():
        acc = l = p_prev = None
        s_next = qk(0)
        for c in range(n):
            s = s_next
            if c + 1 < n:
                s_next = qk(c + 1)
            p = jnp.exp2(s - bound)
            psum = jnp.sum(p, axis=0, keepdims=True)
            l = psum if c == 0 else l + psum
            if c >= 1:
                y = _dot(vt(c - 1), p_prev)
                acc = y if c == 1 else acc + y
            p_prev = p.astype(BF16)
        y = _dot(vt(n - 1), p_prev)
        acc = y if n == 1 else acc + y
        o_ref[0] = (acc / l).T.astype(o_ref.dtype)

    def online():
        def qk_store(c):
            s = qk(c)
            s_buf[c % 2, 0:chunks[c][2], :] = s
            return jnp.max(s, axis=0, keepdims=True)

        def pv(c, alpha):
            y = _dot(vt(c), p_buf[c % 2, 0:chunks[c][2], :])
            if c == 0:
                acc_ref[...] = y
            else:
                acc_ref[...] = alpha * acc_ref[...] + y

        m_next = qk_store(0)
        m = l = alpha_prev = None
        for c in range(n):
            size = chunks[c][2]
            m_c = m_next
            if c + 1 < n:
                m_next = qk_store(c + 1)
            s = s_buf[c % 2, 0:size, :]
            if c == 0:
                m_new, alpha = m_c, None
            else:
                m_new = jnp.maximum(m, m_c)
                alpha = jnp.exp2(m - m_new)
            p = jnp.exp2(s - m_new)
            p_buf[c % 2, 0:size, :] = p.astype(BF16)
            psum = jnp.sum(p, axis=0, keepdims=True)
            l = psum if c == 0 else alpha * l + psum
            m = m_new
            if c >= 1:
                pv(c - 1, alpha_prev)
            alpha_prev = alpha
        pv(n - 1, alpha_prev)
        o_ref[0] = (acc_ref[...] / l).T.astype(o_ref.dtype)

    lax.cond(bound < SAFE_LOG2_SPAN, bounded, online)


def _attention(q, kv_sources, bound, n_heads, n_kv_heads, dq, dv, tq):
    B, Q, _ = q.shape
    G = n_heads // n_kv_heads
    chunks = []
    in_specs = [pl.BlockSpec((1, tq, dq), lambda b, hk, g, i, bnd: (b, i, hk * G + g))]
    args = [q]
    for src, (k, vt) in enumerate(kv_sources):
        T = k.shape[1]
        ck = min(ATTN_K_CHUNK, T)
        chunks += [(src, s, ck) for s in range(0, T, ck)]
        in_specs += [
            pl.BlockSpec((1, T, dq), lambda b, hk, g, i, bnd: (b, 0, hk)),
            pl.BlockSpec((1, 1, dv, T), lambda b, hk, g, i, bnd: (b, hk, 0, 0)),
        ]
        args += [k, vt]
    ck_max = max(size for _, _, size in chunks)
    return pl.pallas_call(
        functools.partial(_attn_kernel, chunks=tuple(chunks), n_src=len(kv_sources), group=G),
        grid_spec=pltpu.PrefetchScalarGridSpec(
            num_scalar_prefetch=1,
            grid=(B, n_kv_heads, G, Q // tq),
            in_specs=in_specs,
            out_specs=pl.BlockSpec((1, tq, dv), lambda b, hk, g, i, bnd: (b, i, hk * G + g)),
            scratch_shapes=[
                pltpu.VMEM((2, ck_max, tq), F32),
                pltpu.VMEM((2, ck_max, tq), BF16),
                pltpu.VMEM((dv, tq), F32),
            ],
        ),
        out_shape=jax.ShapeDtypeStruct((B, Q, n_heads * dv), BF16),
        compiler_params=_cparams(("parallel", "parallel", "parallel", "parallel")),
        name="attention",
    )(bound, *args)


def _score_bound(parts_q, parts_k, qscale, batch, n_heads):
    q2 = sum(n * jnp.max(g * g) for n, g in parts_q) * (qscale * qscale)
    k2 = sum(n * jnp.max(g * g) for n, g in parts_k)
    return jnp.full((batch, n_heads), jnp.sqrt(q2 * k2) * BOUND_MARGIN, F32)


def _oproj_kernel(o_ref, w_ref, x_ref, gate_ref, out_ref, *, col_tile):
    o = o_ref[0]
    gate = gate_ref[0]
    for c0 in range(0, out_ref.shape[-1], col_tile):
        y = _dot(o, w_ref[:, c0:c0 + col_tile])
        out_ref[0, :, c0:c0 + col_tile] = x_ref[0, :, c0:c0 + col_tile] + gate[:, c0:c0 + col_tile] * y


def _oproj(o, w_o, x, gate, tm):
    B, R, D = x.shape
    K = o.shape[-1]
    mod_map = (lambda b, i: (b, 0, 0)) if gate.shape[0] == B else (lambda b, i: (0, 0, 0))
    return pl.pallas_call(
        functools.partial(_oproj_kernel, col_tile=512),
        grid=(B, R // tm),
        in_specs=[
            pl.BlockSpec((1, tm, K), lambda b, i: (b, i, 0)),
            pl.BlockSpec((K, D), lambda b, i: (0, 0)),
            pl.BlockSpec((1, tm, D), lambda b, i: (b, i, 0)),
            pl.BlockSpec((1, 1, D), mod_map),
        ],
        out_specs=pl.BlockSpec((1, tm, D), lambda b, i: (b, i, 0)),
        out_shape=jax.ShapeDtypeStruct((B, R, D), F32),
        compiler_params=_cparams(("parallel", "parallel")),
        name="oproj",
    )(o, w_o, x, gate)


def _ffn_up_kernel(x_ref, g_ref, sh_ref, sc_ref, w_ref, u_ref, h_ref):
    @pl.when(pl.program_id(2) == 0)
    def _():
        h_ref[...] = _modulate(x_ref[0], g_ref[...], sh_ref[0], sc_ref[0]).astype(BF16)

    u_ref[0] = _dot(h_ref[...], w_ref[...].astype(BF16)).astype(u_ref.dtype)


def _ffn_up(x, g, shift, scale, w_up, tm, tn):
    B, R, D = x.shape
    N = w_up.shape[1]
    mod_map = (lambda b, i, j: (b, 0, 0)) if shift.shape[0] == B else (lambda b, i, j: (0, 0, 0))
    return pl.pallas_call(
        _ffn_up_kernel,
        grid=(B, R // tm, N // tn),
        in_specs=[
            pl.BlockSpec((1, tm, D), lambda b, i, j: (b, i, 0)),
            pl.BlockSpec((1, D), lambda b, i, j: (0, 0)),
            pl.BlockSpec((1, 1, D), mod_map),
            pl.BlockSpec((1, 1, D), mod_map),
            pl.BlockSpec((D, tn), lambda b, i, j: (0, j)),
        ],
        out_specs=pl.BlockSpec((1, tm, tn), lambda b, i, j: (b, i, j)),
        out_shape=jax.ShapeDtypeStruct((B, R, N), BF16),
        scratch_shapes=[pltpu.VMEM((tm, D), BF16)],
        compiler_params=_cparams(("parallel", "parallel", "arbitrary")),
        name="ffn_up",
    )(x, g, shift, scale, w_up)


def _ffn_down_kernel(ug_ref, uv_ref, prev_ref, next_ref, cw_ref, cb_ref, w_ref, x_ref, gate_ref,
                     out_ref):
    i = pl.program_id(1)
    kk = pl.program_id(2)
    tm = ug_ref.shape[1]
    first = i == 0
    last = i == pl.num_programs(1) - 1
    prev_row = jnp.where(first, 0.0, prev_ref[0, HALO - 1:HALO, :].astype(F32))
    next_row = jnp.where(last, 0.0, next_ref[0, 0:1, :].astype(F32))
    g = ug_ref[0].astype(F32)
    row = lax.broadcasted_iota(jnp.int32, g.shape, 0)
    g_prev = jnp.where(row == 0, prev_row, pltpu.roll(g, 1, 0))
    g_next = jnp.where(row == tm - 1, next_row, pltpu.roll(g, tm - 1, 0))
    cw = cw_ref[...]
    conv = g_prev * cw[0:1, :] + g * cw[1:2, :] + g_next * cw[2:3, :] + cb_ref[...]
    half = 0.5 * conv
    a = (half * (1.0 + jnp.tanh(half))) * uv_ref[0].astype(F32)
    y = _dot(a.astype(BF16), w_ref[...])

    @pl.when(kk == 0)
    def _():
        out_ref[0] = y

    @pl.when(kk > 0)
    def _():
        out_ref[0] += y

    @pl.when(kk == pl.num_programs(2) - 1)
    def _():
        out_ref[0] = x_ref[0] + gate_ref[0] * out_ref[0]


def _ffn_down(u, conv_w, conv_b, w_down, x, gate, tm, tk):
    B, R, D = x.shape
    F = w_down.shape[0]
    nk = F // tk
    rb = tm // HALO
    n_halo_blocks = R // HALO
    mod_map = (lambda b, i, k: (b, 0, 0)) if gate.shape[0] == B else (lambda b, i, k: (0, 0, 0))
    return pl.pallas_call(
        _ffn_down_kernel,
        grid=(B, R // tm, nk),
        in_specs=[
            pl.BlockSpec((1, tm, tk), lambda b, i, k: (b, i, k)),
            pl.BlockSpec((1, tm, tk), lambda b, i, k: (b, i, nk + k)),
            pl.BlockSpec((1, HALO, tk), lambda b, i, k: (b, jnp.maximum(i * rb - 1, 0), k)),
            pl.BlockSpec((1, HALO, tk),
                         lambda b, i, k: (b, jnp.minimum((i + 1) * rb, n_halo_blocks - 1), k)),
            pl.BlockSpec((CONV_W, tk), lambda b, i, k: (0, k)),
            pl.BlockSpec((1, tk), lambda b, i, k: (0, k)),
            pl.BlockSpec((tk, D), lambda b, i, k: (k, 0)),
            pl.BlockSpec((1, tm, D), lambda b, i, k: (b, i, 0)),
            pl.BlockSpec((1, 1, D), mod_map),
        ],
        out_specs=pl.BlockSpec((1, tm, D), lambda b, i, k: (b, i, 0)),
        out_shape=jax.ShapeDtypeStruct((B, R, D), F32),
        compiler_params=_cparams(("parallel", "parallel", "arbitrary")),
        name="ffn_down",
    )(u, u, u, u, conv_w, conv_b.reshape(1, F), w_down, x, gate)


def _rope_tables(seq, ctx_len, rot_dim):
    t = jnp.arange(seq, dtype=jnp.int32)
    rows = (t // GRID_W).astype(F32)
    cols = (t % GRID_W).astype(F32)
    axis_dim = rot_dim // 2
    inv = jnp.power(ROPE_BASE, -jnp.arange(0, axis_dim, 2, dtype=F32) / axis_dim)
    ang_r = rows[:, None] * inv
    ang_c = cols[:, None] * inv
    ang = jnp.concatenate([ang_r, ang_r, ang_c, ang_c], axis=-1)
    cos, sin = jnp.cos(ang), jnp.sin(ang)
    lane = jnp.arange(rot_dim)
    first_quarter = (lane % axis_dim) < (axis_dim // 2)
    sin_a = jnp.where(first_quarter, -sin, 0.0)
    sin_b = jnp.where(first_quarter, 0.0, sin)

    def finish(tab, ctx_value):
        tab = jnp.concatenate([tab, jnp.full((ctx_len, rot_dim), ctx_value, F32)], axis=0)
        return jnp.pad(tab, ((0, 0), (0, LANES - rot_dim)))

    return finish(cos, 1.0), finish(sin_a, 0.0), finish(sin_b, 0.0)


def _mla_weights(j, w_dq, g_dq, w_uq, g_q_nope, g_q_pe, w_dkv, g_dkv, g_k_pe, w_ukv, g_k_nope):
    D = w_dq.shape[1]
    pad = LANES - MLA_ROPE
    w1 = jnp.concatenate([w_dq[j], w_dkv[j], jnp.zeros((D, pad), F32)], axis=1).astype(BF16)
    g1 = jnp.concatenate([g_dq[j], g_dkv[j], g_k_pe[j], jnp.zeros((pad,), F32)])[None, :]
    wuq = w_uq[j].reshape(MLA_Q_RANK, MLA_HEADS, MLA_NOPE + MLA_ROPE)
    wuq = jnp.pad(wuq, ((0, 0), (0, 0), (0, MLA_QK_PAD - MLA_NOPE - MLA_ROPE)))
    wuq = wuq.reshape(MLA_Q_RANK, MLA_HEADS * MLA_QK_PAD).astype(BF16)
    gq = jnp.concatenate([g_q_nope[j], g_q_pe[j], jnp.zeros((pad,), F32)])[None, :]
    return dict(w1=w1, g1=g1, wuq=wuq, gq=gq, wukv=w_ukv[j].astype(BF16), gk=g_k_nope[j][None, :])


def _gqa_weights(j, w_q, g_q, w_kv, g_k):
    w = jnp.concatenate([w_q[j], w_kv[j]], axis=1).astype(BF16)
    return dict(w=w, gq=g_q[j][None, :], gk=g_k[j][None, :])


def kernel(x, c, ctx, c_ctx, w_mod, b_mod, norm_mix, norm_ffn, mla_w_dq, mla_g_dq, mla_w_uq, mla_g_q_nope, mla_g_q_pe, mla_w_dkv, mla_g_dkv, mla_g_k_pe, mla_w_ukv, mla_g_k_nope, mla_w_o, gqa_w_q, gqa_g_q, gqa_w_kv, gqa_g_k, gqa_w_o, ffn_w_up, ffn_conv_w, ffn_conv_b, ffn_w_down):
    B, S, D = x.shape
    C = ctx.shape[1]
    depth = w_mod.shape[0]
    assert S % FFN_UP_ROW_TILE == 0 and S % ROW_TILE == 0 and C % HALO == 0 and C <= ROW_TILE

    cvec = jnp.concatenate([c, c_ctx[None, :], jnp.zeros((8 - B - 1, D), F32)], axis=0)
    mod = _mod_all(cvec, w_mod, b_mod).reshape(depth, 8, N_MOD, D)
    rope_mla = _rope_tables(S, C, MLA_ROPE)
    rope_gqa = _rope_tables(S, C, GQA_HEAD_DIM)

    for i in range(depth):
        last = i == depth - 1
        j = i // 2
        lat = [mod[i, :B, n][:, None, :] for n in range(N_MOD)]
        cmod = [mod[i, B:B + 1, n][:, None, :] for n in range(N_MOD)]
        g_mix = norm_mix[i][None, :]
        g_ffn = norm_ffn[i][None, :]

        if i % 2 == 0:
            w = _mla_weights(j, mla_w_dq, mla_g_dq, mla_w_uq, mla_g_q_nope, mla_g_q_pe,
                             mla_w_dkv, mla_g_dkv, mla_g_k_pe, mla_w_ukv, mla_g_k_nope)
            qkv, rope = _mla_qkv, rope_mla
            bound = _score_bound([(MLA_NOPE, mla_g_q_nope[j]), (MLA_ROPE, mla_g_q_pe[j])],
                                 [(MLA_NOPE, mla_g_k_nope[j]), (MLA_ROPE, mla_g_k_pe[j])],
                                 MLA_SCALE * LOG2E, B, MLA_HEADS)
            w_o = mla_w_o[j].astype(BF16)
            heads, kv_heads, dq, dv = MLA_HEADS, MLA_HEADS, MLA_QK_PAD, MLA_V
        else:
            w = _gqa_weights(j, gqa_w_q, gqa_g_q, gqa_w_kv, gqa_g_k)
            qkv, rope = _gqa_qkv, rope_gqa
            bound = _score_bound([(GQA_HEAD_DIM, gqa_g_q[j])], [(GQA_HEAD_DIM, gqa_g_k[j])],
                                 GQA_SCALE * LOG2E, B, GQA_HEADS)
            w_o = gqa_w_o[j].astype(BF16)
            heads, kv_heads, dq, dv = GQA_HEADS, GQA_KV_HEADS, GQA_HEAD_DIM, GQA_HEAD_DIM

        q, k, vt = qkv(x, g_mix, lat[0], lat[1], w, rope, ROW_TILE, 0)
        qc, kc, vtc = qkv(ctx, g_mix, cmod[0], cmod[1], w, rope, C, S // C)
        o = _attention(q, [(k, vt), (kc, vtc)], bound, heads, kv_heads, dq, dv, ATTN_Q_TILE)
        x = _oproj(o, w_o, x, lat[2], ROW_TILE)
        if not last:
            oc = _attention(qc, [(kc, vtc)], bound, heads, kv_heads, dq, dv, C)
            ctx = _oproj(oc, w_o, ctx, cmod[2], C)

        w_up = ffn_w_up[i]
        w_down = ffn_w_down[i].astype(BF16)
        u = _ffn_up(x, g_ffn, lat[3], lat[4], w_up, FFN_UP_ROW_TILE, FFN_UP_COL_TILE)
        x = _ffn_down(u, ffn_conv_w[i], ffn_conv_b[i], w_down, x, lat[5],
                      FFN_DOWN_ROW_TILE, FFN_DOWN_K_TILE)
        if not last:
            uc = _ffn_up(ctx, g_ffn, cmod[3], cmod[4], w_up, C, FFN_UP_COL_TILE)
            ctx = _ffn_down(uc, ffn_conv_w[i], ffn_conv_b[i], w_down, ctx, cmod[5],
                            C, FFN_DOWN_K_TILE)
    return x
```

```python
import functools
import math

import jax
import jax.numpy as jnp
from jax import lax
from jax.experimental import pallas as pl
from jax.experimental.pallas import tpu as pltpu

F32 = jnp.float32
BF16 = jnp.bfloat16

GRID_W = 64
N_MOD = 6
MLA_HEADS = 16
MLA_Q_RANK = 512
MLA_KV_RANK = 512
MLA_NOPE = 128
MLA_ROPE = 64
MLA_V = 128
MLA_QK_PAD = 256
GQA_HEADS = 16
GQA_KV_HEADS = 4
GQA_HEAD_DIM = 128
CONV_W = 3
ROPE_BASE = 10000.0
EPS = 1e-6
LOG2E = 1.4426950408889634
MLA_SCALE = 1.0 / math.sqrt(MLA_NOPE + MLA_ROPE)
GQA_SCALE = 1.0 / math.sqrt(GQA_HEAD_DIM)

SAFE_LOG2_SPAN = 60.0
BOUND_MARGIN = 1.02

LANES = 128
BF16_SUBLANES = 16
VMEM_LIMIT_BYTES = 56 * 1024 * 1024

ROW_TILE = 512
MLA_HEAD_GROUP = 4
FFN_UP_ROW_TILE = 1024
FFN_UP_COL_TILE = 1024
FFN_NORM_ROW_STRIP = 128
FFN_NORM_COL_TILE = 256
FFN_DOWN_ROW_TILE = 512
FFN_DOWN_K_TILE = 1408
ATTN_Q_TILE = 1024
ATTN_K_CHUNK = 2048
MOD_COL_TILE = 1024
HALO = BF16_SUBLANES


def _cparams(sem):
    return pltpu.CompilerParams(dimension_semantics=sem, vmem_limit_bytes=VMEM_LIMIT_BYTES)


def _dot(a, b):
    return jnp.dot(a, b, preferred_element_type=F32)


def _rms(y, n):
    return y * lax.rsqrt(jnp.sum(y * y, axis=-1, keepdims=True) * (1.0 / n) + EPS)


def _modulate(x, g, shift, scale):
    return (_rms(x, x.shape[-1]) * g) * (1.0 + scale) + shift


def _rope(x, cos, sin_a, sin_b, quarter):
    return (x * cos + pltpu.roll(x, LANES - quarter, 1) * sin_a
            + pltpu.roll(x, quarter, 1) * sin_b)


def _mod_kernel(c_ref, w_ref, b_ref, o_ref):
    c = c_ref[...]
    sc = (c * jax.nn.sigmoid(c)).astype(BF16)
    o_ref[0] = _dot(sc, w_ref[0].astype(BF16)) + b_ref[0]


def _mod_all(cvec, w_mod, b_mod):
    L, D, N = w_mod.shape
    R = cvec.shape[0]
    return pl.pallas_call(
        _mod_kernel,
        grid=(L, N // MOD_COL_TILE),
        in_specs=[
            pl.BlockSpec((R, D), lambda l, j: (0, 0)),
            pl.BlockSpec((1, D, MOD_COL_TILE), lambda l, j: (l, 0, j)),
            pl.BlockSpec((1, 1, MOD_COL_TILE), lambda l, j: (l, 0, j)),
        ],
        out_specs=pl.BlockSpec((1, R, MOD_COL_TILE), lambda l, j: (l, 0, j)),
        out_shape=jax.ShapeDtypeStruct((L, R, N), F32),
        compiler_params=_cparams(("parallel", "parallel")),
        name="mod_all",
    )(cvec, w_mod, b_mod.reshape(L, 1, N))


def _mla_qkv_kernel(x_ref, g_ref, sh_ref, sc_ref, w1_ref, g1_ref, wuq_ref, gq_ref,
                    wukv_ref, gk_ref, cos_ref, sa_ref, sb_ref,
                    q_ref, k_ref, vt_ref):
    h = _modulate(x_ref[0], g_ref[...], sh_ref[0], sc_ref[0]).astype(BF16)
    y1 = _dot(h, w1_ref[...])
    g1 = g1_ref[...]
    cq = (_rms(y1[:, :MLA_Q_RANK], MLA_Q_RANK) * g1[:, :MLA_Q_RANK]).astype(BF16)
    lo, hi = MLA_Q_RANK, MLA_Q_RANK + MLA_KV_RANK
    ckv = (_rms(y1[:, lo:hi], MLA_KV_RANK) * g1[:, lo:hi]).astype(BF16)
    cos, sa, sb = cos_ref[...], sa_ref[...], sb_ref[...]
    quarter = MLA_ROPE // 4
    kpe = _rms(y1[:, hi:], MLA_ROPE) * g1[:, hi:]
    kpe = _rope(kpe, cos, sa, sb, quarter).astype(BF16)
    gq = gq_ref[...]
    gk = gk_ref[...]
    qscale = MLA_SCALE * LOG2E

    hg = MLA_HEAD_GROUP
    gw = hg * MLA_QK_PAD

    def group_dots(gi):
        return (_dot(cq, wuq_ref[:, gi * gw:(gi + 1) * gw]),
                _dot(ckv, wukv_ref[:, gi * gw:(gi + 1) * gw]))

    nxt = group_dots(0)
    for gi in range(MLA_HEADS // hg):
        yq, ykv = nxt
        if gi + 1 < MLA_HEADS // hg:
            nxt = group_dots(gi + 1)
        offs = [j * MLA_QK_PAD for j in range(hg)]
        qn = [yq[:, o:o + MLA_NOPE] for o in offs]
        qp = [yq[:, o + MLA_NOPE:o + MLA_QK_PAD] for o in offs]
        kn = [ykv[:, o:o + MLA_NOPE] for o in offs]
        ssq = [jnp.sum(t * t, axis=-1, keepdims=True) for t in qn + qp + kn]
        widths = [MLA_NOPE] * hg + [MLA_ROPE] * hg + [MLA_NOPE] * hg
        inv = [lax.rsqrt(s * (1.0 / n) + EPS) for s, n in zip(ssq, widths)]
        qn = [t * r * (gq[:, :MLA_NOPE] * qscale) for t, r in zip(qn, inv[:hg])]
        qp = [t * r * (gq[:, MLA_NOPE:] * qscale) for t, r in zip(qp, inv[hg:2 * hg])]
        kn = [t * r * gk for t, r in zip(kn, inv[2 * hg:])]
        ra = [pltpu.roll(t, LANES - quarter, 1) for t in qp]
        rb = [pltpu.roll(t, quarter, 1) for t in qp]
        qp = [t * cos + a * sa + b * sb for t, a, b in zip(qp, ra, rb)]
        vts = [ykv[:, o + MLA_NOPE:o + MLA_QK_PAD].T for o in offs]
        for j, o in enumerate(offs):
            c0 = gi * gw + o
            q_ref[0, :, c0:c0 + MLA_NOPE] = qn[j].astype(BF16)
            q_ref[0, :, c0 + MLA_NOPE:c0 + MLA_QK_PAD] = qp[j].astype(BF16)
            k_ref[0, :, c0:c0 + MLA_NOPE] = kn[j].astype(BF16)
            k_ref[0, :, c0 + MLA_NOPE:c0 + MLA_QK_PAD] = kpe
            vt_ref[0, gi * hg + j] = vts[j].astype(BF16)


def _mla_qkv(x, g, shift, scale, w, rope, tm, rope_blk0):
    B, R, D = x.shape
    H = MLA_HEADS
    nq = H * MLA_QK_PAD
    const2 = lambda b, i: (0, 0)
    mod_map = (lambda b, i: (b, 0, 0)) if shift.shape[0] == B else (lambda b, i: (0, 0, 0))
    rope_spec = pl.BlockSpec((tm, LANES), lambda b, i: (rope_blk0 + i, 0))
    return pl.pallas_call(
        _mla_qkv_kernel,
        grid=(B, R // tm),
        in_specs=[
            pl.BlockSpec((1, tm, D), lambda b, i: (b, i, 0)),
            pl.BlockSpec((1, D), const2),
            pl.BlockSpec((1, 1, D), mod_map),
            pl.BlockSpec((1, 1, D), mod_map),
            pl.BlockSpec(w["w1"].shape, const2),
            pl.BlockSpec(w["g1"].shape, const2),
            pl.BlockSpec(w["wuq"].shape, const2),
            pl.BlockSpec(w["gq"].shape, const2),
            pl.BlockSpec(w["wukv"].shape, const2),
            pl.BlockSpec(w["gk"].shape, const2),
            rope_spec, rope_spec, rope_spec,
        ],
        out_specs=[
            pl.BlockSpec((1, tm, nq), lambda b, i: (b, i, 0)),
            pl.BlockSpec((1, tm, nq), lambda b, i: (b, i, 0)),
            pl.BlockSpec((1, H, MLA_V, tm), lambda b, i: (b, 0, 0, i)),
        ],
        out_shape=[
            jax.ShapeDtypeStruct((B, R, nq), BF16),
            jax.ShapeDtypeStruct((B, R, nq), BF16),
            jax.ShapeDtypeStruct((B, H, MLA_V, R), BF16),
        ],
        compiler_params=_cparams(("parallel", "parallel")),
        name="mla_qkv",
    )(x, g, shift, scale, w["w1"], w["g1"], w["wuq"], w["gq"], w["wukv"], w["gk"], *rope)


def _gqa_qkv_kernel(x_ref, g_ref, sh_ref, sc_ref, w_ref, gq_ref, gk_ref,
                    cos_ref, sa_ref, sb_ref, q_ref, k_ref, vt_ref):
    h = _modulate(x_ref[0], g_ref[...], sh_ref[0], sc_ref[0]).astype(BF16)
    cos, sa, sb = cos_ref[...], sa_ref[...], sb_ref[...]
    quarter = GQA_HEAD_DIM // 4
    hd = GQA_HEAD_DIM
    group = GQA_KV_HEADS * hd
    nq = GQA_HEADS * hd
    gq = gq_ref[...] * (GQA_SCALE * LOG2E)
    gk = gk_ref[...]
    n_groups = (nq + 2 * group) // group
    heads = range(0, group, hd)

    def norm_rope_group(y, gain):
        tiles = [y[:, j:j + hd] for j in heads]
        ssq = [jnp.sum(t * t, axis=-1, keepdims=True) for t in tiles]
        inv = [lax.rsqrt(s * (1.0 / hd) + EPS) for s in ssq]
        xs = [t * r * gain for t, r in zip(tiles, inv)]
        ra = [pltpu.roll(x, LANES - quarter, 1) for x in xs]
        rb = [pltpu.roll(x, quarter, 1) for x in xs]
        return [(x * cos + a * sa + b * sb).astype(BF16) for x, a, b in zip(xs, ra, rb)]

    y_next = _dot(h, w_ref[:, 0:group])
    for gi in range(n_groups):
        c0 = gi * group
        y = y_next
        if gi + 1 < n_groups:
            y_next = _dot(h, w_ref[:, c0 + group:c0 + 2 * group])
        if c0 < nq:
            for j, qh in zip(heads, norm_rope_group(y, gq)):
                q_ref[0, :, c0 + j:c0 + j + hd] = qh
        elif c0 == nq:
            for j, kh in zip(heads, norm_rope_group(y, gk)):
                k_ref[0, :, j:j + hd] = kh
        else:
            vts = [y[:, j:j + hd].T.astype(BF16) for j in heads]
            for j, v in zip(heads, vts):
                vt_ref[0, j // hd] = v


def _gqa_qkv(x, g, shift, scale, w, rope, tm, rope_blk0):
    B, R, D = x.shape
    nq = GQA_HEADS * GQA_HEAD_DIM
    nk = GQA_KV_HEADS * GQA_HEAD_DIM
    const2 = lambda b, i: (0, 0)
    mod_map = (lambda b, i: (b, 0, 0)) if shift.shape[0] == B else (lambda b, i: (0, 0, 0))
    rope_spec = pl.BlockSpec((tm, LANES), lambda b, i: (rope_blk0 + i, 0))
    return pl.pallas_call(
        _gqa_qkv_kernel,
        grid=(B, R // tm),
        in_specs=[
            pl.BlockSpec((1, tm, D), lambda b, i: (b, i, 0)),
            pl.BlockSpec((1, D), const2),
            pl.BlockSpec((1, 1, D), mod_map),
            pl.BlockSpec((1, 1, D), mod_map),
            pl.BlockSpec(w["w"].shape, const2),
            pl.BlockSpec(w["gq"].shape, const2),
            pl.BlockSpec(w["gk"].shape, const2),
            rope_spec, rope_spec, rope_spec,
        ],
        out_specs=[
            pl.BlockSpec((1, tm, nq), lambda b, i: (b, i, 0)),
            pl.BlockSpec((1, tm, nk), lambda b, i: (b, i, 0)),
            pl.BlockSpec((1, GQA_KV_HEADS, GQA_HEAD_DIM, tm), lambda b, i: (b, 0, 0, i)),
        ],
        out_shape=[
            jax.ShapeDtypeStruct((B, R, nq), BF16),
            jax.ShapeDtypeStruct((B, R, nk), BF16),
            jax.ShapeDtypeStruct((B, GQA_KV_HEADS, GQA_HEAD_DIM, R), BF16),
        ],
        compiler_params=_cparams(("parallel", "parallel")),
        name="gqa_qkv",
    )(x, g, shift, scale, w["w"], w["gq"], w["gk"], *rope)


def _attn_kernel(bound_ref, *refs, chunks, n_src, group):
    q_ref = refs[0]
    kv_refs = refs[1:1 + 2 * n_src]
    o_ref = refs[1 + 2 * n_src]
    s_buf, p_buf, acc_ref = refs[2 + 2 * n_src:]
    n = len(chunks)
    bound = bound_ref[pl.program_id(0), pl.program_id(1) * group + pl.program_id(2)]

    def qk(c):
        src, start, size = chunks[c]
        k_c = kv_refs[2 * src][0, start:start + size, :]
        return lax.dot_general(k_c, q_ref[0], (((1,), (1,)), ((), ())),
                               preferred_element_type=F32)

    def vt(c):
        src, start, size = chunks[c]
        return kv_refs[2 * src + 1][0, 0, :, start:start + size]

    def bounded():
        acc = l = p_prev = None
        s_next = qk(0)
        for c in range(n):
            s = s_next
            if c + 1 < n:
                s_next = qk(c + 1)
            p = jnp.exp2(s - bound)
            psum = jnp.sum(p, axis=0, keepdims=True)
            l = psum if c == 0 else l + psum
            if c >= 1:
                y = _dot(vt(c - 1), p_prev)
                acc = y if c == 1 else acc + y
            p_prev = p.astype(BF16)
        y = _dot(vt(n - 1), p_prev)
        acc = y if n == 1 else acc + y
        o_ref[0] = (acc / l).T.astype(o_ref.dtype)

    def online():
        def qk_store(c):
            s = qk(c)
            s_buf[c % 2, 0:chunks[c][2], :] = s
            return jnp.max(s, axis=0, keepdims=True)

        def pv(c, alpha):
            y = _dot(vt(c), p_buf[c % 2, 0:chunks[c][2], :])
            if c == 0:
                acc_ref[...] = y
            else:
                acc_ref[...] = alpha * acc_ref[...] + y

        m_next = qk_store(0)
        m = l = alpha_prev = None
        for c in range(n):
            size = chunks[c][2]
            m_c = m_next
            if c + 1 < n:
                m_next = qk_store(c + 1)
            s = s_buf[c % 2, 0:size, :]
            if c == 0:
                m_new, alpha = m_c, None
            else:
                m_new = jnp.maximum(m, m_c)
                alpha = jnp.exp2(m - m_new)
            p = jnp.exp2(s - m_new)
            p_buf[c % 2, 0:size, :] = p.astype(BF16)
            psum = jnp.sum(p, axis=0, keepdims=True)
            l = psum if c == 0 else alpha * l + psum
            m = m_new
            if c >= 1:
                pv(c - 1, alpha_prev)
            alpha_prev = alpha
        pv(n - 1, alpha_prev)
        o_ref[0] = (acc_ref[...] / l).T.astype(o_ref.dtype)

    lax.cond(bound < SAFE_LOG2_SPAN, bounded, online)


def _attention(q, kv_sources, bound, n_heads, n_kv_heads, dq, dv, tq):
    B, Q, _ = q.shape
    G = n_heads // n_kv_heads
    chunks = []
    in_specs = [pl.BlockSpec((1, tq, dq), lambda b, hk, g, i, bnd: (b, i, hk * G + g))]
    args = [q]
    for src, (k, vt) in enumerate(kv_sources):
        T = k.shape[1]
        ck = min(ATTN_K_CHUNK, T)
        chunks += [(src, s, ck) for s in range(0, T, ck)]
        in_specs += [
            pl.BlockSpec((1, T, dq), lambda b, hk, g, i, bnd: (b, 0, hk)),
            pl.BlockSpec((1, 1, dv, T), lambda b, hk, g, i, bnd: (b, hk, 0, 0)),
        ]
        args += [k, vt]
    ck_max = max(size for _, _, size in chunks)
    return pl.pallas_call(
        functools.partial(_attn_kernel, chunks=tuple(chunks), n_src=len(kv_sources), group=G),
        grid_spec=pltpu.PrefetchScalarGridSpec(
            num_scalar_prefetch=1,
            grid=(B, n_kv_heads, G, Q // tq),
            in_specs=in_specs,
            out_specs=pl.BlockSpec((1, tq, dv), lambda b, hk, g, i, bnd: (b, i, hk * G + g)),
            scratch_shapes=[
                pltpu.VMEM((2, ck_max, tq), F32),
                pltpu.VMEM((2, ck_max, tq), BF16),
                pltpu.VMEM((dv, tq), F32),
            ],
        ),
        out_shape=jax.ShapeDtypeStruct((B, Q, n_heads * dv), BF16),
        compiler_params=_cparams(("parallel", "parallel", "parallel", "parallel")),
        name="attention",
    )(bound, *args)


def _score_bound(parts_q, parts_k, qscale, batch, n_heads):
    q2 = sum(n * jnp.max(g * g) for n, g in parts_q) * (qscale * qscale)
    k2 = sum(n * jnp.max(g * g) for n, g in parts_k)
    return jnp.full((batch, n_heads), jnp.sqrt(q2 * k2) * BOUND_MARGIN, F32)


def _oproj_kernel(o_ref, w_ref, x_ref, gate_ref, out_ref, *, col_tile):
    o = o_ref[0]
    gate = gate_ref[0]
    for c0 in range(0, out_ref.shape[-1], col_tile):
        y = _dot(o, w_ref[:, c0:c0 + col_tile])
        out_ref[0, :, c0:c0 + col_tile] = x_ref[0, :, c0:c0 + col_tile] + gate[:, c0:c0 + col_tile] * y


def _oproj(o, w_o, x, gate, tm):
    B, R, D = x.shape
    K = o.shape[-1]
    mod_map = (lambda b, i: (b, 0, 0)) if gate.shape[0] == B else (lambda b, i: (0, 0, 0))
    return pl.pallas_call(
        functools.partial(_oproj_kernel, col_tile=512),
        grid=(B, R // tm),
        in_specs=[
            pl.BlockSpec((1, tm, K), lambda b, i: (b, i, 0)),
            pl.BlockSpec((K, D), lambda b, i: (0, 0)),
            pl.BlockSpec((1, tm, D), lambda b, i: (b, i, 0)),
            pl.BlockSpec((1, 1, D), mod_map),
        ],
        out_specs=pl.BlockSpec((1, tm, D), lambda b, i: (b, i, 0)),
        out_shape=jax.ShapeDtypeStruct((B, R, D), F32),
        compiler_params=_cparams(("parallel", "parallel")),
        name="oproj",
    )(o, w_o, x, gate)


def _ffn_up_kernel(x_ref, g_ref, sh_ref, sc_ref, w_ref, u_ref, h_ref):
    @pl.when(pl.program_id(2) == 0)
    def _():
        tm, d = h_ref.shape
        inv = []
        for r0 in range(0, tm, FFN_NORM_ROW_STRIP):
            xs = x_ref[0, r0:r0 + FFN_NORM_ROW_STRIP, :]
            inv.append(lax.rsqrt(jnp.sum(xs * xs, axis=-1, keepdims=True) * (1.0 / d) + EPS))
        inv = jnp.concatenate(inv, axis=0)
        for c0 in range(0, d, FFN_NORM_COL_TILE):
            c1 = c0 + FFN_NORM_COL_TILE
            y = (x_ref[0, :, c0:c1] * inv) * g_ref[:, c0:c1]
            h_ref[:, c0:c1] = (y * (1.0 + sc_ref[0, :, c0:c1]) + sh_ref[0, :, c0:c1]).astype(BF16)

    u_ref[0] = _dot(h_ref[...], w_ref[...]).astype(u_ref.dtype)


def _ffn_up(x, g, shift, scale, w_up, tm, tn):
    B, R, D = x.shape
    N = w_up.shape[1]
    mod_map = (lambda b, i, j: (b, 0, 0)) if shift.shape[0] == B else (lambda b, i, j: (0, 0, 0))
    return pl.pallas_call(
        _ffn_up_kernel,
        grid=(B, R // tm, N // tn),
        in_specs=[
            pl.BlockSpec((1, tm, D), lambda b, i, j: (b, i, 0)),
            pl.BlockSpec((1, D), lambda b, i, j: (0, 0)),
            pl.BlockSpec((1, 1, D), mod_map),
            pl.BlockSpec((1, 1, D), mod_map),
            pl.BlockSpec((D, tn), lambda b, i, j: (0, j)),
        ],
        out_specs=pl.BlockSpec((1, tm, tn), lambda b, i, j: (b, i, j)),
        out_shape=jax.ShapeDtypeStruct((B, R, N), BF16),
        scratch_shapes=[pltpu.VMEM((tm, D), BF16)],
        compiler_params=_cparams(("parallel", "parallel", "arbitrary")),
        name="ffn_up",
    )(x, g, shift, scale, w_up)


def _ffn_down_kernel(ug_ref, uv_ref, prev_ref, next_ref, cw_ref, cb_ref, w_ref, x_ref, gate_ref,
                     out_ref):
    i = pl.program_id(1)
    kk = pl.program_id(2)
    tm = ug_ref.shape[1]
    first = i == 0
    last = i == pl.num_programs(1) - 1
    prev_row = jnp.where(first, 0.0, prev_ref[0, HALO - 1:HALO, :].astype(F32))
    next_row = jnp.where(last, 0.0, next_ref[0, 0:1, :].astype(F32))
    g = ug_ref[0].astype(F32)
    row = lax.broadcasted_iota(jnp.int32, g.shape, 0)
    g_prev = jnp.where(row == 0, prev_row, pltpu.roll(g, 1, 0))
    g_next = jnp.where(row == tm - 1, next_row, pltpu.roll(g, tm - 1, 0))
    cw = cw_ref[...]
    conv = g_prev * cw[0:1, :] + g * cw[1:2, :] + g_next * cw[2:3, :] + cb_ref[...]
    half = 0.5 * conv
    a = (half * (1.0 + jnp.tanh(half))) * uv_ref[0].astype(F32)
    y = _dot(a.astype(BF16), w_ref[...])

    @pl.when(kk == 0)
    def _():
        out_ref[0] = y

    @pl.when(kk > 0)
    def _():
        out_ref[0] += y

    @pl.when(kk == pl.num_programs(2) - 1)
    def _():
        out_ref[0] = x_ref[0] + gate_ref[0] * out_ref[0]


def _ffn_down(u, conv_w, conv_b, w_down, x, gate, tm, tk):
    B, R, D = x.shape
    F = w_down.shape[0]
    nk = F // tk
    rb = tm // HALO
    n_halo_blocks = R // HALO
    mod_map = (lambda b, i, k: (b, 0, 0)) if gate.shape[0] == B else (lambda b, i, k: (0, 0, 0))
    return pl.pallas_call(
        _ffn_down_kernel,
        grid=(B, R // tm, nk),
        in_specs=[
            pl.BlockSpec((1, tm, tk), lambda b, i, k: (b, i, k)),
            pl.BlockSpec((1, tm, tk), lambda b, i, k: (b, i, nk + k)),
            pl.BlockSpec((1, HALO, tk), lambda b, i, k: (b, jnp.maximum(i * rb - 1, 0), k)),
            pl.BlockSpec((1, HALO, tk),
                         lambda b, i, k: (b, jnp.minimum((i + 1) * rb, n_halo_blocks - 1), k)),
            pl.BlockSpec((CONV_W, tk), lambda b, i, k: (0, k)),
            pl.BlockSpec((1, tk), lambda b, i, k: (0, k)),
            pl.BlockSpec((tk, D), lambda b, i, k: (k, 0)),
            pl.BlockSpec((1, tm, D), lambda b, i, k: (b, i, 0)),
            pl.BlockSpec((1, 1, D), mod_map),
        ],
        out_specs=pl.BlockSpec((1, tm, D), lambda b, i, k: (b, i, 0)),
        out_shape=jax.ShapeDtypeStruct((B, R, D), F32),
        compiler_params=_cparams(("parallel", "parallel", "arbitrary")),
        name="ffn_down",
    )(u, u, u, u, conv_w, conv_b.reshape(1, F), w_down, x, gate)


def _rope_tables(seq, ctx_len, rot_dim):
    t = jnp.arange(seq, dtype=jnp.int32)
    rows = (t // GRID_W).astype(F32)
    cols = (t % GRID_W).astype(F32)
    axis_dim = rot_dim // 2
    inv = jnp.power(ROPE_BASE, -jnp.arange(0, axis_dim, 2, dtype=F32) / axis_dim)
    ang_r = rows[:, None] * inv
    ang_c = cols[:, None] * inv
    ang = jnp.concatenate([ang_r, ang_r, ang_c, ang_c], axis=-1)
    cos, sin = jnp.cos(ang), jnp.sin(ang)
    lane = jnp.arange(rot_dim)
    first_quarter = (lane % axis_dim) < (axis_dim // 2)
    sin_a = jnp.where(first_quarter, -sin, 0.0)
    sin_b = jnp.where(first_quarter, 0.0, sin)

    def finish(tab, ctx_value):
        tab = jnp.concatenate([tab, jnp.full((ctx_len, rot_dim), ctx_value, F32)], axis=0)
        return jnp.pad(tab, ((0, 0), (0, LANES - rot_dim)))

    return finish(cos, 1.0), finish(sin_a, 0.0), finish(sin_b, 0.0)


def _mla_weights(j, w_dq, g_dq, w_uq, g_q_nope, g_q_pe, w_dkv, g_dkv, g_k_pe, w_ukv, g_k_nope):
    D = w_dq.shape[1]
    pad = LANES - MLA_ROPE
    w1 = jnp.concatenate([w_dq[j], w_dkv[j], jnp.zeros((D, pad), F32)], axis=1).astype(BF16)
    g1 = jnp.concatenate([g_dq[j], g_dkv[j], g_k_pe[j], jnp.zeros((pad,), F32)])[None, :]
    wuq = w_uq[j].reshape(MLA_Q_RANK, MLA_HEADS, MLA_NOPE + MLA_ROPE)
    wuq = jnp.pad(wuq, ((0, 0), (0, 0), (0, MLA_QK_PAD - MLA_NOPE - MLA_ROPE)))
    wuq = wuq.reshape(MLA_Q_RANK, MLA_HEADS * MLA_QK_PAD).astype(BF16)
    gq = jnp.concatenate([g_q_nope[j], g_q_pe[j], jnp.zeros((pad,), F32)])[None, :]
    return dict(w1=w1, g1=g1, wuq=wuq, gq=gq, wukv=w_ukv[j].astype(BF16), gk=g_k_nope[j][None, :])


def _gqa_weights(j, w_q, g_q, w_kv, g_k):
    w = jnp.concatenate([w_q[j], w_kv[j]], axis=1).astype(BF16)
    return dict(w=w, gq=g_q[j][None, :], gk=g_k[j][None, :])


def kernel(x, c, ctx, c_ctx, w_mod, b_mod, norm_mix, norm_ffn, mla_w_dq, mla_g_dq, mla_w_uq, mla_g_q_nope, mla_g_q_pe, mla_w_dkv, mla_g_dkv, mla_g_k_pe, mla_w_ukv, mla_g_k_nope, mla_w_o, gqa_w_q, gqa_g_q, gqa_w_kv, gqa_g_k, gqa_w_o, ffn_w_up, ffn_conv_w, ffn_conv_b, ffn_w_down):
    B, S, D = x.shape
    C = ctx.shape[1]
    depth = w_mod.shape[0]
    assert S % FFN_UP_ROW_TILE == 0 and S % ROW_TILE == 0 and C % HALO == 0 and C <= ROW_TILE

    cvec = jnp.concatenate([c, c_ctx[None, :], jnp.zeros((8 - B - 1, D), F32)], axis=0)
    mod = _mod_all(cvec, w_mod, b_mod).reshape(depth, 8, N_MOD, D)
    rope_mla = _rope_tables(S, C, MLA_ROPE)
    rope_gqa = _rope_tables(S, C, GQA_HEAD_DIM)

    for i in range(depth):
        last = i == depth - 1
        j = i // 2
        lat = [mod[i, :B, n][:, None, :] for n in range(N_MOD)]
        cmod = [mod[i, B:B + 1, n][:, None, :] for n in range(N_MOD)]
        g_mix = norm_mix[i][None, :]
        g_ffn = norm_ffn[i][None, :]

        if i % 2 == 0:
            w = _mla_weights(j, mla_w_dq, mla_g_dq, mla_w_uq, mla_g_q_nope, mla_g_q_pe,
                             mla_w_dkv, mla_g_dkv, mla_g_k_pe, mla_w_ukv, mla_g_k_nope)
            qkv, rope = _mla_qkv, rope_mla
            bound = _score_bound([(MLA_NOPE, mla_g_q_nope[j]), (MLA_ROPE, mla_g_q_pe[j])],
                                 [(MLA_NOPE, mla_g_k_nope[j]), (MLA_ROPE, mla_g_k_pe[j])],
                                 MLA_SCALE * LOG2E, B, MLA_HEADS)
            w_o = mla_w_o[j].astype(BF16)
            heads, kv_heads, dq, dv = MLA_HEADS, MLA_HEADS, MLA_QK_PAD, MLA_V
        else:
            w = _gqa_weights(j, gqa_w_q, gqa_g_q, gqa_w_kv, gqa_g_k)
            qkv, rope = _gqa_qkv, rope_gqa
            bound = _score_bound([(GQA_HEAD_DIM, gqa_g_q[j])], [(GQA_HEAD_DIM, gqa_g_k[j])],
                                 GQA_SCALE * LOG2E, B, GQA_HEADS)
            w_o = gqa_w_o[j].astype(BF16)
            heads, kv_heads, dq, dv = GQA_HEADS, GQA_KV_HEADS, GQA_HEAD_DIM, GQA_HEAD_DIM

        q, k, vt = qkv(x, g_mix, lat[0], lat[1], w, rope, ROW_TILE, 0)
        qc, kc, vtc = qkv(ctx, g_mix, cmod[0], cmod[1], w, rope, C, S // C)
        o = _attention(q, [(k, vt), (kc, vtc)], bound, heads, kv_heads, dq, dv, ATTN_Q_TILE)
        x = _oproj(o, w_o, x, lat[2], ROW_TILE)
        if not last:
            oc = _attention(qc, [(kc, vtc)], bound, heads, kv_heads, dq, dv, C)
            ctx = _oproj(oc, w_o, ctx, cmod[2], C)

        w_up = ffn_w_up[i].astype(BF16)
        w_down = ffn_w_down[i].astype(BF16)
        u = _ffn_up(x, g_ffn, lat[3], lat[4], w_up, FFN_UP_ROW_TILE, FFN_UP_COL_TILE)
        x = _ffn_down(u, ffn_conv_w[i], ffn_conv_b[i], w_down, x, lat[5],
                      FFN_DOWN_ROW_TILE, FFN_DOWN_K_TILE)
        if not last:
            uc = _ffn_up(ctx, g_ffn, cmod[3], cmod[4], w_up, C, FFN_UP_COL_TILE)
            ctx = _ffn_down(uc, ffn_conv_w[i], ffn_conv_b[i], w_down, ctx, cmod[5],
                            C, FFN_DOWN_K_TILE)
    return x
```

```python
import functools
import math

import jax
import jax.numpy as jnp
from jax import lax
from jax.experimental import pallas as pl
from jax.experimental.pallas import tpu as pltpu

F32 = jnp.float32
BF16 = jnp.bfloat16

GRID_W = 64
N_MOD = 6
MLA_HEADS = 16
MLA_Q_RANK = 512
MLA_KV_RANK = 512
MLA_NOPE = 128
MLA_ROPE = 64
MLA_V = 128
MLA_QK_PAD = 256
GQA_HEADS = 16
GQA_KV_HEADS = 4
GQA_HEAD_DIM = 128
CONV_W = 3
ROPE_BASE = 10000.0
EPS = 1e-6
LOG2E = 1.4426950408889634
MLA_SCALE = 1.0 / math.sqrt(MLA_NOPE + MLA_ROPE)
GQA_SCALE = 1.0 / math.sqrt(GQA_HEAD_DIM)

SAFE_LOG2_SPAN = 60.0
BOUND_MARGIN = 1.02

LANES = 128
BF16_SUBLANES = 16
VMEM_LIMIT_BYTES = 56 * 1024 * 1024

ROW_TILE = 512
MLA_HEAD_GROUP = 4
FFN_UP_ROW_TILE = 1024
FFN_UP_COL_TILE = 1024
FFN_NORM_ROW_STRIP = 128
FFN_NORM_COL_TILE = 256
FFN_DOWN_ROW_TILE = 512
FFN_DOWN_K_TILE = 1408
FFN_DOWN_ROW_SPLIT = 2
ATTN_Q_TILE = 1024
ATTN_K_CHUNK = 2048
MOD_COL_TILE = 1024
HALO = BF16_SUBLANES


def _cparams(sem):
    return pltpu.CompilerParams(dimension_semantics=sem, vmem_limit_bytes=VMEM_LIMIT_BYTES)


def _dot(a, b):
    return jnp.dot(a, b, preferred_element_type=F32)


def _rms(y, n):
    return y * lax.rsqrt(jnp.sum(y * y, axis=-1, keepdims=True) * (1.0 / n) + EPS)


def _modulate(x, g, shift, scale):
    return (_rms(x, x.shape[-1]) * g) * (1.0 + scale) + shift


def _rope(x, cos, sin_a, sin_b, quarter):
    return (x * cos + pltpu.roll(x, LANES - quarter, 1) * sin_a
            + pltpu.roll(x, quarter, 1) * sin_b)


def _mod_kernel(c_ref, w_ref, b_ref, o_ref):
    c = c_ref[...]
    sc = (c * jax.nn.sigmoid(c)).astype(BF16)
    o_ref[0] = _dot(sc, w_ref[0].astype(BF16)) + b_ref[0]


def _mod_all(cvec, w_mod, b_mod):
    L, D, N = w_mod.shape
    R = cvec.shape[0]
    return pl.pallas_call(
        _mod_kernel,
        grid=(L, N // MOD_COL_TILE),
        in_specs=[
            pl.BlockSpec((R, D), lambda l, j: (0, 0)),
            pl.BlockSpec((1, D, MOD_COL_TILE), lambda l, j: (l, 0, j)),
            pl.BlockSpec((1, 1, MOD_COL_TILE), lambda l, j: (l, 0, j)),
        ],
        out_specs=pl.BlockSpec((1, R, MOD_COL_TILE), lambda l, j: (l, 0, j)),
        out_shape=jax.ShapeDtypeStruct((L, R, N), F32),
        compiler_params=_cparams(("parallel", "parallel")),
        name="mod_all",
    )(cvec, w_mod, b_mod.reshape(L, 1, N))


def _mla_qkv_kernel(x_ref, g_ref, sh_ref, sc_ref, w1_ref, g1_ref, wuq_ref, gq_ref,
                    wukv_ref, gk_ref, cos_ref, sa_ref, sb_ref,
                    q_ref, k_ref, vt_ref):
    h = _modulate(x_ref[0], g_ref[...], sh_ref[0], sc_ref[0]).astype(BF16)
    y1 = _dot(h, w1_ref[...])
    g1 = g1_ref[...]
    cq = (_rms(y1[:, :MLA_Q_RANK], MLA_Q_RANK) * g1[:, :MLA_Q_RANK]).astype(BF16)
    lo, hi = MLA_Q_RANK, MLA_Q_RANK + MLA_KV_RANK
    ckv = (_rms(y1[:, lo:hi], MLA_KV_RANK) * g1[:, lo:hi]).astype(BF16)
    cos, sa, sb = cos_ref[...], sa_ref[...], sb_ref[...]
    quarter = MLA_ROPE // 4
    kpe = _rms(y1[:, hi:], MLA_ROPE) * g1[:, hi:]
    kpe = _rope(kpe, cos, sa, sb, quarter).astype(BF16)
    gq = gq_ref[...]
    gk = gk_ref[...]
    qscale = MLA_SCALE * LOG2E

    hg = MLA_HEAD_GROUP
    gw = hg * MLA_QK_PAD

    def group_dots(gi):
        return (_dot(cq, wuq_ref[:, gi * gw:(gi + 1) * gw]),
                _dot(ckv, wukv_ref[:, gi * gw:(gi + 1) * gw]))

    nxt = group_dots(0)
    for gi in range(MLA_HEADS // hg):
        yq, ykv = nxt
        if gi + 1 < MLA_HEADS // hg:
            nxt = group_dots(gi + 1)
        offs = [j * MLA_QK_PAD for j in range(hg)]
        qn = [yq[:, o:o + MLA_NOPE] for o in offs]
        qp = [yq[:, o + MLA_NOPE:o + MLA_QK_PAD] for o in offs]
        kn = [ykv[:, o:o + MLA_NOPE] for o in offs]
        ssq = [jnp.sum(t * t, axis=-1, keepdims=True) for t in qn + qp + kn]
        widths = [MLA_NOPE] * hg + [MLA_ROPE] * hg + [MLA_NOPE] * hg
        inv = [lax.rsqrt(s * (1.0 / n) + EPS) for s, n in zip(ssq, widths)]
        qn = [t * r * (gq[:, :MLA_NOPE] * qscale) for t, r in zip(qn, inv[:hg])]
        qp = [t * r * (gq[:, MLA_NOPE:] * qscale) for t, r in zip(qp, inv[hg:2 * hg])]
        kn = [t * r * gk for t, r in zip(kn, inv[2 * hg:])]
        ra = [pltpu.roll(t, LANES - quarter, 1) for t in qp]
        rb = [pltpu.roll(t, quarter, 1) for t in qp]
        qp = [t * cos + a * sa + b * sb for t, a, b in zip(qp, ra, rb)]
        vts = [ykv[:, o + MLA_NOPE:o + MLA_QK_PAD].T for o in offs]
        for j, o in enumerate(offs):
            c0 = gi * gw + o
            q_ref[0, :, c0:c0 + MLA_NOPE] = qn[j].astype(BF16)
            q_ref[0, :, c0 + MLA_NOPE:c0 + MLA_QK_PAD] = qp[j].astype(BF16)
            k_ref[0, :, c0:c0 + MLA_NOPE] = kn[j].astype(BF16)
            k_ref[0, :, c0 + MLA_NOPE:c0 + MLA_QK_PAD] = kpe
            vt_ref[0, gi * hg + j] = vts[j].astype(BF16)


def _mla_qkv(x, g, shift, scale, w, rope, tm, rope_blk0):
    B, R, D = x.shape
    H = MLA_HEADS
    nq = H * MLA_QK_PAD
    const2 = lambda b, i: (0, 0)
    mod_map = (lambda b, i: (b, 0, 0)) if shift.shape[0] == B else (lambda b, i: (0, 0, 0))
    rope_spec = pl.BlockSpec((tm, LANES), lambda b, i: (rope_blk0 + i, 0))
    return pl.pallas_call(
        _mla_qkv_kernel,
        grid=(B, R // tm),
        in_specs=[
            pl.BlockSpec((1, tm, D), lambda b, i: (b, i, 0)),
            pl.BlockSpec((1, D), const2),
            pl.BlockSpec((1, 1, D), mod_map),
            pl.BlockSpec((1, 1, D), mod_map),
            pl.BlockSpec(w["w1"].shape, const2),
            pl.BlockSpec(w["g1"].shape, const2),
            pl.BlockSpec(w["wuq"].shape, const2),
            pl.BlockSpec(w["gq"].shape, const2),
            pl.BlockSpec(w["wukv"].shape, const2),
            pl.BlockSpec(w["gk"].shape, const2),
            rope_spec, rope_spec, rope_spec,
        ],
        out_specs=[
            pl.BlockSpec((1, tm, nq), lambda b, i: (b, i, 0)),
            pl.BlockSpec((1, tm, nq), lambda b, i: (b, i, 0)),
            pl.BlockSpec((1, H, MLA_V, tm), lambda b, i: (b, 0, 0, i)),
        ],
        out_shape=[
            jax.ShapeDtypeStruct((B, R, nq), BF16),
            jax.ShapeDtypeStruct((B, R, nq), BF16),
            jax.ShapeDtypeStruct((B, H, MLA_V, R), BF16),
        ],
        compiler_params=_cparams(("parallel", "parallel")),
        name="mla_qkv",
    )(x, g, shift, scale, w["w1"], w["g1"], w["wuq"], w["gq"], w["wukv"], w["gk"], *rope)


def _gqa_qkv_kernel(x_ref, g_ref, sh_ref, sc_ref, w_ref, gq_ref, gk_ref,
                    cos_ref, sa_ref, sb_ref, q_ref, k_ref, vt_ref):
    h = _modulate(x_ref[0], g_ref[...], sh_ref[0], sc_ref[0]).astype(BF16)
    cos, sa, sb = cos_ref[...], sa_ref[...], sb_ref[...]
    quarter = GQA_HEAD_DIM // 4
    hd = GQA_HEAD_DIM
    group = GQA_KV_HEADS * hd
    nq = GQA_HEADS * hd
    gq = gq_ref[...] * (GQA_SCALE * LOG2E)
    gk = gk_ref[...]
    n_groups = (nq + 2 * group) // group
    heads = range(0, group, hd)

    def norm_rope_group(y, gain):
        tiles = [y[:, j:j + hd] for j in heads]
        ssq = [jnp.sum(t * t, axis=-1, keepdims=True) for t in tiles]
        inv = [lax.rsqrt(s * (1.0 / hd) + EPS) for s in ssq]
        xs = [t * r * gain for t, r in zip(tiles, inv)]
        ra = [pltpu.roll(x, LANES - quarter, 1) for x in xs]
        rb = [pltpu.roll(x, quarter, 1) for x in xs]
        return [(x * cos + a * sa + b * sb).astype(BF16) for x, a, b in zip(xs, ra, rb)]

    y_next = _dot(h, w_ref[:, 0:group])
    for gi in range(n_groups):
        c0 = gi * group
        y = y_next
        if gi + 1 < n_groups:
            y_next = _dot(h, w_ref[:, c0 + group:c0 + 2 * group])
        if c0 < nq:
            for j, qh in zip(heads, norm_rope_group(y, gq)):
                q_ref[0, :, c0 + j:c0 + j + hd] = qh
        elif c0 == nq:
            for j, kh in zip(heads, norm_rope_group(y, gk)):
                k_ref[0, :, j:j + hd] = kh
        else:
            vts = [y[:, j:j + hd].T.astype(BF16) for j in heads]
            for j, v in zip(heads, vts):
                vt_ref[0, j // hd] = v


def _gqa_qkv(x, g, shift, scale, w, rope, tm, rope_blk0):
    B, R, D = x.shape
    nq = GQA_HEADS * GQA_HEAD_DIM
    nk = GQA_KV_HEADS * GQA_HEAD_DIM
    const2 = lambda b, i: (0, 0)
    mod_map = (lambda b, i: (b, 0, 0)) if shift.shape[0] == B else (lambda b, i: (0, 0, 0))
    rope_spec = pl.BlockSpec((tm, LANES), lambda b, i: (rope_blk0 + i, 0))
    return pl.pallas_call(
        _gqa_qkv_kernel,
        grid=(B, R // tm),
        in_specs=[
            pl.BlockSpec((1, tm, D), lambda b, i: (b, i, 0)),
            pl.BlockSpec((1, D), const2),
            pl.BlockSpec((1, 1, D), mod_map),
            pl.BlockSpec((1, 1, D), mod_map),
            pl.BlockSpec(w["w"].shape, const2),
            pl.BlockSpec(w["gq"].shape, const2),
            pl.BlockSpec(w["gk"].shape, const2),
            rope_spec, rope_spec, rope_spec,
        ],
        out_specs=[
            pl.BlockSpec((1, tm, nq), lambda b, i: (b, i, 0)),
            pl.BlockSpec((1, tm, nk), lambda b, i: (b, i, 0)),
            pl.BlockSpec((1, GQA_KV_HEADS, GQA_HEAD_DIM, tm), lambda b, i: (b, 0, 0, i)),
        ],
        out_shape=[
            jax.ShapeDtypeStruct((B, R, nq), BF16),
            jax.ShapeDtypeStruct((B, R, nk), BF16),
            jax.ShapeDtypeStruct((B, GQA_KV_HEADS, GQA_HEAD_DIM, R), BF16),
        ],
        compiler_params=_cparams(("parallel", "parallel")),
        name="gqa_qkv",
    )(x, g, shift, scale, w["w"], w["gq"], w["gk"], *rope)


def _attn_kernel(bound_ref, *refs, chunks, n_src, group):
    q_ref = refs[0]
    kv_refs = refs[1:1 + 2 * n_src]
    o_ref = refs[1 + 2 * n_src]
    s_buf, p_buf, acc_ref = refs[2 + 2 * n_src:]
    n = len(chunks)
    bound = bound_ref[pl.program_id(0), pl.program_id(1) * group + pl.program_id(2)]

    def qk(c):
        src, start, size = chunks[c]
        k_c = kv_refs[2 * src][0, start:start + size, :]
        return lax.dot_general(k_c, q_ref[0], (((1,), (1,)), ((), ())),
                               preferred_element_type=F32)

    def vt(c):
        src, start, size = chunks[c]
        return kv_refs[2 * src + 1][0, 0, :, start:start + size]

    def bounded():
        acc = l = p_prev = None
        s_next = qk(0)
        for c in range(n):
            s = s_next
            if c + 1 < n:
                s_next = qk(c + 1)
            p = jnp.exp2(s - bound)
            psum = jnp.sum(p, axis=0, keepdims=True)
            l = psum if c == 0 else l + psum
            if c >= 1:
                y = _dot(vt(c - 1), p_prev)
                acc = y if c == 1 else acc + y
            p_prev = p.astype(BF16)
        y = _dot(vt(n - 1), p_prev)
        acc = y if n == 1 else acc + y
        o_ref[0] = (acc / l).T.astype(o_ref.dtype)

    def online():
        def qk_store(c):
            s = qk(c)
            s_buf[c % 2, 0:chunks[c][2], :] = s
            return jnp.max(s, axis=0, keepdims=True)

        def pv(c, alpha):
            y = _dot(vt(c), p_buf[c % 2, 0:chunks[c][2], :])
            if c == 0:
                acc_ref[...] = y
            else:
                acc_ref[...] = alpha * acc_ref[...] + y

        m_next = qk_store(0)
        m = l = alpha_prev = None
        for c in range(n):
            size = chunks[c][2]
            m_c = m_next
            if c + 1 < n:
                m_next = qk_store(c + 1)
            s = s_buf[c % 2, 0:size, :]
            if c == 0:
                m_new, alpha = m_c, None
            else:
                m_new = jnp.maximum(m, m_c)
                alpha = jnp.exp2(m - m_new)
            p = jnp.exp2(s - m_new)
            p_buf[c % 2, 0:size, :] = p.astype(BF16)
            psum = jnp.sum(p, axis=0, keepdims=True)
            l = psum if c == 0 else alpha * l + psum
            m = m_new
            if c >= 1:
                pv(c - 1, alpha_prev)
            alpha_prev = alpha
        pv(n - 1, alpha_prev)
        o_ref[0] = (acc_ref[...] / l).T.astype(o_ref.dtype)

    lax.cond(bound < SAFE_LOG2_SPAN, bounded, online)


def _attention(q, kv_sources, bound, n_heads, n_kv_heads, dq, dv, tq):
    B, Q, _ = q.shape
    G = n_heads // n_kv_heads
    chunks = []
    in_specs = [pl.BlockSpec((1, tq, dq), lambda b, hk, g, i, bnd: (b, i, hk * G + g))]
    args = [q]
    for src, (k, vt) in enumerate(kv_sources):
        T = k.shape[1]
        ck = min(ATTN_K_CHUNK, T)
        chunks += [(src, s, ck) for s in range(0, T, ck)]
        in_specs += [
            pl.BlockSpec((1, T, dq), lambda b, hk, g, i, bnd: (b, 0, hk)),
            pl.BlockSpec((1, 1, dv, T), lambda b, hk, g, i, bnd: (b, hk, 0, 0)),
        ]
        args += [k, vt]
    ck_max = max(size for _, _, size in chunks)
    return pl.pallas_call(
        functools.partial(_attn_kernel, chunks=tuple(chunks), n_src=len(kv_sources), group=G),
        grid_spec=pltpu.PrefetchScalarGridSpec(
            num_scalar_prefetch=1,
            grid=(B, n_kv_heads, G, Q // tq),
            in_specs=in_specs,
            out_specs=pl.BlockSpec((1, tq, dv), lambda b, hk, g, i, bnd: (b, i, hk * G + g)),
            scratch_shapes=[
                pltpu.VMEM((2, ck_max, tq), F32),
                pltpu.VMEM((2, ck_max, tq), BF16),
                pltpu.VMEM((dv, tq), F32),
            ],
        ),
        out_shape=jax.ShapeDtypeStruct((B, Q, n_heads * dv), BF16),
        compiler_params=_cparams(("parallel", "parallel", "parallel", "parallel")),
        name="attention",
    )(bound, *args)


def _score_bound(parts_q, parts_k, qscale, batch, n_heads):
    q2 = sum(n * jnp.max(g * g) for n, g in parts_q) * (qscale * qscale)
    k2 = sum(n * jnp.max(g * g) for n, g in parts_k)
    return jnp.full((batch, n_heads), jnp.sqrt(q2 * k2) * BOUND_MARGIN, F32)


def _oproj_kernel(o_ref, w_ref, x_ref, gate_ref, out_ref, *, col_tile):
    o = o_ref[0]
    gate = gate_ref[0]
    for c0 in range(0, out_ref.shape[-1], col_tile):
        y = _dot(o, w_ref[:, c0:c0 + col_tile])
        out_ref[0, :, c0:c0 + col_tile] = x_ref[0, :, c0:c0 + col_tile] + gate[:, c0:c0 + col_tile] * y


def _oproj(o, w_o, x, gate, tm):
    B, R, D = x.shape
    K = o.shape[-1]
    mod_map = (lambda b, i: (b, 0, 0)) if gate.shape[0] == B else (lambda b, i: (0, 0, 0))
    return pl.pallas_call(
        functools.partial(_oproj_kernel, col_tile=512),
        grid=(B, R // tm),
        in_specs=[
            pl.BlockSpec((1, tm, K), lambda b, i: (b, i, 0)),
            pl.BlockSpec((K, D), lambda b, i: (0, 0)),
            pl.BlockSpec((1, tm, D), lambda b, i: (b, i, 0)),
            pl.BlockSpec((1, 1, D), mod_map),
        ],
        out_specs=pl.BlockSpec((1, tm, D), lambda b, i: (b, i, 0)),
        out_shape=jax.ShapeDtypeStruct((B, R, D), F32),
        compiler_params=_cparams(("parallel", "parallel")),
        name="oproj",
    )(o, w_o, x, gate)


def _ffn_up_kernel(x_ref, g_ref, sh_ref, sc_ref, w_ref, u_ref, h_ref):
    @pl.when(pl.program_id(2) == 0)
    def _():
        tm, d = h_ref.shape
        inv = []
        for r0 in range(0, tm, FFN_NORM_ROW_STRIP):
            xs = x_ref[0, r0:r0 + FFN_NORM_ROW_STRIP, :]
            inv.append(lax.rsqrt(jnp.sum(xs * xs, axis=-1, keepdims=True) * (1.0 / d) + EPS))
        inv = jnp.concatenate(inv, axis=0)
        for c0 in range(0, d, FFN_NORM_COL_TILE):
            c1 = c0 + FFN_NORM_COL_TILE
            y = (x_ref[0, :, c0:c1] * inv) * g_ref[:, c0:c1]
            h_ref[:, c0:c1] = (y * (1.0 + sc_ref[0, :, c0:c1]) + sh_ref[0, :, c0:c1]).astype(BF16)

    u_ref[0] = _dot(h_ref[...], w_ref[...]).astype(u_ref.dtype)


def _ffn_up(x, g, shift, scale, w_up, tm, tn):
    B, R, D = x.shape
    N = w_up.shape[1]
    mod_map = (lambda b, i, j: (b, 0, 0)) if shift.shape[0] == B else (lambda b, i, j: (0, 0, 0))
    return pl.pallas_call(
        _ffn_up_kernel,
        grid=(B, R // tm, N // tn),
        in_specs=[
            pl.BlockSpec((1, tm, D), lambda b, i, j: (b, i, 0)),
            pl.BlockSpec((1, D), lambda b, i, j: (0, 0)),
            pl.BlockSpec((1, 1, D), mod_map),
            pl.BlockSpec((1, 1, D), mod_map),
            pl.BlockSpec((D, tn), lambda b, i, j: (0, j)),
        ],
        out_specs=pl.BlockSpec((1, tm, tn), lambda b, i, j: (b, i, j)),
        out_shape=jax.ShapeDtypeStruct((B, R, N), BF16),
        scratch_shapes=[pltpu.VMEM((tm, D), BF16)],
        compiler_params=_cparams(("parallel", "parallel", "arbitrary")),
        name="ffn_up",
    )(x, g, shift, scale, w_up)


def _ffn_down_kernel(ug_ref, uv_ref, prev_ref, next_ref, cw_ref, cb_ref, w_ref, x_ref, gate_ref,
                     out_ref):
    i = pl.program_id(1)
    kk = pl.program_id(2)
    tm = ug_ref.shape[1]
    first = i == 0
    last = i == pl.num_programs(1) - 1
    prev_row = jnp.where(first, 0.0, prev_ref[0, HALO - 1:HALO, :].astype(F32))
    next_row = jnp.where(last, 0.0, next_ref[0, 0:1, :].astype(F32))
    @pl.when(kk == 0)
    def _():
        out_ref[0] = jnp.zeros(out_ref.shape[1:], out_ref.dtype)

    cw = cw_ref[...]
    cb = cb_ref[...]
    rs = tm // FFN_DOWN_ROW_SPLIT
    for r0 in range(0, tm, rs):
        r1 = r0 + rs
        g = ug_ref[0, r0:r1, :].astype(F32)
        above = prev_row if r0 == 0 else ug_ref[0, r0 - 1:r0, :].astype(F32)
        below = next_row if r1 == tm else ug_ref[0, r1:r1 + 1, :].astype(F32)
        row = lax.broadcasted_iota(jnp.int32, g.shape, 0)
        g_prev = jnp.where(row == 0, above, pltpu.roll(g, 1, 0))
        g_next = jnp.where(row == rs - 1, below, pltpu.roll(g, rs - 1, 0))
        conv = g_prev * cw[0:1, :] + g * cw[1:2, :] + g_next * cw[2:3, :] + cb
        half = 0.5 * conv
        a = (half * (1.0 + jnp.tanh(half))) * uv_ref[0, r0:r1, :].astype(F32)
        out_ref[0, r0:r1, :] += _dot(a.astype(BF16), w_ref[...])

    @pl.when(kk == pl.num_programs(2) - 1)
    def _():
        out_ref[0] = x_ref[0] + gate_ref[0] * out_ref[0]


def _ffn_down(u, conv_w, conv_b, w_down, x, gate, tm, tk):
    B, R, D = x.shape
    F = w_down.shape[0]
    nk = F // tk
    rb = tm // HALO
    n_halo_blocks = R // HALO
    mod_map = (lambda b, i, k: (b, 0, 0)) if gate.shape[0] == B else (lambda b, i, k: (0, 0, 0))
    return pl.pallas_call(
        _ffn_down_kernel,
        grid=(B, R // tm, nk),
        in_specs=[
            pl.BlockSpec((1, tm, tk), lambda b, i, k: (b, i, k)),
            pl.BlockSpec((1, tm, tk), lambda b, i, k: (b, i, nk + k)),
            pl.BlockSpec((1, HALO, tk), lambda b, i, k: (b, jnp.maximum(i * rb - 1, 0), k)),
            pl.BlockSpec((1, HALO, tk),
                         lambda b, i, k: (b, jnp.minimum((i + 1) * rb, n_halo_blocks - 1), k)),
            pl.BlockSpec((CONV_W, tk), lambda b, i, k: (0, k)),
            pl.BlockSpec((1, tk), lambda b, i, k: (0, k)),
            pl.BlockSpec((tk, D), lambda b, i, k: (k, 0)),
            pl.BlockSpec((1, tm, D), lambda b, i, k: (b, i, 0)),
            pl.BlockSpec((1, 1, D), mod_map),
        ],
        out_specs=pl.BlockSpec((1, tm, D), lambda b, i, k: (b, i, 0)),
        out_shape=jax.ShapeDtypeStruct((B, R, D), F32),
        compiler_params=_cparams(("parallel", "parallel", "arbitrary")),
        name="ffn_down",
    )(u, u, u, u, conv_w, conv_b.reshape(1, F), w_down, x, gate)


def _rope_tables(seq, ctx_len, rot_dim):
    t = jnp.arange(seq, dtype=jnp.int32)
    rows = (t // GRID_W).astype(F32)
    cols = (t % GRID_W).astype(F32)
    axis_dim = rot_dim // 2
    inv = jnp.power(ROPE_BASE, -jnp.arange(0, axis_dim, 2, dtype=F32) / axis_dim)
    ang_r = rows[:, None] * inv
    ang_c = cols[:, None] * inv
    ang = jnp.concatenate([ang_r, ang_r, ang_c, ang_c], axis=-1)
    cos, sin = jnp.cos(ang), jnp.sin(ang)
    lane = jnp.arange(rot_dim)
    first_quarter = (lane % axis_dim) < (axis_dim // 2)
    sin_a = jnp.where(first_quarter, -sin, 0.0)
    sin_b = jnp.where(first_quarter, 0.0, sin)

    def finish(tab, ctx_value):
        tab = jnp.concatenate([tab, jnp.full((ctx_len, rot_dim), ctx_value, F32)], axis=0)
        return jnp.pad(tab, ((0, 0), (0, LANES - rot_dim)))

    return finish(cos, 1.0), finish(sin_a, 0.0), finish(sin_b, 0.0)


def _mla_weights(j, w_dq, g_dq, w_uq, g_q_nope, g_q_pe, w_dkv, g_dkv, g_k_pe, w_ukv, g_k_nope):
    D = w_dq.shape[1]
    pad = LANES - MLA_ROPE
    w1 = jnp.concatenate([w_dq[j], w_dkv[j], jnp.zeros((D, pad), F32)], axis=1).astype(BF16)
    g1 = jnp.concatenate([g_dq[j], g_dkv[j], g_k_pe[j], jnp.zeros((pad,), F32)])[None, :]
    wuq = w_uq[j].reshape(MLA_Q_RANK, MLA_HEADS, MLA_NOPE + MLA_ROPE)
    wuq = jnp.pad(wuq, ((0, 0), (0, 0), (0, MLA_QK_PAD - MLA_NOPE - MLA_ROPE)))
    wuq = wuq.reshape(MLA_Q_RANK, MLA_HEADS * MLA_QK_PAD).astype(BF16)
    gq = jnp.concatenate([g_q_nope[j], g_q_pe[j], jnp.zeros((pad,), F32)])[None, :]
    return dict(w1=w1, g1=g1, wuq=wuq, gq=gq, wukv=w_ukv[j].astype(BF16), gk=g_k_nope[j][None, :])


def _gqa_weights(j, w_q, g_q, w_kv, g_k):
    w = jnp.concatenate([w_q[j], w_kv[j]], axis=1).astype(BF16)
    return dict(w=w, gq=g_q[j][None, :], gk=g_k[j][None, :])


def kernel(x, c, ctx, c_ctx, w_mod, b_mod, norm_mix, norm_ffn, mla_w_dq, mla_g_dq, mla_w_uq, mla_g_q_nope, mla_g_q_pe, mla_w_dkv, mla_g_dkv, mla_g_k_pe, mla_w_ukv, mla_g_k_nope, mla_w_o, gqa_w_q, gqa_g_q, gqa_w_kv, gqa_g_k, gqa_w_o, ffn_w_up, ffn_conv_w, ffn_conv_b, ffn_w_down):
    B, S, D = x.shape
    C = ctx.shape[1]
    depth = w_mod.shape[0]
    assert S % FFN_UP_ROW_TILE == 0 and S % ROW_TILE == 0 and C % HALO == 0 and C <= ROW_TILE

    cvec = jnp.concatenate([c, c_ctx[None, :], jnp.zeros((8 - B - 1, D), F32)], axis=0)
    mod = _mod_all(cvec, w_mod, b_mod).reshape(depth, 8, N_MOD, D)
    rope_mla = _rope_tables(S, C, MLA_ROPE)
    rope_gqa = _rope_tables(S, C, GQA_HEAD_DIM)

    for i in range(depth):
        last = i == depth - 1
        j = i // 2
        lat = [mod[i, :B, n][:, None, :] for n in range(N_MOD)]
        cmod = [mod[i, B:B + 1, n][:, None, :] for n in range(N_MOD)]
        g_mix = norm_mix[i][None, :]
        g_ffn = norm_ffn[i][None, :]

        if i % 2 == 0:
            w = _mla_weights(j, mla_w_dq, mla_g_dq, mla_w_uq, mla_g_q_nope, mla_g_q_pe,
                             mla_w_dkv, mla_g_dkv, mla_g_k_pe, mla_w_ukv, mla_g_k_nope)
            qkv, rope = _mla_qkv, rope_mla
            bound = _score_bound([(MLA_NOPE, mla_g_q_nope[j]), (MLA_ROPE, mla_g_q_pe[j])],
                                 [(MLA_NOPE, mla_g_k_nope[j]), (MLA_ROPE, mla_g_k_pe[j])],
                                 MLA_SCALE * LOG2E, B, MLA_HEADS)
            w_o = mla_w_o[j].astype(BF16)
            heads, kv_heads, dq, dv = MLA_HEADS, MLA_HEADS, MLA_QK_PAD, MLA_V
        else:
            w = _gqa_weights(j, gqa_w_q, gqa_g_q, gqa_w_kv, gqa_g_k)
            qkv, rope = _gqa_qkv, rope_gqa
            bound = _score_bound([(GQA_HEAD_DIM, gqa_g_q[j])], [(GQA_HEAD_DIM, gqa_g_k[j])],
                                 GQA_SCALE * LOG2E, B, GQA_HEADS)
            w_o = gqa_w_o[j].astype(BF16)
            heads, kv_heads, dq, dv = GQA_HEADS, GQA_KV_HEADS, GQA_HEAD_DIM, GQA_HEAD_DIM

        q, k, vt = qkv(x, g_mix, lat[0], lat[1], w, rope, ROW_TILE, 0)
        qc, kc, vtc = qkv(ctx, g_mix, cmod[0], cmod[1], w, rope, C, S // C)
        o = _attention(q, [(k, vt), (kc, vtc)], bound, heads, kv_heads, dq, dv, ATTN_Q_TILE)
        x = _oproj(o, w_o, x, lat[2], ROW_TILE)
        if not last:
            oc = _attention(qc, [(kc, vtc)], bound, heads, kv_heads, dq, dv, C)
            ctx = _oproj(oc, w_o, ctx, cmod[2], C)

        w_up = ffn_w_up[i].astype(BF16)
        w_down = ffn_w_down[i].astype(BF16)
        u = _ffn_up(x, g_ffn, lat[3], lat[4], w_up, FFN_UP_ROW_TILE, FFN_UP_COL_TILE)
        x = _ffn_down(u, ffn_conv_w[i], ffn_conv_b[i], w_down, x, lat[5],
                      FFN_DOWN_ROW_TILE, FFN_DOWN_K_TILE)
        if not last:
            uc = _ffn_up(ctx, g_ffn, cmod[3], cmod[4], w_up, C, FFN_UP_COL_TILE)
            ctx = _ffn_down(uc, ffn_conv_w[i], ffn_conv_b[i], w_down, ctx, cmod[5],
                            C, FFN_DOWN_K_TILE)
    return x
```

```python
import functools
import math

import jax
import jax.numpy as jnp
from jax import lax
from jax.experimental import pallas as pl
from jax.experimental.pallas import tpu as pltpu

F32 = jnp.float32
BF16 = jnp.bfloat16

GRID_W = 64
N_MOD = 6
MLA_HEADS = 16
MLA_Q_RANK = 512
MLA_KV_RANK = 512
MLA_NOPE = 128
MLA_ROPE = 64
MLA_V = 128
MLA_QK_PAD = 256
GQA_HEADS = 16
GQA_KV_HEADS = 4
GQA_HEAD_DIM = 128
CONV_W = 3
ROPE_BASE = 10000.0
EPS = 1e-6
LOG2E = 1.4426950408889634
MLA_SCALE = 1.0 / math.sqrt(MLA_NOPE + MLA_ROPE)
GQA_SCALE = 1.0 / math.sqrt(GQA_HEAD_DIM)

SAFE_LOG2_SPAN = 60.0
BOUND_MARGIN = 1.02

LANES = 128
BF16_SUBLANES = 16
VMEM_LIMIT_BYTES = 56 * 1024 * 1024

ROW_TILE = 512
MLA_HEAD_GROUP = 4
FFN_UP_ROW_TILE = 1024
FFN_UP_COL_TILE = 1024
FFN_NORM_ROW_STRIP = 128
FFN_NORM_COL_TILE = 256
FFN_DOWN_ROW_TILE = 512
FFN_DOWN_K_TILE = 1408
FFN_DOWN_ROW_SPLIT = 2
ATTN_Q_TILE = 1024
ATTN_K_CHUNK = 2048
MOD_COL_TILE = 1024
HALO = BF16_SUBLANES


def _cparams(sem):
    return pltpu.CompilerParams(dimension_semantics=sem, vmem_limit_bytes=VMEM_LIMIT_BYTES)


def _dot(a, b):
    return jnp.dot(a, b, preferred_element_type=F32)


def _rms(y, n):
    return y * lax.rsqrt(jnp.sum(y * y, axis=-1, keepdims=True) * (1.0 / n) + EPS)


def _modulate(x, g, shift, scale):
    return (_rms(x, x.shape[-1]) * g) * (1.0 + scale) + shift


def _rope(x, cos, sin_a, sin_b, quarter):
    return (x * cos + pltpu.roll(x, LANES - quarter, 1) * sin_a
            + pltpu.roll(x, quarter, 1) * sin_b)


def _mod_kernel(c_ref, w_ref, b_ref, o_ref):
    c = c_ref[...]
    sc = (c * jax.nn.sigmoid(c)).astype(BF16)
    o_ref[0] = _dot(sc, w_ref[0].astype(BF16)) + b_ref[0]


def _mod_all(cvec, w_mod, b_mod):
    L, D, N = w_mod.shape
    R = cvec.shape[0]
    return pl.pallas_call(
        _mod_kernel,
        grid=(L, N // MOD_COL_TILE),
        in_specs=[
            pl.BlockSpec((R, D), lambda l, j: (0, 0)),
            pl.BlockSpec((1, D, MOD_COL_TILE), lambda l, j: (l, 0, j)),
            pl.BlockSpec((1, 1, MOD_COL_TILE), lambda l, j: (l, 0, j)),
        ],
        out_specs=pl.BlockSpec((1, R, MOD_COL_TILE), lambda l, j: (l, 0, j)),
        out_shape=jax.ShapeDtypeStruct((L, R, N), F32),
        compiler_params=_cparams(("parallel", "parallel")),
        name="mod_all",
    )(cvec, w_mod, b_mod.reshape(L, 1, N))


def _mla_qkv_kernel(x_ref, g_ref, sh_ref, sc_ref, w1_ref, g1_ref, wuq_ref, gq_ref,
                    wuk_ref, wuvt_ref, gk_ref, cos_ref, sa_ref, sb_ref,
                    q_ref, k_ref, vt_ref):
    h = _modulate(x_ref[0], g_ref[...], sh_ref[0], sc_ref[0]).astype(BF16)
    y1 = _dot(h, w1_ref[...])
    g1 = g1_ref[...]
    cq = (_rms(y1[:, :MLA_Q_RANK], MLA_Q_RANK) * g1[:, :MLA_Q_RANK]).astype(BF16)
    lo, hi = MLA_Q_RANK, MLA_Q_RANK + MLA_KV_RANK
    ckv = (_rms(y1[:, lo:hi], MLA_KV_RANK) * g1[:, lo:hi]).astype(BF16)
    cos, sa, sb = cos_ref[...], sa_ref[...], sb_ref[...]
    quarter = MLA_ROPE // 4
    kpe = _rms(y1[:, hi:], MLA_ROPE) * g1[:, hi:]
    kpe = _rope(kpe, cos, sa, sb, quarter).astype(BF16)
    gq = gq_ref[...]
    gk = gk_ref[...]
    qscale = MLA_SCALE * LOG2E

    hg = MLA_HEAD_GROUP
    gw = hg * MLA_QK_PAD

    kw = hg * MLA_NOPE

    def group_dots(gi):
        return (_dot(cq, wuq_ref[:, gi * gw:(gi + 1) * gw]),
                _dot(ckv, wuk_ref[:, gi * kw:(gi + 1) * kw]),
                lax.dot_general(wuvt_ref[gi * kw:(gi + 1) * kw, :], ckv, (((1,), (1,)), ((), ())),
                                preferred_element_type=F32))

    nxt = group_dots(0)
    for gi in range(MLA_HEADS // hg):
        yq, yk, yvt = nxt
        if gi + 1 < MLA_HEADS // hg:
            nxt = group_dots(gi + 1)
        offs = [j * MLA_QK_PAD for j in range(hg)]
        qn = [yq[:, o:o + MLA_NOPE] for o in offs]
        qp = [yq[:, o + MLA_NOPE:o + MLA_QK_PAD] for o in offs]
        kn = [yk[:, j * MLA_NOPE:(j + 1) * MLA_NOPE] for j in range(hg)]
        ssq = [jnp.sum(t * t, axis=-1, keepdims=True) for t in qn + qp + kn]
        widths = [MLA_NOPE] * hg + [MLA_ROPE] * hg + [MLA_NOPE] * hg
        inv = [lax.rsqrt(s * (1.0 / n) + EPS) for s, n in zip(ssq, widths)]
        qn = [t * r * (gq[:, :MLA_NOPE] * qscale) for t, r in zip(qn, inv[:hg])]
        qp = [t * r * (gq[:, MLA_NOPE:] * qscale) for t, r in zip(qp, inv[hg:2 * hg])]
        kn = [t * r * gk for t, r in zip(kn, inv[2 * hg:])]
        ra = [pltpu.roll(t, LANES - quarter, 1) for t in qp]
        rb = [pltpu.roll(t, quarter, 1) for t in qp]
        qp = [t * cos + a * sa + b * sb for t, a, b in zip(qp, ra, rb)]
        for j, o in enumerate(offs):
            c0 = gi * gw + o
            q_ref[0, :, c0:c0 + MLA_NOPE] = qn[j].astype(BF16)
            q_ref[0, :, c0 + MLA_NOPE:c0 + MLA_QK_PAD] = qp[j].astype(BF16)
            k_ref[0, :, c0:c0 + MLA_NOPE] = kn[j].astype(BF16)
            k_ref[0, :, c0 + MLA_NOPE:c0 + MLA_QK_PAD] = kpe
            vt_ref[0, gi * hg + j] = yvt[j * MLA_V:(j + 1) * MLA_V, :].astype(BF16)


def _mla_qkv(x, g, shift, scale, w, rope, tm, rope_blk0):
    B, R, D = x.shape
    H = MLA_HEADS
    nq = H * MLA_QK_PAD
    const2 = lambda b, i: (0, 0)
    mod_map = (lambda b, i: (b, 0, 0)) if shift.shape[0] == B else (lambda b, i: (0, 0, 0))
    rope_spec = pl.BlockSpec((tm, LANES), lambda b, i: (rope_blk0 + i, 0))
    return pl.pallas_call(
        _mla_qkv_kernel,
        grid=(B, R // tm),
        in_specs=[
            pl.BlockSpec((1, tm, D), lambda b, i: (b, i, 0)),
            pl.BlockSpec((1, D), const2),
            pl.BlockSpec((1, 1, D), mod_map),
            pl.BlockSpec((1, 1, D), mod_map),
            pl.BlockSpec(w["w1"].shape, const2),
            pl.BlockSpec(w["g1"].shape, const2),
            pl.BlockSpec(w["wuq"].shape, const2),
            pl.BlockSpec(w["gq"].shape, const2),
            pl.BlockSpec(w["wuk"].shape, const2),
            pl.BlockSpec(w["wuvt"].shape, const2),
            pl.BlockSpec(w["gk"].shape, const2),
            rope_spec, rope_spec, rope_spec,
        ],
        out_specs=[
            pl.BlockSpec((1, tm, nq), lambda b, i: (b, i, 0)),
            pl.BlockSpec((1, tm, nq), lambda b, i: (b, i, 0)),
            pl.BlockSpec((1, H, MLA_V, tm), lambda b, i: (b, 0, 0, i)),
        ],
        out_shape=[
            jax.ShapeDtypeStruct((B, R, nq), BF16),
            jax.ShapeDtypeStruct((B, R, nq), BF16),
            jax.ShapeDtypeStruct((B, H, MLA_V, R), BF16),
        ],
        compiler_params=_cparams(("parallel", "parallel")),
        name="mla_qkv",
    )(x, g, shift, scale, w["w1"], w["g1"], w["wuq"], w["gq"], w["wuk"], w["wuvt"], w["gk"], *rope)


def _gqa_qkv_kernel(x_ref, g_ref, sh_ref, sc_ref, w_ref, gq_ref, gk_ref,
                    cos_ref, sa_ref, sb_ref, q_ref, k_ref, vt_ref):
    h = _modulate(x_ref[0], g_ref[...], sh_ref[0], sc_ref[0]).astype(BF16)
    cos, sa, sb = cos_ref[...], sa_ref[...], sb_ref[...]
    quarter = GQA_HEAD_DIM // 4
    hd = GQA_HEAD_DIM
    group = GQA_KV_HEADS * hd
    nq = GQA_HEADS * hd
    gq = gq_ref[...] * (GQA_SCALE * LOG2E)
    gk = gk_ref[...]
    n_groups = (nq + 2 * group) // group
    heads = range(0, group, hd)

    def norm_rope_group(y, gain):
        tiles = [y[:, j:j + hd] for j in heads]
        ssq = [jnp.sum(t * t, axis=-1, keepdims=True) for t in tiles]
        inv = [lax.rsqrt(s * (1.0 / hd) + EPS) for s in ssq]
        xs = [t * r * gain for t, r in zip(tiles, inv)]
        ra = [pltpu.roll(x, LANES - quarter, 1) for x in xs]
        rb = [pltpu.roll(x, quarter, 1) for x in xs]
        return [(x * cos + a * sa + b * sb).astype(BF16) for x, a, b in zip(xs, ra, rb)]

    y_next = _dot(h, w_ref[:, 0:group])
    for gi in range(n_groups):
        c0 = gi * group
        y = y_next
        if gi + 1 < n_groups:
            y_next = _dot(h, w_ref[:, c0 + group:c0 + 2 * group])
        if c0 < nq:
            for j, qh in zip(heads, norm_rope_group(y, gq)):
                q_ref[0, :, c0 + j:c0 + j + hd] = qh
        elif c0 == nq:
            for j, kh in zip(heads, norm_rope_group(y, gk)):
                k_ref[0, :, j:j + hd] = kh
        else:
            vts = [y[:, j:j + hd].T.astype(BF16) for j in heads]
            for j, v in zip(heads, vts):
                vt_ref[0, j // hd] = v


def _gqa_qkv(x, g, shift, scale, w, rope, tm, rope_blk0):
    B, R, D = x.shape
    nq = GQA_HEADS * GQA_HEAD_DIM
    nk = GQA_KV_HEADS * GQA_HEAD_DIM
    const2 = lambda b, i: (0, 0)
    mod_map = (lambda b, i: (b, 0, 0)) if shift.shape[0] == B else (lambda b, i: (0, 0, 0))
    rope_spec = pl.BlockSpec((tm, LANES), lambda b, i: (rope_blk0 + i, 0))
    return pl.pallas_call(
        _gqa_qkv_kernel,
        grid=(B, R // tm),
        in_specs=[
            pl.BlockSpec((1, tm, D), lambda b, i: (b, i, 0)),
            pl.BlockSpec((1, D), const2),
            pl.BlockSpec((1, 1, D), mod_map),
            pl.BlockSpec((1, 1, D), mod_map),
            pl.BlockSpec(w["w"].shape, const2),
            pl.BlockSpec(w["gq"].shape, const2),
            pl.BlockSpec(w["gk"].shape, const2),
            rope_spec, rope_spec, rope_spec,
        ],
        out_specs=[
            pl.BlockSpec((1, tm, nq), lambda b, i: (b, i, 0)),
            pl.BlockSpec((1, tm, nk), lambda b, i: (b, i, 0)),
            pl.BlockSpec((1, GQA_KV_HEADS, GQA_HEAD_DIM, tm), lambda b, i: (b, 0, 0, i)),
        ],
        out_shape=[
            jax.ShapeDtypeStruct((B, R, nq), BF16),
            jax.ShapeDtypeStruct((B, R, nk), BF16),
            jax.ShapeDtypeStruct((B, GQA_KV_HEADS, GQA_HEAD_DIM, R), BF16),
        ],
        compiler_params=_cparams(("parallel", "parallel")),
        name="gqa_qkv",
    )(x, g, shift, scale, w["w"], w["gq"], w["gk"], *rope)


def _attn_kernel(bound_ref, *refs, chunks, n_src, group):
    q_ref = refs[0]
    kv_refs = refs[1:1 + 2 * n_src]
    o_ref = refs[1 + 2 * n_src]
    s_buf, p_buf, acc_ref = refs[2 + 2 * n_src:]
    n = len(chunks)
    bound = bound_ref[pl.program_id(0), pl.program_id(1) * group + pl.program_id(2)]

    def qk(c):
        src, start, size = chunks[c]
        k_c = kv_refs[2 * src][0, start:start + size, :]
        return lax.dot_general(k_c, q_ref[0], (((1,), (1,)), ((), ())),
                               preferred_element_type=F32)

    def vt(c):
        src, start, size = chunks[c]
        return kv_refs[2 * src + 1][0, 0, :, start:start + size]

    def bounded():
        acc = l = p_prev = None
        s_next = qk(0)
        for c in range(n):
            s = s_next
            if c + 1 < n:
                s_next = qk(c + 1)
            p = jnp.exp2(s - bound)
            psum = jnp.sum(p, axis=0, keepdims=True)
            l = psum if c == 0 else l + psum
            if c >= 1:
                y = _dot(vt(c - 1), p_prev)
                acc = y if c == 1 else acc + y
            p_prev = p.astype(BF16)
        y = _dot(vt(n - 1), p_prev)
        acc = y if n == 1 else acc + y
        o_ref[0] = (acc / l).T.astype(o_ref.dtype)

    def online():
        def qk_store(c):
            s = qk(c)
            s_buf[c % 2, 0:chunks[c][2], :] = s
            return jnp.max(s, axis=0, keepdims=True)

        def pv(c, alpha):
            y = _dot(vt(c), p_buf[c % 2, 0:chunks[c][2], :])
            if c == 0:
                acc_ref[...] = y
            else:
                acc_ref[...] = alpha * acc_ref[...] + y

        m_next = qk_store(0)
        m = l = alpha_prev = None
        for c in range(n):
            size = chunks[c][2]
            m_c = m_next
            if c + 1 < n:
                m_next = qk_store(c + 1)
            s = s_buf[c % 2, 0:size, :]
            if c == 0:
                m_new, alpha = m_c, None
            else:
                m_new = jnp.maximum(m, m_c)
                alpha = jnp.exp2(m - m_new)
            p = jnp.exp2(s - m_new)
            p_buf[c % 2, 0:size, :] = p.astype(BF16)
            psum = jnp.sum(p, axis=0, keepdims=True)
            l = psum if c == 0 else alpha * l + psum
            m = m_new
            if c >= 1:
                pv(c - 1, alpha_prev)
            alpha_prev = alpha
        pv(n - 1, alpha_prev)
        o_ref[0] = (acc_ref[...] / l).T.astype(o_ref.dtype)

    lax.cond(bound < SAFE_LOG2_SPAN, bounded, online)


def _attention(q, kv_sources, bound, n_heads, n_kv_heads, dq, dv, tq):
    B, Q, _ = q.shape
    G = n_heads // n_kv_heads
    chunks = []
    in_specs = [pl.BlockSpec((1, tq, dq), lambda b, hk, g, i, bnd: (b, i, hk * G + g))]
    args = [q]
    for src, (k, vt) in enumerate(kv_sources):
        T = k.shape[1]
        ck = min(ATTN_K_CHUNK, T)
        chunks += [(src, s, ck) for s in range(0, T, ck)]
        in_specs += [
            pl.BlockSpec((1, T, dq), lambda b, hk, g, i, bnd: (b, 0, hk)),
            pl.BlockSpec((1, 1, dv, T), lambda b, hk, g, i, bnd: (b, hk, 0, 0)),
        ]
        args += [k, vt]
    ck_max = max(size for _, _, size in chunks)
    return pl.pallas_call(
        functools.partial(_attn_kernel, chunks=tuple(chunks), n_src=len(kv_sources), group=G),
        grid_spec=pltpu.PrefetchScalarGridSpec(
            num_scalar_prefetch=1,
            grid=(B, n_kv_heads, G, Q // tq),
            in_specs=in_specs,
            out_specs=pl.BlockSpec((1, tq, dv), lambda b, hk, g, i, bnd: (b, i, hk * G + g)),
            scratch_shapes=[
                pltpu.VMEM((2, ck_max, tq), F32),
                pltpu.VMEM((2, ck_max, tq), BF16),
                pltpu.VMEM((dv, tq), F32),
            ],
        ),
        out_shape=jax.ShapeDtypeStruct((B, Q, n_heads * dv), BF16),
        compiler_params=_cparams(("parallel", "parallel", "parallel", "parallel")),
        name="attention",
    )(bound, *args)


def _score_bound(parts_q, parts_k, qscale, batch, n_heads):
    q2 = sum(n * jnp.max(g * g) for n, g in parts_q) * (qscale * qscale)
    k2 = sum(n * jnp.max(g * g) for n, g in parts_k)
    return jnp.full((batch, n_heads), jnp.sqrt(q2 * k2) * BOUND_MARGIN, F32)


def _oproj_kernel(o_ref, w_ref, x_ref, gate_ref, out_ref, *, col_tile):
    o = o_ref[0]
    gate = gate_ref[0]
    for c0 in range(0, out_ref.shape[-1], col_tile):
        y = _dot(o, w_ref[:, c0:c0 + col_tile])
        out_ref[0, :, c0:c0 + col_tile] = x_ref[0, :, c0:c0 + col_tile] + gate[:, c0:c0 + col_tile] * y


def _oproj(o, w_o, x, gate, tm):
    B, R, D = x.shape
    K = o.shape[-1]
    mod_map = (lambda b, i: (b, 0, 0)) if gate.shape[0] == B else (lambda b, i: (0, 0, 0))
    return pl.pallas_call(
        functools.partial(_oproj_kernel, col_tile=512),
        grid=(B, R // tm),
        in_specs=[
            pl.BlockSpec((1, tm, K), lambda b, i: (b, i, 0)),
            pl.BlockSpec((K, D), lambda b, i: (0, 0)),
            pl.BlockSpec((1, tm, D), lambda b, i: (b, i, 0)),
            pl.BlockSpec((1, 1, D), mod_map),
        ],
        out_specs=pl.BlockSpec((1, tm, D), lambda b, i: (b, i, 0)),
        out_shape=jax.ShapeDtypeStruct((B, R, D), F32),
        compiler_params=_cparams(("parallel", "parallel")),
        name="oproj",
    )(o, w_o, x, gate)


def _ffn_up_kernel(x_ref, g_ref, sh_ref, sc_ref, w_ref, u_ref, h_ref):
    @pl.when(pl.program_id(2) == 0)
    def _():
        tm, d = h_ref.shape
        inv = []
        for r0 in range(0, tm, FFN_NORM_ROW_STRIP):
            xs = x_ref[0, r0:r0 + FFN_NORM_ROW_STRIP, :]
            inv.append(lax.rsqrt(jnp.sum(xs * xs, axis=-1, keepdims=True) * (1.0 / d) + EPS))
        inv = jnp.concatenate(inv, axis=0)
        for c0 in range(0, d, FFN_NORM_COL_TILE):
            c1 = c0 + FFN_NORM_COL_TILE
            y = (x_ref[0, :, c0:c1] * inv) * g_ref[:, c0:c1]
            h_ref[:, c0:c1] = (y * (1.0 + sc_ref[0, :, c0:c1]) + sh_ref[0, :, c0:c1]).astype(BF16)

    u_ref[0] = _dot(h_ref[...], w_ref[...]).astype(u_ref.dtype)


def _ffn_up(x, g, shift, scale, w_up, tm, tn):
    B, R, D = x.shape
    N = w_up.shape[1]
    mod_map = (lambda b, i, j: (b, 0, 0)) if shift.shape[0] == B else (lambda b, i, j: (0, 0, 0))
    return pl.pallas_call(
        _ffn_up_kernel,
        grid=(B, R // tm, N // tn),
        in_specs=[
            pl.BlockSpec((1, tm, D), lambda b, i, j: (b, i, 0)),
            pl.BlockSpec((1, D), lambda b, i, j: (0, 0)),
            pl.BlockSpec((1, 1, D), mod_map),
            pl.BlockSpec((1, 1, D), mod_map),
            pl.BlockSpec((D, tn), lambda b, i, j: (0, j)),
        ],
        out_specs=pl.BlockSpec((1, tm, tn), lambda b, i, j: (b, i, j)),
        out_shape=jax.ShapeDtypeStruct((B, R, N), BF16),
        scratch_shapes=[pltpu.VMEM((tm, D), BF16)],
        compiler_params=_cparams(("parallel", "parallel", "arbitrary")),
        name="ffn_up",
    )(x, g, shift, scale, w_up)


def _ffn_down_kernel(ug_ref, uv_ref, prev_ref, next_ref, cw_ref, cb_ref, w_ref, x_ref, gate_ref,
                     out_ref):
    i = pl.program_id(1)
    kk = pl.program_id(2)
    tm = ug_ref.shape[1]
    first = i == 0
    last = i == pl.num_programs(1) - 1
    prev_row = jnp.where(first, 0.0, prev_ref[0, HALO - 1:HALO, :].astype(F32))
    next_row = jnp.where(last, 0.0, next_ref[0, 0:1, :].astype(F32))
    @pl.when(kk == 0)
    def _():
        out_ref[0] = jnp.zeros(out_ref.shape[1:], out_ref.dtype)

    cw = cw_ref[...]
    cb = cb_ref[...]
    rs = tm // FFN_DOWN_ROW_SPLIT
    for r0 in range(0, tm, rs):
        r1 = r0 + rs
        g = ug_ref[0, r0:r1, :].astype(F32)
        above = prev_row if r0 == 0 else ug_ref[0, r0 - 1:r0, :].astype(F32)
        below = next_row if r1 == tm else ug_ref[0, r1:r1 + 1, :].astype(F32)
        row = lax.broadcasted_iota(jnp.int32, g.shape, 0)
        g_prev = jnp.where(row == 0, above, pltpu.roll(g, 1, 0))
        g_next = jnp.where(row == rs - 1, below, pltpu.roll(g, rs - 1, 0))
        conv = g_prev * cw[0:1, :] + g * cw[1:2, :] + g_next * cw[2:3, :] + cb
        half = 0.5 * conv
        a = (half * (1.0 + jnp.tanh(half))) * uv_ref[0, r0:r1, :].astype(F32)
        out_ref[0, r0:r1, :] += _dot(a.astype(BF16), w_ref[...])

    @pl.when(kk == pl.num_programs(2) - 1)
    def _():
        out_ref[0] = x_ref[0] + gate_ref[0] * out_ref[0]


def _ffn_down(u, conv_w, conv_b, w_down, x, gate, tm, tk):
    B, R, D = x.shape
    F = w_down.shape[0]
    nk = F // tk
    rb = tm // HALO
    n_halo_blocks = R // HALO
    mod_map = (lambda b, i, k: (b, 0, 0)) if gate.shape[0] == B else (lambda b, i, k: (0, 0, 0))
    return pl.pallas_call(
        _ffn_down_kernel,
        grid=(B, R // tm, nk),
        in_specs=[
            pl.BlockSpec((1, tm, tk), lambda b, i, k: (b, i, k)),
            pl.BlockSpec((1, tm, tk), lambda b, i, k: (b, i, nk + k)),
            pl.BlockSpec((1, HALO, tk), lambda b, i, k: (b, jnp.maximum(i * rb - 1, 0), k)),
            pl.BlockSpec((1, HALO, tk),
                         lambda b, i, k: (b, jnp.minimum((i + 1) * rb, n_halo_blocks - 1), k)),
            pl.BlockSpec((CONV_W, tk), lambda b, i, k: (0, k)),
            pl.BlockSpec((1, tk), lambda b, i, k: (0, k)),
            pl.BlockSpec((tk, D), lambda b, i, k: (k, 0)),
            pl.BlockSpec((1, tm, D), lambda b, i, k: (b, i, 0)),
            pl.BlockSpec((1, 1, D), mod_map),
        ],
        out_specs=pl.BlockSpec((1, tm, D), lambda b, i, k: (b, i, 0)),
        out_shape=jax.ShapeDtypeStruct((B, R, D), F32),
        compiler_params=_cparams(("parallel", "parallel", "arbitrary")),
        name="ffn_down",
    )(u, u, u, u, conv_w, conv_b.reshape(1, F), w_down, x, gate)


def _rope_tables(seq, ctx_len, rot_dim):
    t = jnp.arange(seq, dtype=jnp.int32)
    rows = (t // GRID_W).astype(F32)
    cols = (t % GRID_W).astype(F32)
    axis_dim = rot_dim // 2
    inv = jnp.power(ROPE_BASE, -jnp.arange(0, axis_dim, 2, dtype=F32) / axis_dim)
    ang_r = rows[:, None] * inv
    ang_c = cols[:, None] * inv
    ang = jnp.concatenate([ang_r, ang_r, ang_c, ang_c], axis=-1)
    cos, sin = jnp.cos(ang), jnp.sin(ang)
    lane = jnp.arange(rot_dim)
    first_quarter = (lane % axis_dim) < (axis_dim // 2)
    sin_a = jnp.where(first_quarter, -sin, 0.0)
    sin_b = jnp.where(first_quarter, 0.0, sin)

    def finish(tab, ctx_value):
        tab = jnp.concatenate([tab, jnp.full((ctx_len, rot_dim), ctx_value, F32)], axis=0)
        return jnp.pad(tab, ((0, 0), (0, LANES - rot_dim)))

    return finish(cos, 1.0), finish(sin_a, 0.0), finish(sin_b, 0.0)


def _mla_weights(j, w_dq, g_dq, w_uq, g_q_nope, g_q_pe, w_dkv, g_dkv, g_k_pe, w_ukv, g_k_nope):
    D = w_dq.shape[1]
    pad = LANES - MLA_ROPE
    w1 = jnp.concatenate([w_dq[j], w_dkv[j], jnp.zeros((D, pad), F32)], axis=1).astype(BF16)
    g1 = jnp.concatenate([g_dq[j], g_dkv[j], g_k_pe[j], jnp.zeros((pad,), F32)])[None, :]
    wuq = w_uq[j].reshape(MLA_Q_RANK, MLA_HEADS, MLA_NOPE + MLA_ROPE)
    wuq = jnp.pad(wuq, ((0, 0), (0, 0), (0, MLA_QK_PAD - MLA_NOPE - MLA_ROPE)))
    wuq = wuq.reshape(MLA_Q_RANK, MLA_HEADS * MLA_QK_PAD).astype(BF16)
    gq = jnp.concatenate([g_q_nope[j], g_q_pe[j], jnp.zeros((pad,), F32)])[None, :]
    wukv = w_ukv[j].reshape(MLA_KV_RANK, MLA_HEADS, MLA_NOPE + MLA_V)
    wuk = wukv[:, :, :MLA_NOPE].reshape(MLA_KV_RANK, MLA_HEADS * MLA_NOPE).astype(BF16)
    wuvt = wukv[:, :, MLA_NOPE:].reshape(MLA_KV_RANK, MLA_HEADS * MLA_V).T.astype(BF16)
    return dict(w1=w1, g1=g1, wuq=wuq, gq=gq, wuk=wuk, wuvt=wuvt, gk=g_k_nope[j][None, :])


def _gqa_weights(j, w_q, g_q, w_kv, g_k):
    w = jnp.concatenate([w_q[j], w_kv[j]], axis=1).astype(BF16)
    return dict(w=w, gq=g_q[j][None, :], gk=g_k[j][None, :])


def kernel(x, c, ctx, c_ctx, w_mod, b_mod, norm_mix, norm_ffn, mla_w_dq, mla_g_dq, mla_w_uq, mla_g_q_nope, mla_g_q_pe, mla_w_dkv, mla_g_dkv, mla_g_k_pe, mla_w_ukv, mla_g_k_nope, mla_w_o, gqa_w_q, gqa_g_q, gqa_w_kv, gqa_g_k, gqa_w_o, ffn_w_up, ffn_conv_w, ffn_conv_b, ffn_w_down):
    B, S, D = x.shape
    C = ctx.shape[1]
    depth = w_mod.shape[0]
    assert S % FFN_UP_ROW_TILE == 0 and S % ROW_TILE == 0 and C % HALO == 0 and C <= ROW_TILE

    cvec = jnp.concatenate([c, c_ctx[None, :], jnp.zeros((8 - B - 1, D), F32)], axis=0)
    mod = _mod_all(cvec, w_mod, b_mod).reshape(depth, 8, N_MOD, D)
    rope_mla = _rope_tables(S, C, MLA_ROPE)
    rope_gqa = _rope_tables(S, C, GQA_HEAD_DIM)

    for i in range(depth):
        last = i == depth - 1
        j = i // 2
        lat = [mod[i, :B, n][:, None, :] for n in range(N_MOD)]
        cmod = [mod[i, B:B + 1, n][:, None, :] for n in range(N_MOD)]
        g_mix = norm_mix[i][None, :]
        g_ffn = norm_ffn[i][None, :]

        if i % 2 == 0:
            w = _mla_weights(j, mla_w_dq, mla_g_dq, mla_w_uq, mla_g_q_nope, mla_g_q_pe,
                             mla_w_dkv, mla_g_dkv, mla_g_k_pe, mla_w_ukv, mla_g_k_nope)
            qkv, rope = _mla_qkv, rope_mla
            bound = _score_bound([(MLA_NOPE, mla_g_q_nope[j]), (MLA_ROPE, mla_g_q_pe[j])],
                                 [(MLA_NOPE, mla_g_k_nope[j]), (MLA_ROPE, mla_g_k_pe[j])],
                                 MLA_SCALE * LOG2E, B, MLA_HEADS)
            w_o = mla_w_o[j].astype(BF16)
            heads, kv_heads, dq, dv = MLA_HEADS, MLA_HEADS, MLA_QK_PAD, MLA_V
        else:
            w = _gqa_weights(j, gqa_w_q, gqa_g_q, gqa_w_kv, gqa_g_k)
            qkv, rope = _gqa_qkv, rope_gqa
            bound = _score_bound([(GQA_HEAD_DIM, gqa_g_q[j])], [(GQA_HEAD_DIM, gqa_g_k[j])],
                                 GQA_SCALE * LOG2E, B, GQA_HEADS)
            w_o = gqa_w_o[j].astype(BF16)
            heads, kv_heads, dq, dv = GQA_HEADS, GQA_KV_HEADS, GQA_HEAD_DIM, GQA_HEAD_DIM

        q, k, vt = qkv(x, g_mix, lat[0], lat[1], w, rope, ROW_TILE, 0)
        qc, kc, vtc = qkv(ctx, g_mix, cmod[0], cmod[1], w, rope, C, S // C)
        o = _attention(q, [(k, vt), (kc, vtc)], bound, heads, kv_heads, dq, dv, ATTN_Q_TILE)
        x = _oproj(o, w_o, x, lat[2], ROW_TILE)
        if not last:
            oc = _attention(qc, [(kc, vtc)], bound, heads, kv_heads, dq, dv, C)
            ctx = _oproj(oc, w_o, ctx, cmod[2], C)

        w_up = ffn_w_up[i].astype(BF16)
        w_down = ffn_w_down[i].astype(BF16)
        u = _ffn_up(x, g_ffn, lat[3], lat[4], w_up, FFN_UP_ROW_TILE, FFN_UP_COL_TILE)
        x = _ffn_down(u, ffn_conv_w[i], ffn_conv_b[i], w_down, x, lat[5],
                      FFN_DOWN_ROW_TILE, FFN_DOWN_K_TILE)
        if not last:
            uc = _ffn_up(ctx.reshape(1, B * C, D), g_ffn, cmod[3], cmod[4], w_up, B * C,
                         FFN_UP_COL_TILE).reshape(B, C, -1)
            ctx = _ffn_down(uc, ffn_conv_w[i], ffn_conv_b[i], w_down, ctx, cmod[5],
                            C, FFN_DOWN_K_TILE)
    return x
```

```python
import functools
import math

import jax
import jax.numpy as jnp
from jax import lax
from jax.experimental import pallas as pl
from jax.experimental.pallas import tpu as pltpu

F32 = jnp.float32
BF16 = jnp.bfloat16

GRID_W = 64
N_MOD = 6
MLA_HEADS = 16
MLA_Q_RANK = 512
MLA_KV_RANK = 512
MLA_NOPE = 128
MLA_ROPE = 64
MLA_V = 128
MLA_QK_PAD = 256
GQA_HEADS = 16
GQA_KV_HEADS = 4
GQA_HEAD_DIM = 128
CONV_W = 3
ROPE_BASE = 10000.0
EPS = 1e-6
LOG2E = 1.4426950408889634
MLA_SCALE = 1.0 / math.sqrt(MLA_NOPE + MLA_ROPE)
GQA_SCALE = 1.0 / math.sqrt(GQA_HEAD_DIM)

SAFE_LOG2_SPAN = 60.0
BOUND_MARGIN = 1.02

LANES = 128
BF16_SUBLANES = 16
VMEM_LIMIT_BYTES = 56 * 1024 * 1024

ROW_TILE = 512
MLA_HEAD_GROUP = 4
FFN_UP_ROW_TILE = 1024
FFN_UP_COL_TILE = 1024
FFN_NORM_ROW_STRIP = 128
FFN_NORM_COL_TILE = 256
FFN_DOWN_ROW_TILE = 512
FFN_DOWN_K_TILE = 1408
FFN_DOWN_ROW_SPLIT = 2
ATTN_Q_TILE = 1024
ATTN_K_CHUNK = 2048
MOD_COL_TILE = 1024
HALO = BF16_SUBLANES


def _cparams(sem):
    return pltpu.CompilerParams(dimension_semantics=sem, vmem_limit_bytes=VMEM_LIMIT_BYTES)


def _dot(a, b):
    return jnp.dot(a, b, preferred_element_type=F32)


def _rms(y, n):
    return y * lax.rsqrt(jnp.sum(y * y, axis=-1, keepdims=True) * (1.0 / n) + EPS)


def _modulate(x, g, shift, scale):
    return (_rms(x, x.shape[-1]) * g) * (1.0 + scale) + shift


def _rope_half_apart(x, cos, sin_signed):
    return x * cos + pltpu.roll(x, LANES // 2, 1) * sin_signed


def _mod_kernel(c_ref, w_ref, b_ref, o_ref):
    c = c_ref[...]
    sc = (c * jax.nn.sigmoid(c)).astype(BF16)
    o_ref[0] = _dot(sc, w_ref[0].astype(BF16)) + b_ref[0]


def _mod_all(cvec, w_mod, b_mod):
    L, D, N = w_mod.shape
    R = cvec.shape[0]
    return pl.pallas_call(
        _mod_kernel,
        grid=(L, N // MOD_COL_TILE),
        in_specs=[
            pl.BlockSpec((R, D), lambda l, j: (0, 0)),
            pl.BlockSpec((1, D, MOD_COL_TILE), lambda l, j: (l, 0, j)),
            pl.BlockSpec((1, 1, MOD_COL_TILE), lambda l, j: (l, 0, j)),
        ],
        out_specs=pl.BlockSpec((1, R, MOD_COL_TILE), lambda l, j: (l, 0, j)),
        out_shape=jax.ShapeDtypeStruct((L, R, N), F32),
        compiler_params=_cparams(("parallel", "parallel")),
        name="mod_all",
    )(cvec, w_mod, b_mod.reshape(L, 1, N))


def _mla_qkv_kernel(x_ref, g_ref, sh_ref, sc_ref, w1_ref, g1_ref, wuq_ref, gq_ref,
                    wuk_ref, wuvt_ref, gk_ref, cos_ref, sr_ref,
                    q_ref, k_ref, vt_ref):
    h = _modulate(x_ref[0], g_ref[...], sh_ref[0], sc_ref[0]).astype(BF16)
    y1 = _dot(h, w1_ref[...])
    g1 = g1_ref[...]
    cq = (_rms(y1[:, :MLA_Q_RANK], MLA_Q_RANK) * g1[:, :MLA_Q_RANK]).astype(BF16)
    lo, hi = MLA_Q_RANK, MLA_Q_RANK + MLA_KV_RANK
    ckv = (_rms(y1[:, lo:hi], MLA_KV_RANK) * g1[:, lo:hi]).astype(BF16)
    cos, sr = cos_ref[...], sr_ref[...]
    kpe = _rms(y1[:, hi:], MLA_ROPE) * g1[:, hi:]
    kpe = _rope_half_apart(kpe, cos, sr).astype(BF16)
    gq = gq_ref[...]
    gk = gk_ref[...]
    qscale = MLA_SCALE * LOG2E

    hg = MLA_HEAD_GROUP
    gw = hg * MLA_QK_PAD

    kw = hg * MLA_NOPE

    def group_dots(gi):
        return (_dot(cq, wuq_ref[:, gi * gw:(gi + 1) * gw]),
                _dot(ckv, wuk_ref[:, gi * kw:(gi + 1) * kw]),
                lax.dot_general(wuvt_ref[gi * kw:(gi + 1) * kw, :], ckv, (((1,), (1,)), ((), ())),
                                preferred_element_type=F32))

    nxt = group_dots(0)
    for gi in range(MLA_HEADS // hg):
        yq, yk, yvt = nxt
        if gi + 1 < MLA_HEADS // hg:
            nxt = group_dots(gi + 1)
        offs = [j * MLA_QK_PAD for j in range(hg)]
        qn = [yq[:, o:o + MLA_NOPE] for o in offs]
        qp = [yq[:, o + MLA_NOPE:o + MLA_QK_PAD] for o in offs]
        kn = [yk[:, j * MLA_NOPE:(j + 1) * MLA_NOPE] for j in range(hg)]
        ssq = [jnp.sum(t * t, axis=-1, keepdims=True) for t in qn + qp + kn]
        widths = [MLA_NOPE] * hg + [MLA_ROPE] * hg + [MLA_NOPE] * hg
        inv = [lax.rsqrt(s * (1.0 / n) + EPS) for s, n in zip(ssq, widths)]
        qn = [t * r * (gq[:, :MLA_NOPE] * qscale) for t, r in zip(qn, inv[:hg])]
        qp = [t * r * (gq[:, MLA_NOPE:] * qscale) for t, r in zip(qp, inv[hg:2 * hg])]
        kn = [t * r * gk for t, r in zip(kn, inv[2 * hg:])]
        rolled = [pltpu.roll(t, LANES // 2, 1) for t in qp]
        qp = [t * cos + r * sr for t, r in zip(qp, rolled)]
        for j, o in enumerate(offs):
            c0 = gi * gw + o
            q_ref[0, :, c0:c0 + MLA_NOPE] = qn[j].astype(BF16)
            q_ref[0, :, c0 + MLA_NOPE:c0 + MLA_QK_PAD] = qp[j].astype(BF16)
            k_ref[0, :, c0:c0 + MLA_NOPE] = kn[j].astype(BF16)
            k_ref[0, :, c0 + MLA_NOPE:c0 + MLA_QK_PAD] = kpe
            vt_ref[0, gi * hg + j] = yvt[j * MLA_V:(j + 1) * MLA_V, :].astype(BF16)


def _mla_qkv(x, g, shift, scale, w, rope, tm, rope_blk0):
    B, R, D = x.shape
    H = MLA_HEADS
    nq = H * MLA_QK_PAD
    const2 = lambda b, i: (0, 0)
    mod_map = (lambda b, i: (b, 0, 0)) if shift.shape[0] == B else (lambda b, i: (0, 0, 0))
    rope_spec = pl.BlockSpec((tm, LANES), lambda b, i: (rope_blk0 + i, 0))
    return pl.pallas_call(
        _mla_qkv_kernel,
        grid=(B, R // tm),
        in_specs=[
            pl.BlockSpec((1, tm, D), lambda b, i: (b, i, 0)),
            pl.BlockSpec((1, D), const2),
            pl.BlockSpec((1, 1, D), mod_map),
            pl.BlockSpec((1, 1, D), mod_map),
            pl.BlockSpec(w["w1"].shape, const2),
            pl.BlockSpec(w["g1"].shape, const2),
            pl.BlockSpec(w["wuq"].shape, const2),
            pl.BlockSpec(w["gq"].shape, const2),
            pl.BlockSpec(w["wuk"].shape, const2),
            pl.BlockSpec(w["wuvt"].shape, const2),
            pl.BlockSpec(w["gk"].shape, const2),
            rope_spec, rope_spec,
        ],
        out_specs=[
            pl.BlockSpec((1, tm, nq), lambda b, i: (b, i, 0)),
            pl.BlockSpec((1, tm, nq), lambda b, i: (b, i, 0)),
            pl.BlockSpec((1, H, MLA_V, tm), lambda b, i: (b, 0, 0, i)),
        ],
        out_shape=[
            jax.ShapeDtypeStruct((B, R, nq), BF16),
            jax.ShapeDtypeStruct((B, R, nq), BF16),
            jax.ShapeDtypeStruct((B, H, MLA_V, R), BF16),
        ],
        compiler_params=_cparams(("parallel", "parallel")),
        name="mla_qkv",
    )(x, g, shift, scale, w["w1"], w["g1"], w["wuq"], w["gq"], w["wuk"], w["wuvt"], w["gk"], *rope)


def _gqa_qkv_kernel(x_ref, g_ref, sh_ref, sc_ref, w_ref, gq_ref, gk_ref,
                    cos_ref, sa_ref, sb_ref, q_ref, k_ref, vt_ref):
    h = _modulate(x_ref[0], g_ref[...], sh_ref[0], sc_ref[0]).astype(BF16)
    cos, sa, sb = cos_ref[...], sa_ref[...], sb_ref[...]
    quarter = GQA_HEAD_DIM // 4
    hd = GQA_HEAD_DIM
    group = GQA_KV_HEADS * hd
    nq = GQA_HEADS * hd
    gq = gq_ref[...] * (GQA_SCALE * LOG2E)
    gk = gk_ref[...]
    n_groups = (nq + 2 * group) // group
    heads = range(0, group, hd)

    def norm_rope_group(y, gain):
        tiles = [y[:, j:j + hd] for j in heads]
        ssq = [jnp.sum(t * t, axis=-1, keepdims=True) for t in tiles]
        inv = [lax.rsqrt(s * (1.0 / hd) + EPS) for s in ssq]
        xs = [t * r * gain for t, r in zip(tiles, inv)]
        ra = [pltpu.roll(x, LANES - quarter, 1) for x in xs]
        rb = [pltpu.roll(x, quarter, 1) for x in xs]
        return [(x * cos + a * sa + b * sb).astype(BF16) for x, a, b in zip(xs, ra, rb)]

    y_next = _dot(h, w_ref[:, 0:group])
    for gi in range(n_groups):
        c0 = gi * group
        y = y_next
        if gi + 1 < n_groups:
            y_next = _dot(h, w_ref[:, c0 + group:c0 + 2 * group])
        if c0 < nq:
            for j, qh in zip(heads, norm_rope_group(y, gq)):
                q_ref[0, :, c0 + j:c0 + j + hd] = qh
        elif c0 == nq:
            for j, kh in zip(heads, norm_rope_group(y, gk)):
                k_ref[0, :, j:j + hd] = kh
        else:
            vts = [y[:, j:j + hd].T.astype(BF16) for j in heads]
            for j, v in zip(heads, vts):
                vt_ref[0, j // hd] = v


def _gqa_qkv(x, g, shift, scale, w, rope, tm, rope_blk0):
    B, R, D = x.shape
    nq = GQA_HEADS * GQA_HEAD_DIM
    nk = GQA_KV_HEADS * GQA_HEAD_DIM
    const2 = lambda b, i: (0, 0)
    mod_map = (lambda b, i: (b, 0, 0)) if shift.shape[0] == B else (lambda b, i: (0, 0, 0))
    rope_spec = pl.BlockSpec((tm, LANES), lambda b, i: (rope_blk0 + i, 0))
    return pl.pallas_call(
        _gqa_qkv_kernel,
        grid=(B, R // tm),
        in_specs=[
            pl.BlockSpec((1, tm, D), lambda b, i: (b, i, 0)),
            pl.BlockSpec((1, D), const2),
            pl.BlockSpec((1, 1, D), mod_map),
            pl.BlockSpec((1, 1, D), mod_map),
            pl.BlockSpec(w["w"].shape, const2),
            pl.BlockSpec(w["gq"].shape, const2),
            pl.BlockSpec(w["gk"].shape, const2),
            rope_spec, rope_spec, rope_spec,
        ],
        out_specs=[
            pl.BlockSpec((1, tm, nq), lambda b, i: (b, i, 0)),
            pl.BlockSpec((1, tm, nk), lambda b, i: (b, i, 0)),
            pl.BlockSpec((1, GQA_KV_HEADS, GQA_HEAD_DIM, tm), lambda b, i: (b, 0, 0, i)),
        ],
        out_shape=[
            jax.ShapeDtypeStruct((B, R, nq), BF16),
            jax.ShapeDtypeStruct((B, R, nk), BF16),
            jax.ShapeDtypeStruct((B, GQA_KV_HEADS, GQA_HEAD_DIM, R), BF16),
        ],
        compiler_params=_cparams(("parallel", "parallel")),
        name="gqa_qkv",
    )(x, g, shift, scale, w["w"], w["gq"], w["gk"], *rope)


def _attn_kernel(bound_ref, *refs, chunks, n_src, group):
    q_ref = refs[0]
    kv_refs = refs[1:1 + 2 * n_src]
    o_ref = refs[1 + 2 * n_src]
    s_buf, p_buf, acc_ref = refs[2 + 2 * n_src:]
    n = len(chunks)
    bound = bound_ref[pl.program_id(0), pl.program_id(1) * group + pl.program_id(2)]

    def qk(c):
        src, start, size = chunks[c]
        k_c = kv_refs[2 * src][0, start:start + size, :]
        return lax.dot_general(k_c, q_ref[0], (((1,), (1,)), ((), ())),
                               preferred_element_type=F32)

    def vt(c):
        src, start, size = chunks[c]
        return kv_refs[2 * src + 1][0, 0, :, start:start + size]

    def bounded():
        acc = l = p_prev = None
        s_next = qk(0)
        for c in range(n):
            s = s_next
            if c + 1 < n:
                s_next = qk(c + 1)
            p = jnp.exp2(s - bound)
            psum = jnp.sum(p, axis=0, keepdims=True)
            l = psum if c == 0 else l + psum
            if c >= 1:
                y = _dot(vt(c - 1), p_prev)
                acc = y if c == 1 else acc + y
            p_prev = p.astype(BF16)
        y = _dot(vt(n - 1), p_prev)
        acc = y if n == 1 else acc + y
        o_ref[0] = (acc / l).T.astype(o_ref.dtype)

    def online():
        def qk_store(c):
            s = qk(c)
            s_buf[c % 2, 0:chunks[c][2], :] = s
            return jnp.max(s, axis=0, keepdims=True)

        def pv(c, alpha):
            y = _dot(vt(c), p_buf[c % 2, 0:chunks[c][2], :])
            if c == 0:
                acc_ref[...] = y
            else:
                acc_ref[...] = alpha * acc_ref[...] + y

        m_next = qk_store(0)
        m = l = alpha_prev = None
        for c in range(n):
            size = chunks[c][2]
            m_c = m_next
            if c + 1 < n:
                m_next = qk_store(c + 1)
            s = s_buf[c % 2, 0:size, :]
            if c == 0:
                m_new, alpha = m_c, None
            else:
                m_new = jnp.maximum(m, m_c)
                alpha = jnp.exp2(m - m_new)
            p = jnp.exp2(s - m_new)
            p_buf[c % 2, 0:size, :] = p.astype(BF16)
            psum = jnp.sum(p, axis=0, keepdims=True)
            l = psum if c == 0 else alpha * l + psum
            m = m_new
            if c >= 1:
                pv(c - 1, alpha_prev)
            alpha_prev = alpha
        pv(n - 1, alpha_prev)
        o_ref[0] = (acc_ref[...] / l).T.astype(o_ref.dtype)

    lax.cond(bound < SAFE_LOG2_SPAN, bounded, online)


def _attention(q, kv_sources, bound, n_heads, n_kv_heads, dq, dv, tq):
    B, Q, _ = q.shape
    G = n_heads // n_kv_heads
    chunks = []
    in_specs = [pl.BlockSpec((1, tq, dq), lambda b, hk, g, i, bnd: (b, i, hk * G + g))]
    args = [q]
    for src, (k, vt) in enumerate(kv_sources):
        T = k.shape[1]
        ck = min(ATTN_K_CHUNK, T)
        chunks += [(src, s, ck) for s in range(0, T, ck)]
        in_specs += [
            pl.BlockSpec((1, T, dq), lambda b, hk, g, i, bnd: (b, 0, hk)),
            pl.BlockSpec((1, 1, dv, T), lambda b, hk, g, i, bnd: (b, hk, 0, 0)),
        ]
        args += [k, vt]
    ck_max = max(size for _, _, size in chunks)
    return pl.pallas_call(
        functools.partial(_attn_kernel, chunks=tuple(chunks), n_src=len(kv_sources), group=G),
        grid_spec=pltpu.PrefetchScalarGridSpec(
            num_scalar_prefetch=1,
            grid=(B, n_kv_heads, G, Q // tq),
            in_specs=in_specs,
            out_specs=pl.BlockSpec((1, tq, dv), lambda b, hk, g, i, bnd: (b, i, hk * G + g)),
            scratch_shapes=[
                pltpu.VMEM((2, ck_max, tq), F32),
                pltpu.VMEM((2, ck_max, tq), BF16),
                pltpu.VMEM((dv, tq), F32),
            ],
        ),
        out_shape=jax.ShapeDtypeStruct((B, Q, n_heads * dv), BF16),
        compiler_params=_cparams(("parallel", "parallel", "parallel", "parallel")),
        name="attention",
    )(bound, *args)


def _score_bound(parts_q, parts_k, qscale, batch, n_heads):
    q2 = sum(n * jnp.max(g * g) for n, g in parts_q) * (qscale * qscale)
    k2 = sum(n * jnp.max(g * g) for n, g in parts_k)
    return jnp.full((batch, n_heads), jnp.sqrt(q2 * k2) * BOUND_MARGIN, F32)


def _oproj_kernel(o_ref, w_ref, x_ref, gate_ref, out_ref, *, col_tile):
    o = o_ref[0]
    gate = gate_ref[0]
    for c0 in range(0, out_ref.shape[-1], col_tile):
        y = _dot(o, w_ref[:, c0:c0 + col_tile])
        out_ref[0, :, c0:c0 + col_tile] = x_ref[0, :, c0:c0 + col_tile] + gate[:, c0:c0 + col_tile] * y


def _oproj(o, w_o, x, gate, tm):
    B, R, D = x.shape
    K = o.shape[-1]
    mod_map = (lambda b, i: (b, 0, 0)) if gate.shape[0] == B else (lambda b, i: (0, 0, 0))
    return pl.pallas_call(
        functools.partial(_oproj_kernel, col_tile=512),
        grid=(B, R // tm),
        in_specs=[
            pl.BlockSpec((1, tm, K), lambda b, i: (b, i, 0)),
            pl.BlockSpec((K, D), lambda b, i: (0, 0)),
            pl.BlockSpec((1, tm, D), lambda b, i: (b, i, 0)),
            pl.BlockSpec((1, 1, D), mod_map),
        ],
        out_specs=pl.BlockSpec((1, tm, D), lambda b, i: (b, i, 0)),
        out_shape=jax.ShapeDtypeStruct((B, R, D), F32),
        compiler_params=_cparams(("parallel", "parallel")),
        name="oproj",
    )(o, w_o, x, gate)


def _ffn_up_kernel(x_ref, g_ref, sh_ref, sc_ref, w_ref, u_ref, h_ref):
    @pl.when(pl.program_id(2) == 0)
    def _():
        tm, d = h_ref.shape
        inv = []
        for r0 in range(0, tm, FFN_NORM_ROW_STRIP):
            xs = x_ref[0, r0:r0 + FFN_NORM_ROW_STRIP, :]
            inv.append(lax.rsqrt(jnp.sum(xs * xs, axis=-1, keepdims=True) * (1.0 / d) + EPS))
        inv = jnp.concatenate(inv, axis=0)
        for c0 in range(0, d, FFN_NORM_COL_TILE):
            c1 = c0 + FFN_NORM_COL_TILE
            y = (x_ref[0, :, c0:c1] * inv) * g_ref[:, c0:c1]
            h_ref[:, c0:c1] = (y * (1.0 + sc_ref[0, :, c0:c1]) + sh_ref[0, :, c0:c1]).astype(BF16)

    u_ref[0] = _dot(h_ref[...], w_ref[...]).astype(u_ref.dtype)


def _ffn_up(x, g, shift, scale, w_up, tm, tn):
    B, R, D = x.shape
    N = w_up.shape[1]
    mod_map = (lambda b, i, j: (b, 0, 0)) if shift.shape[0] == B else (lambda b, i, j: (0, 0, 0))
    return pl.pallas_call(
        _ffn_up_kernel,
        grid=(B, R // tm, N // tn),
        in_specs=[
            pl.BlockSpec((1, tm, D), lambda b, i, j: (b, i, 0)),
            pl.BlockSpec((1, D), lambda b, i, j: (0, 0)),
            pl.BlockSpec((1, 1, D), mod_map),
            pl.BlockSpec((1, 1, D), mod_map),
            pl.BlockSpec((D, tn), lambda b, i, j: (0, j)),
        ],
        out_specs=pl.BlockSpec((1, tm, tn), lambda b, i, j: (b, i, j)),
        out_shape=jax.ShapeDtypeStruct((B, R, N), BF16),
        scratch_shapes=[pltpu.VMEM((tm, D), BF16)],
        compiler_params=_cparams(("parallel", "parallel", "arbitrary")),
        name="ffn_up",
    )(x, g, shift, scale, w_up)


def _ffn_down_kernel(ug_ref, uv_ref, prev_ref, next_ref, cw_ref, cb_ref, w_ref, x_ref, gate_ref,
                     out_ref):
    i = pl.program_id(1)
    kk = pl.program_id(2)
    tm = ug_ref.shape[1]
    first = i == 0
    last = i == pl.num_programs(1) - 1
    prev_row = jnp.where(first, 0.0, prev_ref[0, HALO - 1:HALO, :].astype(F32))
    next_row = jnp.where(last, 0.0, next_ref[0, 0:1, :].astype(F32))
    @pl.when(kk == 0)
    def _():
        out_ref[0] = jnp.zeros(out_ref.shape[1:], out_ref.dtype)

    cw = cw_ref[...]
    cb = cb_ref[...]
    rs = tm // FFN_DOWN_ROW_SPLIT
    for r0 in range(0, tm, rs):
        r1 = r0 + rs
        g = ug_ref[0, r0:r1, :].astype(F32)
        above = prev_row if r0 == 0 else ug_ref[0, r0 - 1:r0, :].astype(F32)
        below = next_row if r1 == tm else ug_ref[0, r1:r1 + 1, :].astype(F32)
        row = lax.broadcasted_iota(jnp.int32, g.shape, 0)
        g_prev = jnp.where(row == 0, above, pltpu.roll(g, 1, 0))
        g_next = jnp.where(row == rs - 1, below, pltpu.roll(g, rs - 1, 0))
        conv = g_prev * cw[0:1, :] + g * cw[1:2, :] + g_next * cw[2:3, :] + cb
        half = 0.5 * conv
        a = (half * (1.0 + jnp.tanh(half))) * uv_ref[0, r0:r1, :].astype(F32)
        out_ref[0, r0:r1, :] += _dot(a.astype(BF16), w_ref[...])

    @pl.when(kk == pl.num_programs(2) - 1)
    def _():
        out_ref[0] = x_ref[0] + gate_ref[0] * out_ref[0]


def _ffn_down(u, conv_w, conv_b, w_down, x, gate, tm, tk):
    B, R, D = x.shape
    F = w_down.shape[0]
    nk = F // tk
    rb = tm // HALO
    n_halo_blocks = R // HALO
    mod_map = (lambda b, i, k: (b, 0, 0)) if gate.shape[0] == B else (lambda b, i, k: (0, 0, 0))
    return pl.pallas_call(
        _ffn_down_kernel,
        grid=(B, R // tm, nk),
        in_specs=[
            pl.BlockSpec((1, tm, tk), lambda b, i, k: (b, i, k)),
            pl.BlockSpec((1, tm, tk), lambda b, i, k: (b, i, nk + k)),
            pl.BlockSpec((1, HALO, tk), lambda b, i, k: (b, jnp.maximum(i * rb - 1, 0), k)),
            pl.BlockSpec((1, HALO, tk),
                         lambda b, i, k: (b, jnp.minimum((i + 1) * rb, n_halo_blocks - 1), k)),
            pl.BlockSpec((CONV_W, tk), lambda b, i, k: (0, k)),
            pl.BlockSpec((1, tk), lambda b, i, k: (0, k)),
            pl.BlockSpec((tk, D), lambda b, i, k: (k, 0)),
            pl.BlockSpec((1, tm, D), lambda b, i, k: (b, i, 0)),
            pl.BlockSpec((1, 1, D), mod_map),
        ],
        out_specs=pl.BlockSpec((1, tm, D), lambda b, i, k: (b, i, 0)),
        out_shape=jax.ShapeDtypeStruct((B, R, D), F32),
        compiler_params=_cparams(("parallel", "parallel", "arbitrary")),
        name="ffn_down",
    )(u, u, u, u, conv_w, conv_b.reshape(1, F), w_down, x, gate)


def _rope_tables(seq, ctx_len, rot_dim):
    t = jnp.arange(seq, dtype=jnp.int32)
    rows = (t // GRID_W).astype(F32)
    cols = (t % GRID_W).astype(F32)
    axis_dim = rot_dim // 2
    inv = jnp.power(ROPE_BASE, -jnp.arange(0, axis_dim, 2, dtype=F32) / axis_dim)
    ang_r = rows[:, None] * inv
    ang_c = cols[:, None] * inv
    ang = jnp.concatenate([ang_r, ang_r, ang_c, ang_c], axis=-1)
    cos, sin = jnp.cos(ang), jnp.sin(ang)
    lane = jnp.arange(rot_dim)
    first_quarter = (lane % axis_dim) < (axis_dim // 2)
    sin_a = jnp.where(first_quarter, -sin, 0.0)
    sin_b = jnp.where(first_quarter, 0.0, sin)

    def finish(tab, ctx_value):
        tab = jnp.concatenate([tab, jnp.full((ctx_len, rot_dim), ctx_value, F32)], axis=0)
        return jnp.pad(tab, ((0, 0), (0, LANES - rot_dim)))

    return finish(cos, 1.0), finish(sin_a, 0.0), finish(sin_b, 0.0)


_PE_SRC = tuple(range(0, 16)) + tuple(range(32, 48)) + (0,) * 32 \
    + tuple(range(16, 32)) + tuple(range(48, 64)) + (0,) * 32
_PE_VALID = (True,) * 32 + (False,) * 32 + (True,) * 32 + (False,) * 32


def _pe_lanes(a):
    return jnp.where(jnp.array(_PE_VALID), jnp.take(a, jnp.array(_PE_SRC), axis=-1), 0.0)


def _rope_tables_mla(seq, ctx_len):
    cos, sin_a, sin_b = _rope_tables(seq, ctx_len, MLA_ROPE)
    cos, sin = cos[:, :MLA_ROPE], (sin_a + sin_b)[:, :MLA_ROPE]
    return _pe_lanes(cos), _pe_lanes(sin)


def _mla_weights(j, w_dq, g_dq, w_uq, g_q_nope, g_q_pe, w_dkv, g_dkv, g_k_pe, w_ukv, g_k_nope):
    D = w_dq.shape[1]
    w1 = jnp.concatenate([w_dq[j], w_dkv[j][:, :MLA_KV_RANK], _pe_lanes(w_dkv[j][:, MLA_KV_RANK:])],
                         axis=1).astype(BF16)
    g1 = jnp.concatenate([g_dq[j], g_dkv[j], _pe_lanes(g_k_pe[j])])[None, :]
    wuq = w_uq[j].reshape(MLA_Q_RANK, MLA_HEADS, MLA_NOPE + MLA_ROPE)
    wuq = jnp.concatenate([wuq[:, :, :MLA_NOPE], _pe_lanes(wuq[:, :, MLA_NOPE:])], axis=-1)
    wuq = wuq.reshape(MLA_Q_RANK, MLA_HEADS * MLA_QK_PAD).astype(BF16)
    gq = jnp.concatenate([g_q_nope[j], _pe_lanes(g_q_pe[j])])[None, :]
    wukv = w_ukv[j].reshape(MLA_KV_RANK, MLA_HEADS, MLA_NOPE + MLA_V)
    wuk = wukv[:, :, :MLA_NOPE].reshape(MLA_KV_RANK, MLA_HEADS * MLA_NOPE).astype(BF16)
    wuvt = wukv[:, :, MLA_NOPE:].reshape(MLA_KV_RANK, MLA_HEADS * MLA_V).T.astype(BF16)
    return dict(w1=w1, g1=g1, wuq=wuq, gq=gq, wuk=wuk, wuvt=wuvt, gk=g_k_nope[j][None, :])


def _gqa_weights(j, w_q, g_q, w_kv, g_k):
    w = jnp.concatenate([w_q[j], w_kv[j]], axis=1).astype(BF16)
    return dict(w=w, gq=g_q[j][None, :], gk=g_k[j][None, :])


def kernel(x, c, ctx, c_ctx, w_mod, b_mod, norm_mix, norm_ffn, mla_w_dq, mla_g_dq, mla_w_uq, mla_g_q_nope, mla_g_q_pe, mla_w_dkv, mla_g_dkv, mla_g_k_pe, mla_w_ukv, mla_g_k_nope, mla_w_o, gqa_w_q, gqa_g_q, gqa_w_kv, gqa_g_k, gqa_w_o, ffn_w_up, ffn_conv_w, ffn_conv_b, ffn_w_down):
    B, S, D = x.shape
    C = ctx.shape[1]
    depth = w_mod.shape[0]
    assert S % FFN_UP_ROW_TILE == 0 and S % ROW_TILE == 0 and C % HALO == 0 and C <= ROW_TILE

    cvec = jnp.concatenate([c, c_ctx[None, :], jnp.zeros((8 - B - 1, D), F32)], axis=0)
    mod = _mod_all(cvec, w_mod, b_mod).reshape(depth, 8, N_MOD, D)
    rope_mla = _rope_tables_mla(S, C)
    rope_gqa = _rope_tables(S, C, GQA_HEAD_DIM)

    for i in range(depth):
        last = i == depth - 1
        j = i // 2
        lat = [mod[i, :B, n][:, None, :] for n in range(N_MOD)]
        cmod = [mod[i, B:B + 1, n][:, None, :] for n in range(N_MOD)]
        g_mix = norm_mix[i][None, :]
        g_ffn = norm_ffn[i][None, :]

        if i % 2 == 0:
            w = _mla_weights(j, mla_w_dq, mla_g_dq, mla_w_uq, mla_g_q_nope, mla_g_q_pe,
                             mla_w_dkv, mla_g_dkv, mla_g_k_pe, mla_w_ukv, mla_g_k_nope)
            qkv, rope = _mla_qkv, rope_mla
            bound = _score_bound([(MLA_NOPE, mla_g_q_nope[j]), (MLA_ROPE, mla_g_q_pe[j])],
                                 [(MLA_NOPE, mla_g_k_nope[j]), (MLA_ROPE, mla_g_k_pe[j])],
                                 MLA_SCALE * LOG2E, B, MLA_HEADS)
            w_o = mla_w_o[j].astype(BF16)
            heads, kv_heads, dq, dv = MLA_HEADS, MLA_HEADS, MLA_QK_PAD, MLA_V
        else:
            w = _gqa_weights(j, gqa_w_q, gqa_g_q, gqa_w_kv, gqa_g_k)
            qkv, rope = _gqa_qkv, rope_gqa
            bound = _score_bound([(GQA_HEAD_DIM, gqa_g_q[j])], [(GQA_HEAD_DIM, gqa_g_k[j])],
                                 GQA_SCALE * LOG2E, B, GQA_HEADS)
            w_o = gqa_w_o[j].astype(BF16)
            heads, kv_heads, dq, dv = GQA_HEADS, GQA_KV_HEADS, GQA_HEAD_DIM, GQA_HEAD_DIM

        q, k, vt = qkv(x, g_mix, lat[0], lat[1], w, rope, ROW_TILE, 0)
        qc, kc, vtc = qkv(ctx, g_mix, cmod[0], cmod[1], w, rope, C, S // C)
        o = _attention(q, [(k, vt), (kc, vtc)], bound, heads, kv_heads, dq, dv, ATTN_Q_TILE)
        x = _oproj(o, w_o, x, lat[2], ROW_TILE)
        if not last:
            oc = _attention(qc, [(kc, vtc)], bound, heads, kv_heads, dq, dv, C)
            ctx = _oproj(oc, w_o, ctx, cmod[2], C)

        w_up = ffn_w_up[i].astype(BF16)
        w_down = ffn_w_down[i].astype(BF16)
        u = _ffn_up(x, g_ffn, lat[3], lat[4], w_up, FFN_UP_ROW_TILE, FFN_UP_COL_TILE)
        x = _ffn_down(u, ffn_conv_w[i], ffn_conv_b[i], w_down, x, lat[5],
                      FFN_DOWN_ROW_TILE, FFN_DOWN_K_TILE)
        if not last:
            uc = _ffn_up(ctx.reshape(1, B * C, D), g_ffn, cmod[3], cmod[4], w_up, B * C,
                         FFN_UP_COL_TILE).reshape(B, C, -1)
            ctx = _ffn_down(uc, ffn_conv_w[i], ffn_conv_b[i], w_down, ctx, cmod[5],
                            C, FFN_DOWN_K_TILE)
    return x
```
